```python
import math
import numpy as np
import jax
import jax.numpy as jnp
from jax import lax

D_MODEL = 1024
BATCH = 2
SEQ = 8192
DEPTH = 2

NORM_EPS = 1e-6
N_MOD = 6
Q_BLOCK = 128
NEG_INF = -1e30

SSD_HEADS = 16
SSD_HEAD_DIM = 64
SSD_INNER = SSD_HEADS * SSD_HEAD_DIM
SSD_STATE = 128
SSD_GROUPS = 2
SSD_CONV = 4
SSD_CONV_CH = SSD_INNER + 2 * SSD_GROUPS * SSD_STATE
SSD_CHUNK = 128

MLA_HEADS = 8
MLA_Q_RANK = 384
MLA_KV_RANK = 256
MLA_NOPE = 64
MLA_ROPE = 32
MLA_V = 64
ROPE_THETA = 10000.0

GLA_HEADS = 4
GLA_DK = 128
GLA_DV = 256
GLA_GATE_RANK = 16
GLA_GATE_NORM = 16.0
GLA_CHUNK = 64

NSA_HEADS = 8
NSA_GROUPS = 2
NSA_HEAD_DIM = 64
NSA_CMP_BLOCK = 32
NSA_CMP_STRIDE = 16
NSA_CMP_HIDDEN = 256
NSA_SEL_BLOCK = 64
NSA_N_SEL = 16
NSA_WINDOW = 512
NSA_FORCE = 1e4

FFN_HIDDEN = -(-8 * D_MODEL // (3 * 256)) * 256

L0_SIZES = (SSD_INNER, SSD_CONV_CH, SSD_HEADS, MLA_Q_RANK, MLA_KV_RANK, MLA_ROPE)
L0_IN = sum(L0_SIZES)
L0_CAT = SSD_INNER + MLA_HEADS * MLA_V
L1_SIZES = (GLA_HEADS * GLA_DK, GLA_HEADS * GLA_DK, GLA_HEADS * GLA_DV, GLA_GATE_RANK,
            GLA_HEADS * GLA_DV, NSA_HEADS * NSA_HEAD_DIM, 6 * NSA_GROUPS * NSA_HEAD_DIM, 3 * NSA_HEADS)
L1_IN = sum(L1_SIZES)
L1_CAT = GLA_HEADS * GLA_DV + NSA_HEADS * NSA_HEAD_DIM

kernel_name = 'hybrid_ssd_mla_gla_nsa_trunk'


def rmsnorm(x, g):
    xf = x.astype(jnp.float32)
    y = xf * lax.rsqrt(jnp.mean(xf * xf, axis=-1, keepdims=True) + NORM_EPS)
    return (y * g.astype(jnp.float32)).astype(x.dtype)


def split_cols(h, sizes):
    return jnp.split(h, [int(v) for v in np.cumsum(sizes)[:-1]], axis=-1)


def masked_softmax(s, mask):
    s = jnp.where(mask, s.astype(jnp.float32), NEG_INF)
    m = jnp.max(s, axis=-1, keepdims=True)
    e = jnp.where(mask, jnp.exp(s - m), 0.0)
    return e / jnp.maximum(jnp.sum(e, axis=-1, keepdims=True), 1e-30)


def alibi_slopes(n):
    return jnp.asarray(2.0 ** (-8.0 * np.arange(1, n + 1) / n), jnp.float32)


def rope_cos_sin(positions, dim):
    inv = jnp.asarray(ROPE_THETA ** (-np.arange(0, dim, 2) / dim), jnp.float32)
    ang = positions.astype(jnp.float32)[..., None] * inv
    return jnp.cos(ang), jnp.sin(ang)


def apply_rope(x, cos, sin):
    x1, x2 = jnp.split(x, 2, axis=-1)
    return jnp.concatenate([x1 * cos - x2 * sin, x2 * cos + x1 * sin], axis=-1).astype(x.dtype)


def causal_depthwise_conv(x, w, b):
    k_w, ch = w.shape
    y = lax.conv_general_dilated(x, w[:, None, :].astype(x.dtype), window_strides=(1,),
                                 padding=[(k_w - 1, 0)], dimension_numbers=('NWC', 'WIO', 'NWC'),
                                 feature_group_count=ch)
    return y + b


def segsum(a):
    n = a.shape[-1]
    cs = jnp.cumsum(a, axis=-1)
    diff = cs[..., :, None] - cs[..., None, :]
    return jnp.where(jnp.tril(jnp.ones((n, n), bool)), diff, -jnp.inf)


def ssd_chunked(x, dt, a, bmat, cmat):
    b, t, nh, p = x.shape
    g, n = bmat.shape[2], bmat.shape[3]
    hg = nh // g
    L = SSD_CHUNK
    c = t // L
    xd = (x.astype(jnp.float32) * dt[..., None]).reshape(b, c, L, g, hg, p)
    adt = (dt * a).reshape(b, c, L, g, hg).transpose(0, 3, 4, 1, 2)
    bm = bmat.astype(jnp.float32).reshape(b, c, L, g, n)
    cm = cmat.astype(jnp.float32).reshape(b, c, L, g, n)
    a_cs = jnp.cumsum(adt, axis=-1)
    lmat = jnp.exp(segsum(adt))
    y_diag = jnp.einsum('bclgn,bcsgn,bghcls,bcsghp->bclghp', cm, bm, lmat, xd)
    decay_states = jnp.exp(a_cs[..., -1:] - a_cs)
    states = jnp.einsum('bcsgn,bghcs,bcsghp->bcghpn', bm, decay_states, xd)
    states = jnp.concatenate([jnp.zeros_like(states[:, :1]), states], axis=1)
    a_chunk = jnp.pad(a_cs[..., -1], ((0, 0), (0, 0), (0, 0), (1, 0)))
    decay_chunk = jnp.exp(segsum(a_chunk))
    states = jnp.einsum('bghzc,bcghpn->bzghpn', decay_chunk, states)[:, :-1]
    y_off = jnp.einsum('bclgn,bcghpn,bghcl->bclghp', cm, states, jnp.exp(a_cs))
    return (y_diag + y_off).reshape(b, t, nh, p)


def gla_chunked(q, k, v, log_a):
    b, t, nh, dk = q.shape
    dv = v.shape[-1]
    L = GLA_CHUNK
    c = t // L

    def to_chunks(z):
        return z.astype(jnp.float32).reshape(b, c, L, nh, -1).transpose(1, 0, 3, 2, 4)

    qc = to_chunks(q) * (dk ** -0.5)
    kc, vc = to_chunks(k), to_chunks(v)
    bc = jnp.cumsum(to_chunks(log_a), axis=3)
    causal = jnp.tril(jnp.ones((L, L), bool))[:, :, None]

    def step(state, inp):
        qi, ki, vi, bi = inp
        o_inter = jnp.einsum('bhtk,bhkv->bhtv', qi * jnp.exp(bi), state)
        decay = jnp.exp(jnp.where(causal, bi[:, :, :, None, :] - bi[:, :, None, :, :], -jnp.inf))
        att = jnp.einsum('bhtk,bhsk,bhtsk->bhts', qi, ki, decay)
        o = o_inter + jnp.einsum('bhts,bhsv->bhtv', att, vi)
        b_last = bi[:, :, -1]
        state = jnp.exp(b_last)[..., None] * state + jnp.einsum(
            'bhsk,bhsv->bhkv', ki * jnp.exp(b_last[:, :, None] - bi), vi)
        return state, o

    s0 = jnp.zeros((b, nh, dk, dv), jnp.float32)
    _, o = lax.scan(step, s0, (qc, kc, vc, bc))
    return o.transpose(1, 0, 3, 2, 4).reshape(b, t, nh, dv)


def causal_block_attention(q, k, v, scale):
    b, t, nh, _ = q.shape
    dv = v.shape[-1]
    key_pos = jnp.arange(t)

    def one_block(i):
        q0 = i * Q_BLOCK
        qb = lax.dynamic_slice_in_dim(q, q0, Q_BLOCK, 1)
        qpos = q0 + jnp.arange(Q_BLOCK)
        s = jnp.einsum('bqhd,bkhd->bhqk', qb, k) * scale
        p = masked_softmax(s, key_pos[None, :] <= qpos[:, None])
        return jnp.einsum('bhqk,bkhd->bqhd', p.astype(v.dtype), v)

    out = lax.map(one_block, jnp.arange(t // Q_BLOCK))
    return out.transpose(1, 0, 2, 3, 4).reshape(b, t, nh * dv)


def nsa_attention(nq, nkv, ngate, cmp_pos, cmp_k_w1, cmp_k_w2, cmp_v_w1, cmp_v_w2):
    b, t, _ = nq.shape
    g, hg, dh = NSA_GROUPS, NSA_HEADS // NSA_GROUPS, NSA_HEAD_DIM
    q = nq.reshape(b, t, g, hg, dh)
    kc, vc, ks, vs, kw, vw = [z.reshape(b, t, g, dh) for z in split_cols(nkv, (g * dh,) * 6)]
    gates = jax.nn.sigmoid(ngate.astype(jnp.float32)).reshape(b, t, g, hg, 3)
    slopes = alibi_slopes(NSA_HEADS).reshape(g, hg)
    scale = dh ** -0.5

    n_cmp = (t - NSA_CMP_BLOCK) // NSA_CMP_STRIDE + 1
    win_idx = np.arange(n_cmp)[:, None] * NSA_CMP_STRIDE + np.arange(NSA_CMP_BLOCK)[None, :]

    def compress(z, w1, w2):
        zb = z[:, win_idx] + cmp_pos[None, None, :, None, :]
        zb = zb.transpose(0, 1, 3, 2, 4).reshape(b, n_cmp, g, NSA_CMP_BLOCK * dh)
        return jax.nn.silu(zb @ w1) @ w2

    k_cmp = compress(kc, cmp_k_w1, cmp_k_w2)
    v_cmp = compress(vc, cmp_v_w1, cmp_v_w2)
    cmp_end = np.arange(n_cmp) * NSA_CMP_STRIDE + NSA_CMP_BLOCK - 1

    n_slc = t // NSA_SEL_BLOCK
    n_sel = min(NSA_N_SEL, n_slc)
    c_start = np.arange(n_cmp) * NSA_CMP_STRIDE
    s_start = np.arange(n_slc) * NSA_SEL_BLOCK
    overlap = jnp.asarray((c_start[:, None] < s_start[None, :] + NSA_SEL_BLOCK)
                          & (c_start[:, None] + NSA_CMP_BLOCK > s_start[None, :]), jnp.float32)
    ks_blk = ks.reshape(b, n_slc, NSA_SEL_BLOCK, g, dh).transpose(0, 3, 1, 2, 4)
    vs_blk = vs.reshape(b, n_slc, NSA_SEL_BLOCK, g, dh).transpose(0, 3, 1, 2, 4)
    gather = jax.vmap(jax.vmap(lambda blocks, ix: blocks[ix]))

    pad = ((0, 0), (NSA_WINDOW, 0), (0, 0), (0, 0))
    kw_pad, vw_pad = jnp.pad(kw, pad), jnp.pad(vw, pad)
    blk = jnp.arange(n_slc)
    m_sel = n_sel * NSA_SEL_BLOCK

    def one_block(i):
        q0 = i * Q_BLOCK
        qpos = q0 + jnp.arange(Q_BLOCK)
        qb = lax.dynamic_slice_in_dim(q, q0, Q_BLOCK, 1)
        gb = lax.dynamic_slice_in_dim(gates, q0, Q_BLOCK, 1)
        dist_c = (qpos[:, None] - cmp_end[None, :]).astype(jnp.float32)
        s_c = jnp.einsum('bqghd,bngd->bghqn', qb, k_cmp) * scale - slopes[:, :, None, None] * dist_c
        p_c = masked_softmax(s_c, dist_c >= 0)
        o_c = jnp.einsum('bghqn,bngd->bqghd', p_c, v_cmp)
        imp = jnp.einsum('bghqn,nj->bgqj', p_c, overlap)
        forced = (blk[None, :] == 0) | (blk[None, :] == (qpos // NSA_SEL_BLOCK)[:, None])
        avail = blk[None, :] * NSA_SEL_BLOCK <= qpos[:, None]
        imp = jnp.where(forced, NSA_FORCE, jnp.where(avail, imp, -1.0))
        _, idx = lax.top_k(imp, n_sel)
        k_sel = gather(ks_blk, idx).reshape(b, g, Q_BLOCK, m_sel, dh)
        v_sel = gather(vs_blk, idx).reshape(b, g, Q_BLOCK, m_sel, dh)
        spos = (idx[..., None] * NSA_SEL_BLOCK + jnp.arange(NSA_SEL_BLOCK)).reshape(b, g, 1, Q_BLOCK, m_sel)
        dist_s = (qpos[:, None] - spos).astype(jnp.float32)
        s_s = jnp.einsum('bqghd,bgqmd->bghqm', qb, k_sel) * scale - slopes[None, :, :, None, None] * dist_s
        p_s = masked_softmax(s_s, dist_s >= 0)
        o_s = jnp.einsum('bghqm,bgqmd->bqghd', p_s, v_sel)
        kwb = lax.dynamic_slice_in_dim(kw_pad, q0, NSA_WINDOW + Q_BLOCK, 1)
        vwb = lax.dynamic_slice_in_dim(vw_pad, q0, NSA_WINDOW + Q_BLOCK, 1)
        kpos = q0 - NSA_WINDOW + jnp.arange(NSA_WINDOW + Q_BLOCK)
        dist_w = qpos[:, None] - kpos[None, :]
        mask_w = (dist_w >= 0) & (dist_w < NSA_WINDOW) & (kpos[None, :] >= 0)
        s_w = jnp.einsum('bqghd,bkgd->bghqk', qb, kwb) * scale - slopes[:, :, None, None] * dist_w.astype(jnp.float32)
        p_w = masked_softmax(s_w, mask_w)
        o_w = jnp.einsum('bghqk,bkgd->bqghd', p_w, vwb)
        o = gb[..., 0:1] * o_c + gb[..., 1:2] * o_s + gb[..., 2:3] * o_w
        return o.reshape(b, Q_BLOCK, NSA_HEADS * dh)

    out = lax.map(one_block, jnp.arange(t // Q_BLOCK))
    return out.transpose(1, 0, 2, 3).reshape(b, t, NSA_HEADS * dh)


def mixer_ssd_mla(h, positions, w_in, conv_w, conv_b, dt_bias, a_log, d_skip, ssm_norm_g,
                  q_a_norm_g, w_q_b, kv_a_norm_g, w_kv_b, w_out):
    b, t, _ = h.shape
    z, xbc, dt_raw, q_a, kv_a, k_pe = split_cols(h @ w_in, L0_SIZES)
    xbc = jax.nn.silu(causal_depthwise_conv(xbc, conv_w, conv_b))
    xs, bmat, cmat = split_cols(xbc, (SSD_INNER, SSD_GROUPS * SSD_STATE, SSD_GROUPS * SSD_STATE))
    dt = jax.nn.softplus(dt_raw.astype(jnp.float32) + dt_bias.astype(jnp.float32))
    a = -jnp.exp(a_log.astype(jnp.float32))
    xs = xs.reshape(b, t, SSD_HEADS, SSD_HEAD_DIM)
    y = ssd_chunked(xs, dt, a, bmat.reshape(b, t, SSD_GROUPS, SSD_STATE),
                    cmat.reshape(b, t, SSD_GROUPS, SSD_STATE))
    y = y + d_skip.astype(jnp.float32)[:, None] * xs.astype(jnp.float32)
    gsz = SSD_INNER // SSD_GROUPS
    y = y.reshape(b, t, SSD_GROUPS, gsz) * jax.nn.silu(z.astype(jnp.float32).reshape(b, t, SSD_GROUPS, gsz))
    y_ssd = rmsnorm(y, ssm_norm_g.reshape(SSD_GROUPS, gsz)).reshape(b, t, SSD_INNER).astype(h.dtype)
    cos, sin = rope_cos_sin(positions, MLA_ROPE)
    q = (rmsnorm(q_a, q_a_norm_g) @ w_q_b).reshape(b, t, MLA_HEADS, MLA_NOPE + MLA_ROPE)
    q = jnp.concatenate([q[..., :MLA_NOPE],
                         apply_rope(q[..., MLA_NOPE:], cos[:, :, None], sin[:, :, None])], axis=-1)
    kv = (rmsnorm(kv_a, kv_a_norm_g) @ w_kv_b).reshape(b, t, MLA_HEADS, MLA_NOPE + MLA_V)
    k_rot = apply_rope(k_pe, cos, sin)
    k = jnp.concatenate([kv[..., :MLA_NOPE],
                         jnp.broadcast_to(k_rot[:, :, None], (b, t, MLA_HEADS, MLA_ROPE))], axis=-1)
    o_mla = causal_block_attention(q, k, kv[..., MLA_NOPE:], (MLA_NOPE + MLA_ROPE) ** -0.5).astype(h.dtype)
    return jnp.concatenate([y_ssd, o_mla], axis=-1) @ w_out


def mixer_gla_nsa(h, w_in, w_gk2, b_gk, gla_norm_g, cmp_pos, cmp_k_w1, cmp_k_w2, cmp_v_w1, cmp_v_w2, w_out):
    b, t, _ = h.shape
    gq, gk, gv, glr, gg, nq, nkv, ngate = split_cols(h @ w_in, L1_SIZES)
    log_a = jax.nn.log_sigmoid((glr @ w_gk2 + b_gk).astype(jnp.float32)) / GLA_GATE_NORM
    o = gla_chunked(gq.reshape(b, t, GLA_HEADS, GLA_DK), gk.reshape(b, t, GLA_HEADS, GLA_DK),
                    gv.reshape(b, t, GLA_HEADS, GLA_DV), log_a.reshape(b, t, GLA_HEADS, GLA_DK))
    o = rmsnorm(o, gla_norm_g) * jax.nn.silu(gg.astype(jnp.float32).reshape(b, t, GLA_HEADS, GLA_DV))
    o_gla = o.reshape(b, t, GLA_HEADS * GLA_DV).astype(h.dtype)
    o_nsa = nsa_attention(nq, nkv, ngate, cmp_pos, cmp_k_w1, cmp_k_w2, cmp_v_w1, cmp_v_w2).astype(h.dtype)
    return jnp.concatenate([o_gla, o_nsa], axis=-1) @ w_out


def swiglu(h, w_gate, w_up, w_down):
    return (jax.nn.silu(h @ w_gate) * (h @ w_up)) @ w_down


def adaln_modulation(c, w, b):
    mod = jax.nn.silu(c) @ w + b
    return [m[:, None, :] for m in jnp.split(mod, N_MOD, axis=-1)]


def setup_inputs(seed: int = 0) -> dict:
    key = jax.random.key(seed)
    keys = jax.random.split(key, 64)
    counter = [0]
    f32 = jnp.float32
    D = D_MODEL

    def nk():
        counter[0] += 1
        return keys[counter[0] - 1]

    def nrm(shape, fan_in):
        return jax.random.normal(nk(), shape, f32) * (fan_in ** -0.5)

    def gain(n):
        return 1.0 + 0.05 * jax.random.normal(nk(), (n,), f32)

    def small(shape):
        return 0.01 * jax.random.normal(nk(), shape, f32)

    x = jax.random.normal(nk(), (BATCH, SEQ, D), f32)
    c = jax.random.normal(nk(), (BATCH, D), f32)
    offset = jax.random.randint(nk(), (BATCH, 1), 0, 1024, jnp.int32)
    positions = (offset + jnp.arange(SEQ, dtype=jnp.int32)[None, :]).astype(jnp.int32)
    dt0 = jnp.exp(jax.random.uniform(nk(), (SSD_HEADS,), f32) * (math.log(0.1) - math.log(0.001)) + math.log(0.001))
    dt_bias = dt0 + jnp.log(-jnp.expm1(-dt0))
    a_log = jnp.log(jax.random.uniform(nk(), (SSD_HEADS,), f32, minval=1.0, maxval=16.0))
    return {
        'x': x, 'c': c, 'positions': positions,
        'l0_ada_w': nrm((D, N_MOD * D), D), 'l0_ada_b': small((N_MOD * D,)),
        'l0_mix_pre_g': gain(D), 'l0_mix_post_g': gain(D),
        'l0_w_in': nrm((D, L0_IN), D),
        'l0_conv_w': nrm((SSD_CONV, SSD_CONV_CH), SSD_CONV), 'l0_conv_b': small((SSD_CONV_CH,)),
        'l0_dt_bias': dt_bias, 'l0_a_log': a_log,
        'l0_d_skip': 1.0 + 0.1 * jax.random.normal(nk(), (SSD_HEADS,), f32),
        'l0_ssm_norm_g': gain(SSD_INNER),
        'l0_q_a_norm_g': gain(MLA_Q_RANK),
        'l0_w_q_b': nrm((MLA_Q_RANK, MLA_HEADS * (MLA_NOPE + MLA_ROPE)), MLA_Q_RANK),
        'l0_kv_a_norm_g': gain(MLA_KV_RANK),
        'l0_w_kv_b': nrm((MLA_KV_RANK, MLA_HEADS * (MLA_NOPE + MLA_V)), MLA_KV_RANK),
        'l0_w_out': nrm((L0_CAT, D), L0_CAT),
        'l0_ffn_pre_g': gain(D), 'l0_ffn_post_g': gain(D),
        'l0_w_gate': nrm((D, FFN_HIDDEN), D), 'l0_w_up': nrm((D, FFN_HIDDEN), D),
        'l0_w_down': nrm((FFN_HIDDEN, D), FFN_HIDDEN),
        'l1_ada_w': nrm((D, N_MOD * D), D), 'l1_ada_b': small((N_MOD * D,)),
        'l1_mix_pre_g': gain(D), 'l1_mix_post_g': gain(D),
        'l1_w_in': nrm((D, L1_IN), D),
        'l1_w_gk2': nrm((GLA_GATE_RANK, GLA_HEADS * GLA_DK), GLA_GATE_RANK),
        'l1_b_gk': small((GLA_HEADS * GLA_DK,)),
        'l1_gla_norm_g': gain(GLA_DV),
        'l1_cmp_pos': 0.1 * jax.random.normal(nk(), (NSA_CMP_BLOCK, NSA_HEAD_DIM), f32),
        'l1_cmp_k_w1': nrm((NSA_CMP_BLOCK * NSA_HEAD_DIM, NSA_CMP_HIDDEN), NSA_CMP_BLOCK * NSA_HEAD_DIM),
        'l1_cmp_k_w2': nrm((NSA_CMP_HIDDEN, NSA_HEAD_DIM), NSA_CMP_HIDDEN),
        'l1_cmp_v_w1': nrm((NSA_CMP_BLOCK * NSA_HEAD_DIM, NSA_CMP_HIDDEN), NSA_CMP_BLOCK * NSA_HEAD_DIM),
        'l1_cmp_v_w2': nrm((NSA_CMP_HIDDEN, NSA_HEAD_DIM), NSA_CMP_HIDDEN),
        'l1_w_out': nrm((L1_CAT, D), L1_CAT),
        'l1_ffn_pre_g': gain(D), 'l1_ffn_post_g': gain(D),
        'l1_w_gate': nrm((D, FFN_HIDDEN), D), 'l1_w_up': nrm((D, FFN_HIDDEN), D),
        'l1_w_down': nrm((FFN_HIDDEN, D), FFN_HIDDEN),
    }


def reference(x, c, positions,
              l0_ada_w, l0_ada_b, l0_mix_pre_g, l0_mix_post_g, l0_w_in, l0_conv_w, l0_conv_b,
              l0_dt_bias, l0_a_log, l0_d_skip, l0_ssm_norm_g, l0_q_a_norm_g, l0_w_q_b,
              l0_kv_a_norm_g, l0_w_kv_b, l0_w_out, l0_ffn_pre_g, l0_ffn_post_g,
              l0_w_gate, l0_w_up, l0_w_down,
              l1_ada_w, l1_ada_b, l1_mix_pre_g, l1_mix_post_g, l1_w_in, l1_w_gk2, l1_b_gk,
              l1_gla_norm_g, l1_cmp_pos, l1_cmp_k_w1, l1_cmp_k_w2, l1_cmp_v_w1, l1_cmp_v_w2,
              l1_w_out, l1_ffn_pre_g, l1_ffn_post_g, l1_w_gate, l1_w_up, l1_w_down):
    ada = ((l0_ada_w, l0_ada_b), (l1_ada_w, l1_ada_b))
    norms = ((l0_mix_pre_g, l0_mix_post_g, l0_ffn_pre_g, l0_ffn_post_g),
             (l1_mix_pre_g, l1_mix_post_g, l1_ffn_pre_g, l1_ffn_post_g))
    ffns = ((l0_w_gate, l0_w_up, l0_w_down), (l1_w_gate, l1_w_up, l1_w_down))
    mixers = (
        lambda h: mixer_ssd_mla(h, positions, l0_w_in, l0_conv_w, l0_conv_b, l0_dt_bias, l0_a_log,
                                l0_d_skip, l0_ssm_norm_g, l0_q_a_norm_g, l0_w_q_b, l0_kv_a_norm_g,
                                l0_w_kv_b, l0_w_out),
        lambda h: mixer_gla_nsa(h, l1_w_in, l1_w_gk2, l1_b_gk, l1_gla_norm_g, l1_cmp_pos,
                                l1_cmp_k_w1, l1_cmp_k_w2, l1_cmp_v_w1, l1_cmp_v_w2, l1_w_out),
    )
    for layer in range(DEPTH):
        shift_m, scale_m, gate_m, shift_f, scale_f, gate_f = adaln_modulation(c, *ada[layer])
        pre_m, post_m, pre_f, post_f = norms[layer]
        h = rmsnorm(x, pre_m) * (1.0 + scale_m) + shift_m
        x = x + gate_m * rmsnorm(mixers[layer](h), post_m)
        h = rmsnorm(x, pre_f) * (1.0 + scale_f) + shift_f
        x = x + gate_f * rmsnorm(swiglu(h, *ffns[layer]), post_f)
    return x
```

```python
import functools

import numpy as np
import jax
import jax.numpy as jnp
from jax import lax
from jax.experimental import pallas as pl
from jax.experimental.pallas import tpu as pltpu

F32, BF16 = jnp.float32, jnp.bfloat16
HI = lax.Precision.HIGHEST
LANES = 128
VMEM_LIMIT = 48 * 1024 * 1024

NORM_EPS = 1e-6
NEG_INF = -1e30
N_MOD = 6

SSD_HEADS, SSD_HEAD_DIM, SSD_STATE, SSD_GROUPS, SSD_CONV = 16, 64, 128, 2, 4
SSD_INNER = SSD_HEADS * SSD_HEAD_DIM
SSD_CONV_CH = SSD_INNER + 2 * SSD_GROUPS * SSD_STATE
SSD_CHUNK = 128

MLA_HEADS, MLA_Q_RANK, MLA_KV_RANK, MLA_NOPE, MLA_ROPE, MLA_V = 8, 384, 256, 64, 32, 64
ROPE_THETA = 10000.0

GLA_HEADS, GLA_DK, GLA_DV, GLA_GATE_RANK, GLA_GATE_NORM = 4, 128, 256, 16, 16.0
GLA_CHUNK = 128
GLA_SUB = 16

NSA_HEADS, NSA_GROUPS, NSA_HEAD_DIM = 8, 2, 64
NSA_HG = NSA_HEADS // NSA_GROUPS
NSA_CMP_BLOCK, NSA_CMP_STRIDE, NSA_CMP_HIDDEN = 32, 16, 256
NSA_SEL_BLOCK, NSA_N_SEL, NSA_WINDOW, NSA_FORCE = 64, 16, 512, 1e4
_SEL_SHIFT = NSA_SEL_BLOCK.bit_length() - 1
assert 1 << _SEL_SHIFT == NSA_SEL_BLOCK


def _cparams(sem):
    return pltpu.CompilerParams(dimension_semantics=sem, vmem_limit_bytes=VMEM_LIMIT)


def _dot(a, b, prec=None):
    return jnp.dot(a, b, preferred_element_type=F32, precision=prec)


def _dot_nt(a, b, prec=None):
    return lax.dot_general(a, b, (((1,), (1,)), ((), ())), preferred_element_type=F32, precision=prec)


def _silu(x):
    return x * (1.0 / (1.0 + jnp.exp(-x)))


def _softplus(x):
    return jnp.maximum(x, 0.0) + jnp.log1p(jnp.exp(-jnp.abs(x)))


def _rms(x, g):
    return x * lax.rsqrt(jnp.mean(x * x, axis=-1, keepdims=True) + NORM_EPS) * g


def _tril(n):
    return lax.broadcasted_iota(jnp.int32, (n, n), 0) >= lax.broadcasted_iota(jnp.int32, (n, n), 1)


def _alibi_table(n, groups):
    s = 2.0 ** (-8.0 * np.arange(1, n + 1) / n)
    return [[float(v) for v in row] for row in s.reshape(groups, n // groups)]


def _by_group(g, table, h):
    val = table[0][h]
    for gi in range(1, len(table)):
        val = jnp.where(g == gi, table[gi][h], val)
    return val


def _ada_kernel(c_ref, w_ref, b_ref, o_ref):
    o_ref[...] = _dot(_silu(c_ref[...]), w_ref[...], HI) + b_ref[...]


def _ada(c, w, b):
    bsz, d = c.shape
    n = w.shape[1]
    rows = 8
    cp = jnp.zeros((rows, d), F32).at[:bsz].set(c)
    tn = 1024
    out = pl.pallas_call(
        _ada_kernel,
        grid=(n // tn,),
        in_specs=[pl.BlockSpec((rows, d), lambda j: (0, 0)),
                  pl.BlockSpec((d, tn), lambda j: (0, j)),
                  pl.BlockSpec((1, tn), lambda j: (0, j))],
        out_specs=pl.BlockSpec((rows, tn), lambda j: (0, j)),
        out_shape=jax.ShapeDtypeStruct((rows, n), F32),
        compiler_params=_cparams(("arbitrary",)),
        name="ada",
    )(cp, w, b.reshape(1, n))
    return [m.reshape(bsz, 1, d) for m in jnp.split(out[:bsz], N_MOD, axis=-1)]


def _proj_in_kernel(x_ref, g_ref, sc_ref, sh_ref, w_ref, *o_refs, segs):
    h = (_rms(x_ref[...], g_ref[...]) * (1.0 + sc_ref[0]) + sh_ref[0]).astype(BF16)
    for (a, b), o_ref in zip(segs, o_refs):
        o_ref[...] = _dot(h, w_ref[:, a:b]).astype(o_ref.dtype)


def _proj_in(x2, g, scale, shift, w, segs, seq, tm=256):
    m, d = x2.shape
    n = w.shape[1]
    tpb = seq // tm
    return pl.pallas_call(
        functools.partial(_proj_in_kernel, segs=segs),
        grid=(m // tm,),
        in_specs=[pl.BlockSpec((tm, d), lambda i: (i, 0)),
                  pl.BlockSpec((1, d), lambda i: (0, 0)),
                  pl.BlockSpec((1, 1, d), lambda i: (i // tpb, 0, 0)),
                  pl.BlockSpec((1, 1, d), lambda i: (i // tpb, 0, 0)),
                  pl.BlockSpec((d, n), lambda i: (0, 0))],
        out_specs=[pl.BlockSpec((tm, b - a), lambda i: (i, 0)) for a, b in segs],
        out_shape=[jax.ShapeDtypeStruct((m, b - a), F32) for a, b in segs],
        compiler_params=_cparams(("arbitrary",)),
        name="proj_in",
    )(x2, g.reshape(1, d), scale, shift, w)


def _out_res_kernel(x_ref, a_ref, b_ref, wa_ref, wb_ref, g_ref, gate_ref, o_ref):
    y = _dot(a_ref[...], wa_ref[...]) + _dot(b_ref[...], wb_ref[...])
    o_ref[...] = x_ref[...] + gate_ref[0] * _rms(y, g_ref[...])


def _out_res(x2, a, b, wa, wb, g, gate, seq, tm=512):
    m, d = x2.shape
    ka, kb = a.shape[1], b.shape[1]
    tpb = seq // tm
    return pl.pallas_call(
        _out_res_kernel,
        grid=(m // tm,),
        in_specs=[pl.BlockSpec((tm, d), lambda i: (i, 0)),
                  pl.BlockSpec((tm, ka), lambda i: (i, 0)),
                  pl.BlockSpec((tm, kb), lambda i: (i, 0)),
                  pl.BlockSpec((ka, d), lambda i: (0, 0)),
                  pl.BlockSpec((kb, d), lambda i: (0, 0)),
                  pl.BlockSpec((1, d), lambda i: (0, 0)),
                  pl.BlockSpec((1, 1, d), lambda i: (i // tpb, 0, 0))],
        out_specs=pl.BlockSpec((tm, d), lambda i: (i, 0)),
        out_shape=jax.ShapeDtypeStruct((m, d), F32),
        compiler_params=_cparams(("arbitrary",)),
        name="out_res",
    )(x2, a, b, wa, wb, g.reshape(1, d), gate)


def _ffn_kernel(x_ref, gpre_ref, sc_ref, sh_ref, wg_ref, wu_ref, wd_ref, gpost_ref, gate_ref, o_ref,
                h_scr, acc_scr):
    j = pl.program_id(1)

    @pl.when(j == 0)
    def _():
        h_scr[...] = (_rms(x_ref[...], gpre_ref[...]) * (1.0 + sc_ref[0]) + sh_ref[0]).astype(BF16)
        acc_scr[...] = jnp.zeros_like(acc_scr)

    h = h_scr[...]
    act = (_silu(_dot(h, wg_ref[...])) * _dot(h, wu_ref[...])).astype(BF16)
    acc_scr[...] += _dot(act, wd_ref[...])

    @pl.when(j == pl.num_programs(1) - 1)
    def _():
        o_ref[...] = x_ref[...] + gate_ref[0] * _rms(acc_scr[...], gpost_ref[...])


def _ffn(x2, gpre, scale, shift, wg, wu, wd, gpost, gate, seq, tm=512, th=1408):
    m, d = x2.shape
    hid = wg.shape[1]
    tpb = seq // tm
    return pl.pallas_call(
        _ffn_kernel,
        grid=(m // tm, hid // th),
        in_specs=[pl.BlockSpec((tm, d), lambda i, j: (i, 0)),
                  pl.BlockSpec((1, d), lambda i, j: (0, 0)),
                  pl.BlockSpec((1, 1, d), lambda i, j: (i // tpb, 0, 0)),
                  pl.BlockSpec((1, 1, d), lambda i, j: (i // tpb, 0, 0)),
                  pl.BlockSpec((d, th), lambda i, j: (0, j)),
                  pl.BlockSpec((d, th), lambda i, j: (0, j)),
                  pl.BlockSpec((th, d), lambda i, j: (j, 0)),
                  pl.BlockSpec((1, d), lambda i, j: (0, 0)),
                  pl.BlockSpec((1, 1, d), lambda i, j: (i // tpb, 0, 0))],
        out_specs=pl.BlockSpec((tm, d), lambda i, j: (i, 0)),
        out_shape=jax.ShapeDtypeStruct((m, d), F32),
        scratch_shapes=[pltpu.VMEM((tm, d), BF16), pltpu.VMEM((tm, d), F32)],
        compiler_params=_cparams(("arbitrary", "arbitrary")),
        name="ffn",
    )(x2, gpre.reshape(1, d), scale, shift, wg, wu, wd, gpost.reshape(1, d), gate)


def _ssd_kernel(z_ref, xbc_ref, misc_ref, cw_ref, cb_ref, dtb_ref, alog_ref, dsk_ref, ng_ref, e_ref, o_ref,
                ext_scr, st_scr):
    L = SSD_CHUNK
    gsz = SSD_INNER // SSD_GROUPS
    hpg = SSD_HEADS // SSD_GROUPS
    pad = 8

    @pl.when(pl.program_id(1) == 0)
    def _():
        ext_scr[0:pad, :] = jnp.zeros((pad, SSD_CONV_CH), F32)
        st_scr[...] = jnp.zeros_like(st_scr)

    xt = xbc_ref[...]
    ext_scr[pad:pad + L, :] = xt
    acc = cb_ref[...] + cw_ref[0:1, :] * ext_scr[pad - 3:pad - 3 + L, :]
    for k in range(1, SSD_CONV):
        acc = acc + cw_ref[k:k + 1, :] * ext_scr[pad - 3 + k:pad - 3 + k + L, :]
    ext_scr[0:pad, :] = xt[L - pad:L, :]
    xbc = _silu(acc)
    xs = xbc[:, :SSD_INNER]

    e = e_ref[...]
    dt = _softplus(misc_ref[...] + dtb_ref[...])
    adt = dt * (-jnp.exp(alog_ref[...]))
    a_cs = _dot(jnp.where(_tril(L), 1.0, 0.0), adt, HI)
    a_cs_t = a_cs.T
    ea = jnp.exp(a_cs)
    ea_e = _dot(ea, e, HI)
    dec_e = _dot(jnp.exp(a_cs[L - 1:L, :] - a_cs), e, HI)
    xd = xs * _dot(dt, e, HI)
    tril = _tril(L)

    ys = []
    for g in range(SSD_GROUPS):
        bg = xbc[:, SSD_INNER + g * SSD_STATE:SSD_INNER + (g + 1) * SSD_STATE]
        cg = xbc[:, SSD_INNER + (SSD_GROUPS + g) * SSD_STATE:SSD_INNER + (SSD_GROUPS + g + 1) * SSD_STATE]
        bg16, cg16 = bg.astype(BF16), cg.astype(BF16)
        gmat = _dot_nt(cg16, bg16)
        cols = slice(g * gsz, (g + 1) * gsz)
        xdg = xd[:, cols]
        st = st_scr[g]
        y_off = _dot(cg16, st.astype(BF16)) * ea_e[:, cols]
        st_scr[g] = st * ea_e[L - 1:L, cols] + _dot(bg.T.astype(BF16), (xdg * dec_e[:, cols]).astype(BF16))
        yd = []
        for h in range(hpg):
            hh = g * hpg + h
            seg = a_cs[:, hh:hh + 1] - a_cs_t[hh:hh + 1, :]
            lmat = jnp.where(tril, jnp.exp(seg), 0.0)
            yd.append(_dot((gmat * lmat).astype(BF16),
                           xdg[:, h * SSD_HEAD_DIM:(h + 1) * SSD_HEAD_DIM].astype(BF16)))
        y = jnp.concatenate(yd, axis=-1) + y_off + dsk_ref[:, cols] * xs[:, cols]
        y = y * _silu(z_ref[:, cols])
        ys.append(_rms(y, ng_ref[:, cols]))
    o_ref[...] = jnp.concatenate(ys, axis=-1).astype(o_ref.dtype)


def _ssd(z, xbc, misc, conv_w, conv_b, dt_bias, a_log, d_skip, norm_g, bsz, seq):
    L = SSD_CHUNK
    nc = seq // L
    m = bsz * seq
    e = np.zeros((LANES, SSD_INNER), np.float32)
    for h in range(SSD_HEADS):
        e[h, h * SSD_HEAD_DIM:(h + 1) * SSD_HEAD_DIM] = 1.0
    pad128 = lambda v: jnp.zeros((1, LANES), F32).at[0, :v.shape[0]].set(v)
    row = lambda i, c: (i * nc + c, 0)
    const = lambda i, c: (0, 0)
    return pl.pallas_call(
        _ssd_kernel,
        grid=(bsz, nc),
        in_specs=[pl.BlockSpec((L, SSD_INNER), row),
                  pl.BlockSpec((L, SSD_CONV_CH), row),
                  pl.BlockSpec((L, LANES), row),
                  pl.BlockSpec((SSD_CONV, SSD_CONV_CH), const),
                  pl.BlockSpec((1, SSD_CONV_CH), const),
                  pl.BlockSpec((1, LANES), const),
                  pl.BlockSpec((1, LANES), const),
                  pl.BlockSpec((1, SSD_INNER), const),
                  pl.BlockSpec((1, SSD_INNER), const),
                  pl.BlockSpec((LANES, SSD_INNER), const)],
        out_specs=pl.BlockSpec((L, SSD_INNER), row),
        out_shape=jax.ShapeDtypeStruct((m, SSD_INNER), BF16),
        scratch_shapes=[pltpu.VMEM((L + 8, SSD_CONV_CH), F32),
                        pltpu.VMEM((SSD_GROUPS, SSD_STATE, SSD_INNER // SSD_GROUPS), F32)],
        compiler_params=_cparams(("arbitrary", "arbitrary")),
        name="ssd",
    )(z, xbc, misc, conv_w, conv_b.reshape(1, -1), pad128(dt_bias), pad128(a_log),
      jnp.repeat(d_skip, SSD_HEAD_DIM).reshape(1, -1), norm_g.reshape(1, -1), jnp.asarray(e))


MLA_D = LANES
_R1 = MLA_NOPE
_R2 = MLA_NOPE + MLA_ROPE // 2


def _mla_prep_kernel(qa_ref, kva_ref, misc_ref, pos_ref, gq_ref, gkv_ref, wq_ref, wqs_ref, wk_ref, wv_ref,
                     pk_ref, pks_ref, invf_ref, sgn_ref, q_ref, k_ref, v_ref):
    ang = pos_ref[...].astype(F32) * invf_ref[...]
    cos = jnp.cos(ang)
    sin = jnp.sin(ang) * sgn_ref[...]
    nq = _rms(qa_ref[...], gq_ref[...]).astype(BF16)
    nkv = _rms(kva_ref[...], gkv_ref[...]).astype(BF16)
    misc = misc_ref[...]
    k_rot = _dot(misc, pk_ref[...], HI) * cos + _dot(misc, pks_ref[...], HI) * sin
    for h in range(MLA_HEADS):
        q_ref[0, h] = (_dot(nq, wq_ref[h]) * cos + _dot(nq, wqs_ref[h]) * sin).astype(q_ref.dtype)
        k_ref[0, h] = (_dot(nkv, wk_ref[h]) + k_rot).astype(k_ref.dtype)
        v_ref[0, h] = _dot(nkv, wv_ref[h]).astype(v_ref.dtype)


def _mla_prep(q_a, kv_a, misc, positions, gq, gkv, w_q_b, w_kv_b, bsz, seq, tm=512):
    m = bsz * seq
    half = MLA_ROPE // 2
    dq = MLA_NOPE + MLA_ROPE
    wq3 = w_q_b.reshape(MLA_Q_RANK, MLA_HEADS, dq).transpose(1, 0, 2)
    zq = jnp.zeros((MLA_HEADS, MLA_Q_RANK, MLA_D - dq), F32)
    wq = jnp.concatenate([wq3, zq], axis=-1).astype(BF16)
    wqs = jnp.concatenate([jnp.zeros((MLA_HEADS, MLA_Q_RANK, MLA_NOPE), F32), wq3[..., _R2:dq], wq3[..., _R1:_R2], zq],
                          axis=-1).astype(BF16)
    wkv3 = w_kv_b.reshape(MLA_KV_RANK, MLA_HEADS, MLA_NOPE + MLA_V).transpose(1, 0, 2)
    wk = jnp.concatenate([wkv3[..., :MLA_NOPE], jnp.zeros((MLA_HEADS, MLA_KV_RANK, MLA_D - MLA_NOPE), F32)],
                         axis=-1).astype(BF16)
    wv = wkv3[..., MLA_NOPE:].astype(BF16)
    pk = np.zeros((LANES, MLA_D), np.float32)
    pks = np.zeros((LANES, MLA_D), np.float32)
    invf = np.zeros((1, MLA_D), np.float32)
    sgn = np.zeros((1, MLA_D), np.float32)
    inv = (ROPE_THETA ** (-np.arange(0, MLA_ROPE, 2) / MLA_ROPE)).astype(np.float32)
    for i in range(half):
        pk[SSD_HEADS + i, _R1 + i] = 1.0
        pk[SSD_HEADS + half + i, _R2 + i] = 1.0
        pks[SSD_HEADS + half + i, _R1 + i] = 1.0
        pks[SSD_HEADS + i, _R2 + i] = 1.0
        invf[0, _R1 + i] = invf[0, _R2 + i] = inv[i]
        sgn[0, _R1 + i], sgn[0, _R2 + i] = -1.0, 1.0
    tpb = seq // tm
    c2 = lambda i: (0, 0)
    c3 = lambda i: (0, 0, 0)
    hm = lambda i: (i // tpb, 0, i % tpb, 0)
    return pl.pallas_call(
        _mla_prep_kernel,
        grid=(m // tm,),
        in_specs=[pl.BlockSpec((tm, MLA_Q_RANK), lambda i: (i, 0)),
                  pl.BlockSpec((tm, MLA_KV_RANK), lambda i: (i, 0)),
                  pl.BlockSpec((tm, LANES), lambda i: (i, 0)),
                  pl.BlockSpec((tm, 1), lambda i: (i, 0)),
                  pl.BlockSpec((1, MLA_Q_RANK), c2),
                  pl.BlockSpec((1, MLA_KV_RANK), c2),
                  pl.BlockSpec((MLA_HEADS, MLA_Q_RANK, MLA_D), c3),
                  pl.BlockSpec((MLA_HEADS, MLA_Q_RANK, MLA_D), c3),
                  pl.BlockSpec((MLA_HEADS, MLA_KV_RANK, MLA_D), c3),
                  pl.BlockSpec((MLA_HEADS, MLA_KV_RANK, MLA_V), c3),
                  pl.BlockSpec((LANES, MLA_D), c2),
                  pl.BlockSpec((LANES, MLA_D), c2),
                  pl.BlockSpec((1, MLA_D), c2),
                  pl.BlockSpec((1, MLA_D), c2)],
        out_specs=[pl.BlockSpec((1, MLA_HEADS, tm, MLA_D), hm),
                   pl.BlockSpec((1, MLA_HEADS, tm, MLA_D), hm),
                   pl.BlockSpec((1, MLA_HEADS, tm, MLA_V), hm)],
        out_shape=[jax.ShapeDtypeStruct((bsz, MLA_HEADS, seq, MLA_D), BF16),
                   jax.ShapeDtypeStruct((bsz, MLA_HEADS, seq, MLA_D), BF16),
                   jax.ShapeDtypeStruct((bsz, MLA_HEADS, seq, MLA_V), BF16)],
        compiler_params=_cparams(("arbitrary",)),
        name="mla_prep",
    )(q_a, kv_a, misc, positions.reshape(m, 1), gq.reshape(1, -1), gkv.reshape(1, -1), wq, wqs, wk, wv,
      jnp.asarray(pk), jnp.asarray(pks), jnp.asarray(invf), jnp.asarray(sgn))


def _flash_kernel(*refs, hg, tq, tk, dv, scale, slopes, window, has_sel):
    if has_sel:
        q_ref, k_ref, v_ref, sel_ref, o_ref, m_scr, l_scr, acc_scr = refs
    else:
        q_ref, k_ref, v_ref, o_ref, m_scr, l_scr, acc_scr = refs
    g = pl.program_id(1)
    qi = pl.program_id(2)
    ki = pl.program_id(3)
    nk = pl.num_programs(3)

    @pl.when(ki == 0)
    def _():
        m_scr[...] = jnp.full_like(m_scr, NEG_INF)
        l_scr[...] = jnp.zeros_like(l_scr)
        acc_scr[...] = jnp.zeros_like(acc_scr)

    if window is None:
        kb = ki
        run = ki * tk <= qi * tq + (tq - 1)
    else:
        kb = qi - (nk - 1) + ki
        run = kb >= 0

    @pl.when(run)
    def _():
        k = k_ref[0, 0]
        v = v_ref[0, 0]
        dist = (qi * tq + lax.broadcasted_iota(jnp.int32, (tq, tk), 0)) - \
               (kb * tk + lax.broadcasted_iota(jnp.int32, (tq, tk), 1))
        mask = dist >= 0
        if window is not None:
            mask = mask & (dist < window)
        if has_sel:
            blk = lax.broadcasted_iota(jnp.int32, (LANES, tk), 0)
            key = kb * tk + lax.broadcasted_iota(jnp.int32, (LANES, tk), 1)
            expand = jnp.where(blk * NSA_SEL_BLOCK <= key, jnp.where(key < (blk + 1) * NSA_SEL_BLOCK, 1.0, 0.0), 0.0)
            mask = mask & (_dot(sel_ref[0, 0], expand.astype(BF16)) > 0.5)
        distf = dist.astype(F32)
        for h in range(hg):
            s = _dot_nt(q_ref[0, 0, h].astype(BF16), k) * scale
            if slopes is not None:
                s = s - _by_group(g, slopes, h) * distf
            s = jnp.where(mask, s, NEG_INF)
            m_prev = m_scr[h]
            m_next = jnp.maximum(m_prev, jnp.max(s, axis=1, keepdims=True))
            p = jnp.where(mask, jnp.exp(s - m_next[:, :1]), 0.0)
            alpha = jnp.exp(m_prev - m_next)
            l_scr[h] = alpha * l_scr[h] + jnp.sum(p, axis=1, keepdims=True)
            m_scr[h] = m_next
            acc_scr[h] = acc_scr[h] * alpha[:, :dv] + _dot(p.astype(BF16), v)

    @pl.when(ki == nk - 1)
    def _():
        for h in range(hg):
            o_ref[0, 0, h] = (acc_scr[h] / jnp.maximum(l_scr[h][:, :dv], 1e-30)).astype(o_ref.dtype)


def _flash(q, k, v, sel, *, tq, tk, scale, slopes=None, window=None, out_dtype=F32):
    bsz, ng, hg, seq, d = q.shape
    dv = v.shape[-1]
    if window is None:
        nk = seq // tk
        kv_idx = lambda b, g, qi, ki: (b, g, jnp.minimum(ki, (qi * tq + tq - 1) // tk), 0)
    else:
        assert tq == tk and window % tk == 0
        nk = window // tk + 1
        kv_idx = lambda b, g, qi, ki: (b, g, jnp.maximum(qi - (nk - 1) + ki, 0), 0)
    in_specs = [pl.BlockSpec((1, 1, hg, tq, d), lambda b, g, qi, ki: (b, g, 0, qi, 0)),
                pl.BlockSpec((1, 1, tk, d), kv_idx),
                pl.BlockSpec((1, 1, tk, dv), kv_idx)]
    args = [q, k, v]
    if sel is not None:
        in_specs.append(pl.BlockSpec((1, 1, tq, LANES), lambda b, g, qi, ki: (b, g, qi, 0)))
        args.append(sel)
    return pl.pallas_call(
        functools.partial(_flash_kernel, hg=hg, tq=tq, tk=tk, dv=dv, scale=scale, slopes=slopes, window=window,
                          has_sel=sel is not None),
        grid=(bsz, ng, seq // tq, nk),
        in_specs=in_specs,
        out_specs=pl.BlockSpec((1, 1, hg, tq, dv), lambda b, g, qi, ki: (b, g, 0, qi, 0)),
        out_shape=jax.ShapeDtypeStruct((bsz, ng, hg, seq, dv), out_dtype),
        scratch_shapes=[pltpu.VMEM((hg, tq, LANES), F32), pltpu.VMEM((hg, tq, LANES), F32),
                        pltpu.VMEM((hg, tq, dv), F32)],
        compiler_params=_cparams(("arbitrary", "arbitrary", "arbitrary", "arbitrary")),
        name="flash",
    )(*args)


def _gla_kernel(q_ref, k_ref, v_ref, gg_ref, misc_ref, w2_ref, bgk_ref, ng_ref, o_ref, st_scr):
    L = GLA_CHUNK

    @pl.when(pl.program_id(2) == 0)
    def _():
        st_scr[...] = jnp.zeros_like(st_scr)

    zg = _dot(misc_ref[...], w2_ref[...], HI) + bgk_ref[...]
    log_a = -_softplus(-zg) * (1.0 / GLA_GATE_NORM)
    tril = _tril(L)
    bc = _dot(jnp.where(tril, 1.0, 0.0), log_a, HI)
    q = q_ref[...] * (GLA_DK ** -0.5)
    k = k_ref[...]
    v16 = v_ref[...].astype(BF16)
    st = st_scr[...]
    o = _dot_nt((q * jnp.exp(bc)).astype(BF16), st.astype(BF16))
    row = lax.broadcasted_iota(jnp.int32, (L, GLA_DK), 0)
    att = []
    for i in range(L // GLA_SUB):
        r0 = i * GLA_SUB
        ref = bc[r0:r0 + 1, :]
        qi = q[r0:r0 + GLA_SUB] * jnp.exp(bc[r0:r0 + GLA_SUB] - ref)
        ki = k * jnp.exp(jnp.where(row < r0 + GLA_SUB, ref - bc, 0.0))
        att.append(_dot_nt(qi.astype(BF16), ki.astype(BF16)))
    att = jnp.where(tril, jnp.concatenate(att, axis=0), 0.0)
    o = o + _dot(att.astype(BF16), v16)
    b_last = bc[L - 1:L, :]
    kd = (k * jnp.exp(b_last - bc)).astype(BF16)
    st_scr[...] = st * jnp.exp(b_last) + _dot(v_ref[...].T.astype(BF16), kd)
    o_ref[...] = (_rms(o, ng_ref[...]) * _silu(gg_ref[...])).astype(o_ref.dtype)


def _gla(gq, gk, gv, gg, misc, w_gk2, b_gk, norm_g, bsz, seq):
    L = GLA_CHUNK
    nc = seq // L
    m = bsz * seq
    w2 = jnp.zeros((LANES, GLA_HEADS * GLA_DK), F32).at[:GLA_GATE_RANK].set(w_gk2)
    rowh = lambda b, h, c: (b * nc + c, h)
    return pl.pallas_call(
        _gla_kernel,
        grid=(bsz, GLA_HEADS, nc),
        in_specs=[pl.BlockSpec((L, GLA_DK), rowh),
                  pl.BlockSpec((L, GLA_DK), rowh),
                  pl.BlockSpec((L, GLA_DV), rowh),
                  pl.BlockSpec((L, GLA_DV), rowh),
                  pl.BlockSpec((L, LANES), lambda b, h, c: (b * nc + c, 0)),
                  pl.BlockSpec((LANES, GLA_DK), lambda b, h, c: (0, h)),
                  pl.BlockSpec((1, GLA_DK), lambda b, h, c: (0, h)),
                  pl.BlockSpec((1, GLA_DV), lambda b, h, c: (0, 0))],
        out_specs=pl.BlockSpec((L, GLA_DV), rowh),
        out_shape=jax.ShapeDtypeStruct((m, GLA_HEADS * GLA_DV), BF16),
        scratch_shapes=[pltpu.VMEM((GLA_DV, GLA_DK), F32)],
        compiler_params=_cparams(("arbitrary", "arbitrary", "arbitrary")),
        name="gla",
    )(gq, gk, gv, gg, misc, w2, b_gk.reshape(1, -1), norm_g.reshape(1, -1))


def _cmp_kernel(z_ref, pa_ref, pb_ref, wa_ref, wb_ref, w2_ref, o_ref, *, prec):
    z = z_ref[0]
    if prec is None:
        cast = lambda t: t.astype(BF16)
    else:
        cast = lambda t: t
    first = _dot(cast(z + pa_ref[...]), cast(wa_ref[...]), prec)
    second = _dot(cast(z + pb_ref[...]), cast(wb_ref[...]), prec)
    n = first.shape[0]
    hid = _silu(first + pltpu.roll(second, n - 1, 0))
    for g in range(NSA_GROUPS):
        o_ref[0, g] = _dot(cast(hid[:, g * NSA_CMP_HIDDEN:(g + 1) * NSA_CMP_HIDDEN]), cast(w2_ref[...]), prec)


def _compress(z, cmp_pos, w1, w2, bsz, seq, prec):
    nb = seq // NSA_CMP_STRIDE
    width = NSA_CMP_STRIDE * NSA_GROUPS * NSA_HEAD_DIM
    zr = z.reshape(bsz, nb, width)
    per = NSA_CMP_BLOCK // NSA_CMP_STRIDE
    eye = jnp.eye(NSA_GROUPS, dtype=F32)
    wbig = jnp.einsum("ldj,gh->lgdhj", w1.reshape(NSA_CMP_BLOCK, NSA_HEAD_DIM, NSA_CMP_HIDDEN), eye)
    wbig = wbig.reshape(per, width, NSA_GROUPS * NSA_CMP_HIDDEN)
    posb = jnp.broadcast_to(cmp_pos[:, None, :], (NSA_CMP_BLOCK, NSA_GROUPS, NSA_HEAD_DIM)).reshape(per, 1, width)
    assert per == 2
    c2 = lambda b: (0, 0)
    return pl.pallas_call(
        functools.partial(_cmp_kernel, prec=prec),
        grid=(bsz,),
        in_specs=[pl.BlockSpec((1, nb, width), lambda b: (b, 0, 0)),
                  pl.BlockSpec((1, width), c2), pl.BlockSpec((1, width), c2),
                  pl.BlockSpec((width, NSA_GROUPS * NSA_CMP_HIDDEN), c2),
                  pl.BlockSpec((width, NSA_GROUPS * NSA_CMP_HIDDEN), c2),
                  pl.BlockSpec((NSA_CMP_HIDDEN, NSA_HEAD_DIM), c2)],
        out_specs=pl.BlockSpec((1, NSA_GROUPS, nb, NSA_HEAD_DIM), lambda b: (b, 0, 0, 0)),
        out_shape=jax.ShapeDtypeStruct((bsz, NSA_GROUPS, nb, NSA_HEAD_DIM), F32),
        compiler_params=_cparams(("arbitrary",)),
        name="nsa_compress",
    )(zr, posb[0], posb[1], wbig[0], wbig[1], w2)


def _nsa_select_kernel(q_ref, kc_ref, vct_ref, ovt_ref, oct_ref, selt_ref, *, tq, n_sel, slopes, scale):
    g = pl.program_id(1)
    qi = pl.program_id(2)
    ncmp = kc_ref.shape[2]
    nslc = ovt_ref.shape[0]
    qpos_c = qi * tq + lax.broadcasted_iota(jnp.int32, (ncmp, tq), 1)
    cmp_end = lax.broadcasted_iota(jnp.int32, (ncmp, tq), 0) * NSA_CMP_STRIDE + (NSA_CMP_BLOCK - 1)
    dist = (qpos_c - cmp_end).astype(F32)
    mask = dist >= 0.0
    kc = kc_ref[0, 0]
    vct = vct_ref[0, 0]
    psum = jnp.zeros((ncmp, tq), F32)
    for h in range(NSA_HG):
        s = _dot_nt(kc, q_ref[0, 0, h], HI) * scale - _by_group(g, slopes, h) * dist
        s = jnp.where(mask, s, NEG_INF)
        mx = jnp.max(s, axis=0, keepdims=True)
        ex = jnp.where(mask, jnp.exp(s - mx), 0.0)
        p = ex / jnp.maximum(jnp.sum(ex, axis=0, keepdims=True), 1e-30)
        oct_ref[0, 0, h] = _dot(vct, p, HI)
        psum = psum + p
    imp = _dot(ovt_ref[...], psum, HI)
    blk = lax.broadcasted_iota(jnp.int32, (nslc, tq), 0)
    qpos = qi * tq + lax.broadcasted_iota(jnp.int32, (nslc, tq), 1)
    forced = (blk == 0) | (blk == (qpos >> _SEL_SHIFT))
    avail = blk * NSA_SEL_BLOCK <= qpos
    imp = jnp.where(forced, NSA_FORCE, jnp.where(avail, imp, -1.0))
    blkf = blk.astype(F32)
    sel = jnp.zeros((nslc, tq), F32)
    for _ in range(n_sel):
        mx = jnp.max(imp, axis=0, keepdims=True)
        first = jnp.min(jnp.where(imp == mx, blkf, float(nslc)), axis=0, keepdims=True)
        one = blkf == first
        sel = jnp.where(one, 1.0, sel)
        imp = jnp.where(one, -2.0, imp)
    selt_ref[0, 0] = sel


def _nsa_select(qf, k_cmp, v_cmp_t, seq, tq=128):
    bsz, ng, hg, _, dh = qf.shape
    ncmp = k_cmp.shape[2]
    nslc = seq // NSA_SEL_BLOCK
    n_sel = min(NSA_N_SEL, nslc)
    c_start = np.arange(ncmp) * NSA_CMP_STRIDE
    s_start = np.arange(nslc) * NSA_SEL_BLOCK
    ovt = ((c_start[None, :] < s_start[:, None] + NSA_SEL_BLOCK)
           & (c_start[None, :] + NSA_CMP_BLOCK > s_start[:, None])).astype(np.float32)
    ovt[:, (seq - NSA_CMP_BLOCK) // NSA_CMP_STRIDE + 1:] = 0.0
    return pl.pallas_call(
        functools.partial(_nsa_select_kernel, tq=tq, n_sel=n_sel, slopes=_alibi_table(NSA_HEADS, NSA_GROUPS),
                          scale=NSA_HEAD_DIM ** -0.5),
        grid=(bsz, ng, seq // tq),
        in_specs=[pl.BlockSpec((1, 1, hg, tq, dh), lambda b, g, i: (b, g, 0, i, 0)),
                  pl.BlockSpec((1, 1, ncmp, dh), lambda b, g, i: (b, g, 0, 0)),
                  pl.BlockSpec((1, 1, dh, ncmp), lambda b, g, i: (b, g, 0, 0)),
                  pl.BlockSpec((nslc, ncmp), lambda b, g, i: (0, 0))],
        out_specs=[pl.BlockSpec((1, 1, hg, dh, tq), lambda b, g, i: (b, g, 0, 0, i)),
                   pl.BlockSpec((1, 1, nslc, tq), lambda b, g, i: (b, g, 0, i))],
        out_shape=[jax.ShapeDtypeStruct((bsz, ng, hg, dh, seq), F32),
                   jax.ShapeDtypeStruct((bsz, ng, nslc, seq), F32)],
        compiler_params=_cparams(("arbitrary", "arbitrary", "arbitrary")),
        name="nsa_select",
    )(qf, k_cmp, v_cmp_t, jnp.asarray(ovt))


def _nsa_combine_kernel(oc_ref, os_ref, ow_ref, misc_ref, o_ref):
    gates = 1.0 / (1.0 + jnp.exp(-misc_ref[...]))
    for h in range(NSA_HEADS):
        c0 = GLA_GATE_RANK + 3 * h
        o = gates[:, c0:c0 + 1] * oc_ref[0, h] + gates[:, c0 + 1:c0 + 2] * os_ref[0, h] \
            + gates[:, c0 + 2:c0 + 3] * ow_ref[0, h]
        o_ref[0, h] = o.astype(o_ref.dtype)


def _nsa_combine(o_c, o_s, o_w, misc, bsz, seq, tq=512):
    tpb = seq // tq
    hm = lambda i: (i // tpb, 0, i % tpb, 0)
    spec = pl.BlockSpec((1, NSA_HEADS, tq, NSA_HEAD_DIM), hm)
    return pl.pallas_call(
        _nsa_combine_kernel,
        grid=(bsz * tpb,),
        in_specs=[spec, spec, spec, pl.BlockSpec((tq, LANES), lambda i: (i, 0))],
        out_specs=spec,
        out_shape=jax.ShapeDtypeStruct((bsz, NSA_HEADS, seq, NSA_HEAD_DIM), BF16),
        compiler_params=_cparams(("arbitrary",)),
        name="nsa_combine",
    )(o_c, o_s, o_w, misc)


L0_SEGS = ((0, 1024), (1024, 2560), (2560, 2688), (2688, 3072), (3072, 3328))
L1_SEGS = ((0, 512), (512, 1024), (1024, 2048), (2048, 3072), (3072, 3584), (3584, 3712), (3712, 3840),
           (3840, 3968), (3968, 4096), (4096, 4224), (4224, 4352), (4352, 4480))


def _pack_w_in0(w):
    d = w.shape[0]
    a = SSD_INNER + SSD_CONV_CH
    dt = w[:, a:a + SSD_HEADS]
    qa = w[:, a + SSD_HEADS:a + SSD_HEADS + MLA_Q_RANK]
    kva = w[:, a + SSD_HEADS + MLA_Q_RANK:a + SSD_HEADS + MLA_Q_RANK + MLA_KV_RANK]
    kpe = w[:, a + SSD_HEADS + MLA_Q_RANK + MLA_KV_RANK:]
    pad = jnp.zeros((d, LANES - SSD_HEADS - MLA_ROPE), F32)
    return jnp.concatenate([w[:, :a], dt, kpe, pad, qa, kva], axis=1).astype(BF16)


def _pack_w_in1(w):
    d = w.shape[0]
    qk = 2 * GLA_HEADS * GLA_DK
    vv = GLA_HEADS * GLA_DV
    o = qk + vv
    glr = w[:, o:o + GLA_GATE_RANK]
    gg = w[:, o + GLA_GATE_RANK:o + GLA_GATE_RANK + vv]
    o2 = o + GLA_GATE_RANK + vv
    nsa = w[:, o2:o2 + NSA_HEADS * NSA_HEAD_DIM + 6 * NSA_GROUPS * NSA_HEAD_DIM]
    ngate = w[:, o2 + NSA_HEADS * NSA_HEAD_DIM + 6 * NSA_GROUPS * NSA_HEAD_DIM:]
    pad = jnp.zeros((d, LANES - GLA_GATE_RANK - 3 * NSA_HEADS), F32)
    return jnp.concatenate([w[:, :o], gg, nsa, glr, ngate, pad], axis=1).astype(BF16)


def _mixer0_parts(h_args, positions, bsz, seq, w_in, conv_w, conv_b, dt_bias, a_log, d_skip, ssm_norm_g,
                  q_a_norm_g, w_q_b, kv_a_norm_g, w_kv_b):
    z, xbc, misc, q_a, kv_a = _proj_in(*h_args, _pack_w_in0(w_in), L0_SEGS, seq)
    y_ssd = _ssd(z, xbc, misc, conv_w, conv_b, dt_bias, a_log, d_skip, ssm_norm_g, bsz, seq)
    q, k, v = _mla_prep(q_a, kv_a, misc, positions, q_a_norm_g, kv_a_norm_g, w_q_b, w_kv_b, bsz, seq)
    o = _flash(q[:, :, None], k, v, None, tq=512, tk=512, scale=(MLA_NOPE + MLA_ROPE) ** -0.5, out_dtype=BF16)
    o_mla = o.reshape(bsz, MLA_HEADS, seq, MLA_V).transpose(0, 2, 1, 3).reshape(bsz * seq, MLA_HEADS * MLA_V)
    return y_ssd, o_mla


def _heads_major(t, bsz, seq, lead):
    return t.reshape((bsz, seq) + lead + (NSA_HEAD_DIM,)).transpose((0,) + tuple(range(2, 2 + len(lead))) + (1, len(lead) + 2))


def _mixer1_parts(h_args, bsz, seq, w_in, w_gk2, b_gk, gla_norm_g, cmp_pos, cmp_k_w1, cmp_k_w2, cmp_v_w1, cmp_v_w2):
    gq, gk, gv, gg, nq, kc, vc, ks, vs, kw, vw, misc = _proj_in(*h_args, _pack_w_in1(w_in), L1_SEGS, seq)
    o_gla = _gla(gq, gk, gv, gg, misc, w_gk2, b_gk, gla_norm_g, bsz, seq)
    k_cmp = _compress(kc, cmp_pos, cmp_k_w1, cmp_k_w2, bsz, seq, HI)
    v_cmp = _compress(vc, cmp_pos, cmp_v_w1, cmp_v_w2, bsz, seq, None)
    qf = _heads_major(nq, bsz, seq, (NSA_GROUPS, NSA_HG))
    oc_t, sel_t = _nsa_select(qf, k_cmp, v_cmp.transpose(0, 1, 3, 2), seq)
    sel = sel_t.transpose(0, 1, 3, 2).astype(BF16)
    if sel.shape[-1] < LANES:
        sel = jnp.pad(sel, ((0, 0), (0, 0), (0, 0), (0, LANES - sel.shape[-1])))
    hm = lambda t: _heads_major(t, bsz, seq, (NSA_GROUPS,)).astype(BF16)
    scale = NSA_HEAD_DIM ** -0.5
    slopes = _alibi_table(NSA_HEADS, NSA_GROUPS)
    o_s = _flash(qf, hm(ks), hm(vs), sel, tq=256, tk=512, scale=scale, slopes=slopes)
    o_w = _flash(qf, hm(kw), hm(vw), None, tq=NSA_WINDOW, tk=NSA_WINDOW, scale=scale, slopes=slopes,
                 window=NSA_WINDOW)
    shp = (bsz, NSA_HEADS, seq, NSA_HEAD_DIM)
    o_nsa = _nsa_combine(oc_t.transpose(0, 1, 2, 4, 3).reshape(shp), o_s.reshape(shp), o_w.reshape(shp), misc,
                         bsz, seq)
    o_nsa = o_nsa.transpose(0, 2, 1, 3).reshape(bsz * seq, NSA_HEADS * NSA_HEAD_DIM)
    return o_gla, o_nsa


def kernel(x, c, positions, l0_ada_w, l0_ada_b, l0_mix_pre_g, l0_mix_post_g, l0_w_in, l0_conv_w, l0_conv_b, l0_dt_bias, l0_a_log, l0_d_skip, l0_ssm_norm_g, l0_q_a_norm_g, l0_w_q_b, l0_kv_a_norm_g, l0_w_kv_b, l0_w_out, l0_ffn_pre_g, l0_ffn_post_g, l0_w_gate, l0_w_up, l0_w_down, l1_ada_w, l1_ada_b, l1_mix_pre_g, l1_mix_post_g, l1_w_in, l1_w_gk2, l1_b_gk, l1_gla_norm_g, l1_cmp_pos, l1_cmp_k_w1, l1_cmp_k_w2, l1_cmp_v_w1, l1_cmp_v_w2, l1_w_out, l1_ffn_pre_g, l1_ffn_post_g, l1_w_gate, l1_w_up, l1_w_down):
    bsz, seq, d = x.shape
    x2 = x.reshape(bsz * seq, d)

    def sublayers(x2, ada_w, ada_b, pre_m, post_m, mixer, w_out, pre_f, post_f, w_gate, w_up, w_down):
        shift_m, scale_m, gate_m, shift_f, scale_f, gate_f = _ada(c, ada_w, ada_b)
        a, b = mixer((x2, pre_m, scale_m, shift_m))
        ka = a.shape[1]
        x2 = _out_res(x2, a, b, w_out[:ka].astype(BF16), w_out[ka:].astype(BF16), post_m, gate_m, seq)
        return _ffn(x2, pre_f, scale_f, shift_f, w_gate.astype(BF16), w_up.astype(BF16), w_down.astype(BF16),
                    post_f, gate_f, seq)

    x2 = sublayers(
        x2, l0_ada_w, l0_ada_b, l0_mix_pre_g, l0_mix_post_g,
        lambda h: _mixer0_parts(h, positions, bsz, seq, l0_w_in, l0_conv_w, l0_conv_b, l0_dt_bias, l0_a_log,
                                l0_d_skip, l0_ssm_norm_g, l0_q_a_norm_g, l0_w_q_b, l0_kv_a_norm_g, l0_w_kv_b),
        l0_w_out, l0_ffn_pre_g, l0_ffn_post_g, l0_w_gate, l0_w_up, l0_w_down)
    x2 = sublayers(
        x2, l1_ada_w, l1_ada_b, l1_mix_pre_g, l1_mix_post_g,
        lambda h: _mixer1_parts(h, bsz, seq, l1_w_in, l1_w_gk2, l1_b_gk, l1_gla_norm_g, l1_cmp_pos,
                                l1_cmp_k_w1, l1_cmp_k_w2, l1_cmp_v_w1, l1_cmp_v_w2),
        l1_w_out, l1_ffn_pre_g, l1_ffn_post_g, l1_w_gate, l1_w_up, l1_w_down)
    return x2.reshape(bsz, seq, d)
```

```python
import functools

import numpy as np
import jax
import jax.numpy as jnp
from jax import lax
from jax.experimental import pallas as pl
from jax.experimental.pallas import tpu as pltpu

F32, BF16 = jnp.float32, jnp.bfloat16
HI = lax.Precision.HIGHEST
LANES = 128
VMEM_LIMIT = 48 * 1024 * 1024

NORM_EPS = 1e-6
NEG_INF = -1e30
LOG2E = 1.4426950408889634
N_MOD = 6

SSD_HEADS, SSD_HEAD_DIM, SSD_STATE, SSD_GROUPS, SSD_CONV = 16, 64, 128, 2, 4
SSD_INNER = SSD_HEADS * SSD_HEAD_DIM
SSD_CONV_CH = SSD_INNER + 2 * SSD_GROUPS * SSD_STATE
SSD_CHUNK = 128

MLA_HEADS, MLA_Q_RANK, MLA_KV_RANK, MLA_NOPE, MLA_ROPE, MLA_V = 8, 384, 256, 64, 32, 64
ROPE_THETA = 10000.0

GLA_HEADS, GLA_DK, GLA_DV, GLA_GATE_RANK, GLA_GATE_NORM = 4, 128, 256, 16, 16.0
GLA_CHUNK = 128
GLA_SUB = 16

NSA_HEADS, NSA_GROUPS, NSA_HEAD_DIM = 8, 2, 64
NSA_HG = NSA_HEADS // NSA_GROUPS
NSA_CMP_BLOCK, NSA_CMP_STRIDE, NSA_CMP_HIDDEN = 32, 16, 256
NSA_SEL_BLOCK, NSA_N_SEL, NSA_WINDOW, NSA_FORCE = 64, 16, 512, 1e4
_SEL_SHIFT = NSA_SEL_BLOCK.bit_length() - 1
assert 1 << _SEL_SHIFT == NSA_SEL_BLOCK


def _cparams(sem):
    return pltpu.CompilerParams(dimension_semantics=sem, vmem_limit_bytes=VMEM_LIMIT)


def _dot(a, b, prec=None):
    return jnp.dot(a, b, preferred_element_type=F32, precision=prec)


def _dot_nt(a, b, prec=None):
    return lax.dot_general(a, b, (((1,), (1,)), ((), ())), preferred_element_type=F32, precision=prec)


def _silu(x):
    return x * (1.0 / (1.0 + jnp.exp(-x)))


def _softplus(x):
    return jnp.maximum(x, 0.0) + jnp.log1p(jnp.exp(-jnp.abs(x)))


def _rms(x, g):
    return x * lax.rsqrt(jnp.mean(x * x, axis=-1, keepdims=True) + NORM_EPS) * g


def _tril(n):
    return lax.broadcasted_iota(jnp.int32, (n, n), 0) >= lax.broadcasted_iota(jnp.int32, (n, n), 1)


def _alibi_table(n, groups):
    s = 2.0 ** (-8.0 * np.arange(1, n + 1) / n)
    return [[float(v) for v in row] for row in s.reshape(groups, n // groups)]


def _by_group(g, table, h):
    val = table[0][h]
    for gi in range(1, len(table)):
        val = jnp.where(g == gi, table[gi][h], val)
    return val


def _ada_kernel(c_ref, w_ref, b_ref, o_ref):
    o_ref[...] = _dot(_silu(c_ref[...]), w_ref[...], HI) + b_ref[...]


def _ada(c, w, b):
    bsz, d = c.shape
    n = w.shape[1]
    rows = 8
    cp = jnp.zeros((rows, d), F32).at[:bsz].set(c)
    tn = 1024
    out = pl.pallas_call(
        _ada_kernel,
        grid=(n // tn,),
        in_specs=[pl.BlockSpec((rows, d), lambda j: (0, 0)),
                  pl.BlockSpec((d, tn), lambda j: (0, j)),
                  pl.BlockSpec((1, tn), lambda j: (0, j))],
        out_specs=pl.BlockSpec((rows, tn), lambda j: (0, j)),
        out_shape=jax.ShapeDtypeStruct((rows, n), F32),
        compiler_params=_cparams(("arbitrary",)),
        name="ada",
    )(cp, w, b.reshape(1, n))
    return [m.reshape(bsz, 1, d) for m in jnp.split(out[:bsz], N_MOD, axis=-1)]


def _proj_in_kernel(x_ref, g_ref, sc_ref, sh_ref, w_ref, *o_refs, segs):
    h = (_rms(x_ref[...], g_ref[...]) * (1.0 + sc_ref[0]) + sh_ref[0]).astype(BF16)
    for (a, b), o_ref in zip(segs, o_refs):
        o_ref[...] = _dot(h, w_ref[:, a:b]).astype(o_ref.dtype)


def _proj_in(x2, g, scale, shift, w, segs, seq, tm=256):
    m, d = x2.shape
    n = w.shape[1]
    tpb = seq // tm
    return pl.pallas_call(
        functools.partial(_proj_in_kernel, segs=segs),
        grid=(m // tm,),
        in_specs=[pl.BlockSpec((tm, d), lambda i: (i, 0)),
                  pl.BlockSpec((1, d), lambda i: (0, 0)),
                  pl.BlockSpec((1, 1, d), lambda i: (i // tpb, 0, 0)),
                  pl.BlockSpec((1, 1, d), lambda i: (i // tpb, 0, 0)),
                  pl.BlockSpec((d, n), lambda i: (0, 0))],
        out_specs=[pl.BlockSpec((tm, b - a), lambda i: (i, 0)) for a, b in segs],
        out_shape=[jax.ShapeDtypeStruct((m, b - a), F32) for a, b in segs],
        compiler_params=_cparams(("arbitrary",)),
        name="proj_in",
    )(x2, g.reshape(1, d), scale, shift, w)


def _out_res_kernel(x_ref, a_ref, b_ref, wa_ref, wb_ref, g_ref, gate_ref, o_ref):
    y = _dot(a_ref[...], wa_ref[...]) + _dot(b_ref[...], wb_ref[...])
    o_ref[...] = x_ref[...] + gate_ref[0] * _rms(y, g_ref[...])


def _out_res(x2, a, b, wa, wb, g, gate, seq, tm=512):
    m, d = x2.shape
    ka, kb = a.shape[1], b.shape[1]
    tpb = seq // tm
    return pl.pallas_call(
        _out_res_kernel,
        grid=(m // tm,),
        in_specs=[pl.BlockSpec((tm, d), lambda i: (i, 0)),
                  pl.BlockSpec((tm, ka), lambda i: (i, 0)),
                  pl.BlockSpec((tm, kb), lambda i: (i, 0)),
                  pl.BlockSpec((ka, d), lambda i: (0, 0)),
                  pl.BlockSpec((kb, d), lambda i: (0, 0)),
                  pl.BlockSpec((1, d), lambda i: (0, 0)),
                  pl.BlockSpec((1, 1, d), lambda i: (i // tpb, 0, 0))],
        out_specs=pl.BlockSpec((tm, d), lambda i: (i, 0)),
        out_shape=jax.ShapeDtypeStruct((m, d), F32),
        compiler_params=_cparams(("arbitrary",)),
        name="out_res",
    )(x2, a, b, wa, wb, g.reshape(1, d), gate)


def _ffn_kernel(x_ref, gpre_ref, sc_ref, sh_ref, wg_ref, wu_ref, wd_ref, gpost_ref, gate_ref, o_ref,
                h_scr, acc_scr):
    j = pl.program_id(1)

    @pl.when(j == 0)
    def _():
        h_scr[...] = (_rms(x_ref[...], gpre_ref[...]) * (1.0 + sc_ref[0]) + sh_ref[0]).astype(BF16)
        acc_scr[...] = jnp.zeros_like(acc_scr)

    h = h_scr[...]
    act = (_silu(_dot(h, wg_ref[...])) * _dot(h, wu_ref[...])).astype(BF16)
    acc_scr[...] += _dot(act, wd_ref[...])

    @pl.when(j == pl.num_programs(1) - 1)
    def _():
        o_ref[...] = x_ref[...] + gate_ref[0] * _rms(acc_scr[...], gpost_ref[...])


def _ffn(x2, gpre, scale, shift, wg, wu, wd, gpost, gate, seq, tm=512, th=1408):
    m, d = x2.shape
    hid = wg.shape[1]
    tpb = seq // tm
    return pl.pallas_call(
        _ffn_kernel,
        grid=(m // tm, hid // th),
        in_specs=[pl.BlockSpec((tm, d), lambda i, j: (i, 0)),
                  pl.BlockSpec((1, d), lambda i, j: (0, 0)),
                  pl.BlockSpec((1, 1, d), lambda i, j: (i // tpb, 0, 0)),
                  pl.BlockSpec((1, 1, d), lambda i, j: (i // tpb, 0, 0)),
                  pl.BlockSpec((d, th), lambda i, j: (0, j)),
                  pl.BlockSpec((d, th), lambda i, j: (0, j)),
                  pl.BlockSpec((th, d), lambda i, j: (j, 0)),
                  pl.BlockSpec((1, d), lambda i, j: (0, 0)),
                  pl.BlockSpec((1, 1, d), lambda i, j: (i // tpb, 0, 0))],
        out_specs=pl.BlockSpec((tm, d), lambda i, j: (i, 0)),
        out_shape=jax.ShapeDtypeStruct((m, d), F32),
        scratch_shapes=[pltpu.VMEM((tm, d), BF16), pltpu.VMEM((tm, d), F32)],
        compiler_params=_cparams(("arbitrary", "arbitrary")),
        name="ffn",
    )(x2, gpre.reshape(1, d), scale, shift, wg, wu, wd, gpost.reshape(1, d), gate)


def _ssd_kernel(z_ref, xbc_ref, misc_ref, cw_ref, cb_ref, dtb_ref, alog_ref, dsk_ref, ng_ref, e_ref, o_ref,
                ext_scr, st_scr):
    L = SSD_CHUNK
    gsz = SSD_INNER // SSD_GROUPS
    hpg = SSD_HEADS // SSD_GROUPS
    pad = 8

    @pl.when(pl.program_id(1) == 0)
    def _():
        ext_scr[0:pad, :] = jnp.zeros((pad, SSD_CONV_CH), F32)
        st_scr[...] = jnp.zeros_like(st_scr)

    xt = xbc_ref[...]
    ext_scr[pad:pad + L, :] = xt
    acc = cb_ref[...] + cw_ref[0:1, :] * ext_scr[pad - 3:pad - 3 + L, :]
    for k in range(1, SSD_CONV):
        acc = acc + cw_ref[k:k + 1, :] * ext_scr[pad - 3 + k:pad - 3 + k + L, :]
    ext_scr[0:pad, :] = xt[L - pad:L, :]
    xbc = _silu(acc)
    xs = xbc[:, :SSD_INNER]

    e = e_ref[...]
    dt = _softplus(misc_ref[...] + dtb_ref[...])
    adt = dt * (-jnp.exp(alog_ref[...]))
    a_cs = _dot(jnp.where(_tril(L), 1.0, 0.0), adt, HI)
    a_cs_t = a_cs.T
    ea = jnp.exp(a_cs)
    ea_e = _dot(ea, e, HI)
    dec_e = _dot(jnp.exp(a_cs[L - 1:L, :] - a_cs), e, HI)
    xd = xs * _dot(dt, e, HI)
    tril = _tril(L)

    ys = []
    for g in range(SSD_GROUPS):
        bg = xbc[:, SSD_INNER + g * SSD_STATE:SSD_INNER + (g + 1) * SSD_STATE]
        cg = xbc[:, SSD_INNER + (SSD_GROUPS + g) * SSD_STATE:SSD_INNER + (SSD_GROUPS + g + 1) * SSD_STATE]
        bg16, cg16 = bg.astype(BF16), cg.astype(BF16)
        gmat = _dot_nt(cg16, bg16)
        cols = slice(g * gsz, (g + 1) * gsz)
        xdg = xd[:, cols]
        st = st_scr[g]
        y_off = _dot(cg16, st.astype(BF16)) * ea_e[:, cols]
        st_scr[g] = st * ea_e[L - 1:L, cols] + _dot(bg.T.astype(BF16), (xdg * dec_e[:, cols]).astype(BF16))
        yd = []
        for h in range(hpg):
            hh = g * hpg + h
            seg = a_cs[:, hh:hh + 1] - a_cs_t[hh:hh + 1, :]
            lmat = jnp.where(tril, jnp.exp(seg), 0.0)
            yd.append(_dot((gmat * lmat).astype(BF16),
                           xdg[:, h * SSD_HEAD_DIM:(h + 1) * SSD_HEAD_DIM].astype(BF16)))
        y = jnp.concatenate(yd, axis=-1) + y_off + dsk_ref[:, cols] * xs[:, cols]
        y = y * _silu(z_ref[:, cols])
        ys.append(_rms(y, ng_ref[:, cols]))
    o_ref[...] = jnp.concatenate(ys, axis=-1).astype(o_ref.dtype)


def _ssd(z, xbc, misc, conv_w, conv_b, dt_bias, a_log, d_skip, norm_g, bsz, seq):
    L = SSD_CHUNK
    nc = seq // L
    m = bsz * seq
    e = np.zeros((LANES, SSD_INNER), np.float32)
    for h in range(SSD_HEADS):
        e[h, h * SSD_HEAD_DIM:(h + 1) * SSD_HEAD_DIM] = 1.0
    pad128 = lambda v: jnp.zeros((1, LANES), F32).at[0, :v.shape[0]].set(v)
    row = lambda i, c: (i * nc + c, 0)
    const = lambda i, c: (0, 0)
    return pl.pallas_call(
        _ssd_kernel,
        grid=(bsz, nc),
        in_specs=[pl.BlockSpec((L, SSD_INNER), row),
                  pl.BlockSpec((L, SSD_CONV_CH), row),
                  pl.BlockSpec((L, LANES), row),
                  pl.BlockSpec((SSD_CONV, SSD_CONV_CH), const),
                  pl.BlockSpec((1, SSD_CONV_CH), const),
                  pl.BlockSpec((1, LANES), const),
                  pl.BlockSpec((1, LANES), const),
                  pl.BlockSpec((1, SSD_INNER), const),
                  pl.BlockSpec((1, SSD_INNER), const),
                  pl.BlockSpec((LANES, SSD_INNER), const)],
        out_specs=pl.BlockSpec((L, SSD_INNER), row),
        out_shape=jax.ShapeDtypeStruct((m, SSD_INNER), BF16),
        scratch_shapes=[pltpu.VMEM((L + 8, SSD_CONV_CH), F32),
                        pltpu.VMEM((SSD_GROUPS, SSD_STATE, SSD_INNER // SSD_GROUPS), F32)],
        compiler_params=_cparams(("arbitrary", "arbitrary")),
        name="ssd",
    )(z, xbc, misc, conv_w, conv_b.reshape(1, -1), pad128(dt_bias), pad128(a_log),
      jnp.repeat(d_skip, SSD_HEAD_DIM).reshape(1, -1), norm_g.reshape(1, -1), jnp.asarray(e))


MLA_D = LANES
MLA_HEADS_PER_STEP = 2
_R1 = MLA_NOPE
_R2 = MLA_NOPE + MLA_ROPE // 2


def _mla_prep_kernel(qa_ref, kva_ref, misc_ref, pos_ref, gq_ref, gkv_ref, wq_ref, wqs_ref, wk_ref, wv_ref,
                     pk_ref, pks_ref, invf_ref, sgn_ref, q_ref, k_ref, v_ref):
    ang = pos_ref[...].astype(F32) * invf_ref[...]
    cos = jnp.cos(ang)
    sin = jnp.sin(ang) * sgn_ref[...]
    nq = _rms(qa_ref[...], gq_ref[...]).astype(BF16)
    nkv = _rms(kva_ref[...], gkv_ref[...]).astype(BF16)
    misc = misc_ref[...]
    k_rot = _dot(misc, pk_ref[...], HI) * cos + _dot(misc, pks_ref[...], HI) * sin
    for h in range(MLA_HEADS):
        qh = _dot(nq, wq_ref[h]) * cos + _dot(nq, wqs_ref[h]) * sin
        q_ref[0, h] = (qh * ((MLA_NOPE + MLA_ROPE) ** -0.5 * LOG2E)).astype(q_ref.dtype)
        k_ref[0, h] = (_dot(nkv, wk_ref[h]) + k_rot).astype(k_ref.dtype)
        v_ref[0, h] = _dot(nkv, wv_ref[h]).astype(v_ref.dtype)


def _mla_prep(q_a, kv_a, misc, positions, gq, gkv, w_q_b, w_kv_b, bsz, seq, tm=512):
    m = bsz * seq
    half = MLA_ROPE // 2
    dq = MLA_NOPE + MLA_ROPE
    wq3 = w_q_b.reshape(MLA_Q_RANK, MLA_HEADS, dq).transpose(1, 0, 2)
    zq = jnp.zeros((MLA_HEADS, MLA_Q_RANK, MLA_D - dq), F32)
    wq = jnp.concatenate([wq3, zq], axis=-1).astype(BF16)
    wqs = jnp.concatenate([jnp.zeros((MLA_HEADS, MLA_Q_RANK, MLA_NOPE), F32), wq3[..., _R2:dq], wq3[..., _R1:_R2], zq],
                          axis=-1).astype(BF16)
    wkv3 = w_kv_b.reshape(MLA_KV_RANK, MLA_HEADS, MLA_NOPE + MLA_V).transpose(1, 0, 2)
    wk = jnp.concatenate([wkv3[..., :MLA_NOPE], jnp.zeros((MLA_HEADS, MLA_KV_RANK, MLA_D - MLA_NOPE), F32)],
                         axis=-1).astype(BF16)
    wv = wkv3[..., MLA_NOPE:].astype(BF16)
    pk = np.zeros((LANES, MLA_D), np.float32)
    pks = np.zeros((LANES, MLA_D), np.float32)
    invf = np.zeros((1, MLA_D), np.float32)
    sgn = np.zeros((1, MLA_D), np.float32)
    inv = (ROPE_THETA ** (-np.arange(0, MLA_ROPE, 2) / MLA_ROPE)).astype(np.float32)
    for i in range(half):
        pk[SSD_HEADS + i, _R1 + i] = 1.0
        pk[SSD_HEADS + half + i, _R2 + i] = 1.0
        pks[SSD_HEADS + half + i, _R1 + i] = 1.0
        pks[SSD_HEADS + i, _R2 + i] = 1.0
        invf[0, _R1 + i] = invf[0, _R2 + i] = inv[i]
        sgn[0, _R1 + i], sgn[0, _R2 + i] = -1.0, 1.0
    tpb = seq // tm
    c2 = lambda i: (0, 0)
    c3 = lambda i: (0, 0, 0)
    hm = lambda i: (i // tpb, 0, i % tpb, 0)
    return pl.pallas_call(
        _mla_prep_kernel,
        grid=(m // tm,),
        in_specs=[pl.BlockSpec((tm, MLA_Q_RANK), lambda i: (i, 0)),
                  pl.BlockSpec((tm, MLA_KV_RANK), lambda i: (i, 0)),
                  pl.BlockSpec((tm, LANES), lambda i: (i, 0)),
                  pl.BlockSpec((tm, 1), lambda i: (i, 0)),
                  pl.BlockSpec((1, MLA_Q_RANK), c2),
                  pl.BlockSpec((1, MLA_KV_RANK), c2),
                  pl.BlockSpec((MLA_HEADS, MLA_Q_RANK, MLA_D), c3),
                  pl.BlockSpec((MLA_HEADS, MLA_Q_RANK, MLA_D), c3),
                  pl.BlockSpec((MLA_HEADS, MLA_KV_RANK, MLA_D), c3),
                  pl.BlockSpec((MLA_HEADS, MLA_KV_RANK, MLA_V), c3),
                  pl.BlockSpec((LANES, MLA_D), c2),
                  pl.BlockSpec((LANES, MLA_D), c2),
                  pl.BlockSpec((1, MLA_D), c2),
                  pl.BlockSpec((1, MLA_D), c2)],
        out_specs=[pl.BlockSpec((1, MLA_HEADS, tm, MLA_D), hm),
                   pl.BlockSpec((1, MLA_HEADS, tm, MLA_D), hm),
                   pl.BlockSpec((1, MLA_HEADS, tm, MLA_V), hm)],
        out_shape=[jax.ShapeDtypeStruct((bsz, MLA_HEADS, seq, MLA_D), BF16),
                   jax.ShapeDtypeStruct((bsz, MLA_HEADS, seq, MLA_D), BF16),
                   jax.ShapeDtypeStruct((bsz, MLA_HEADS, seq, MLA_V), BF16)],
        compiler_params=_cparams(("arbitrary",)),
        name="mla_prep",
    )(q_a, kv_a, misc, positions.reshape(m, 1), gq.reshape(1, -1), gkv.reshape(1, -1), wq, wqs, wk, wv,
      jnp.asarray(pk), jnp.asarray(pks), jnp.asarray(invf), jnp.asarray(sgn))


_FIRST, _LAST, _MASKED = 1, 2, 4
SOFTMAX_ROWS = 64


def _flash_schedule(seq, tq, tk, window):
    qi_l, kb_l, fl_l = [], [], []
    for qi in range(seq // tq):
        q0, q1 = qi * tq, qi * tq + tq - 1
        first = q0 // tk
        lo = 0 if window is None else max(0, q0 - window + 1) // tk
        blocks = [first] + [b for b in range(lo, q1 // tk + 1) if b != first]
        for n, kb in enumerate(blocks):
            k0, k1 = kb * tk, kb * tk + tk - 1
            masked = k1 > q0 or (window is not None and q1 - k0 >= window)
            qi_l.append(qi)
            kb_l.append(kb)
            fl_l.append((_FIRST if n == 0 else 0) | (_LAST if n == len(blocks) - 1 else 0) | (_MASKED if masked else 0))
    return [jnp.asarray(np.asarray(t, np.int32)) for t in (qi_l, kb_l, fl_l)]


def _flash_kernel(qi_ref, kb_ref, fl_ref, *refs, hg, tq, tk, dv, slopes, window, has_sel, kv_per_head):
    if has_sel:
        q_ref, k_ref, v_ref, selb_ref, o_ref, m_scr, l_scr, alpha_scr, acc_scr, s_scr, p_scr, bias_scr = refs
    else:
        q_ref, k_ref, v_ref, o_ref, m_scr, l_scr, alpha_scr, acc_scr, s_scr, p_scr, bias_scr = refs
    g = pl.program_id(1)
    st = pl.program_id(2)
    qi, kb, fl = qi_ref[st], kb_ref[st], fl_ref[st]
    masked = (fl & _MASKED) != 0
    rows = hg * tq
    rc = SOFTMAX_ROWS

    @pl.when((fl & _FIRST) != 0)
    def _():
        m_scr[...] = jnp.full_like(m_scr, NEG_INF)
        l_scr[...] = jnp.zeros_like(l_scr)
        acc_scr[...] = jnp.zeros_like(acc_scr)

    def mask_bias():
        dist = (qi * tq + lax.broadcasted_iota(jnp.int32, (tq, tk), 0)) - \
               (kb * tk + lax.broadcasted_iota(jnp.int32, (tq, tk), 1))
        ok = dist >= 0
        if window is not None:
            ok = ok & (dist < window)
        return jnp.where(ok, 0.0, NEG_INF)

    def step(with_mask):
        with_bias = has_sel or with_mask
        if has_sel:
            blk = lax.broadcasted_iota(jnp.int32, (LANES, tk), 0)
            key = kb * tk + lax.broadcasted_iota(jnp.int32, (LANES, tk), 1)
            expand = jnp.where((key >> _SEL_SHIFT) == blk, 1.0, 0.0).astype(BF16)
            bias = _dot(selb_ref[0, 0], expand)
            bias_scr[...] = bias + mask_bias() if with_mask else bias
        elif with_mask:
            bias_scr[...] = mask_bias()
        kv = (lambda ref, h: ref[0, 0, h]) if kv_per_head else (lambda ref, h: ref[0, 0])
        for h in range(hg):
            s_scr[h * tq:(h + 1) * tq, :] = _dot_nt(q_ref[0, 0, h], kv(k_ref, h))
        for h in range(hg):
            if slopes is not None:
                rel = (kb * tk - qi * tq + lax.broadcasted_iota(jnp.int32, (1, tk), 1)).astype(F32)
                arow = (_by_group(g, slopes, h) * LOG2E) * rel
            for c in range(tq // rc):
                rs = slice(h * tq + c * rc, h * tq + (c + 1) * rc)
                s = s_scr[rs, :]
                if with_bias:
                    s = s + bias_scr[c * rc:(c + 1) * rc, :]
                if slopes is not None:
                    s = s + arow
                if with_bias or slopes is not None:
                    s_scr[rs, :] = s
                m_prev = m_scr[rs, :]
                m_next = jnp.maximum(m_prev, jnp.max(s, axis=1, keepdims=True))
                alpha_scr[rs, :] = jnp.exp2(m_prev - m_next)
                m_scr[rs, :] = m_next
        for h in range(hg):
            for c in range(tq // rc):
                rs = slice(h * tq + c * rc, h * tq + (c + 1) * rc)
                p = jnp.exp2(s_scr[rs, :] - jnp.tile(m_scr[rs, :], (1, tk // LANES)))
                alpha = alpha_scr[rs, :]
                l_scr[rs, :] = alpha * l_scr[rs, :] + jnp.sum(p, axis=1, keepdims=True)
                acc_scr[rs, :] = acc_scr[rs, :] * alpha[:, :dv]
                p_scr[rs, :] = p.astype(BF16)
        for h in range(hg):
            hs = slice(h * tq, (h + 1) * tq)
            acc_scr[hs, :] += _dot(p_scr[hs, :], kv(v_ref, h))

    pl.when(masked)(lambda: step(True))
    pl.when(jnp.logical_not(masked))(lambda: step(False))

    @pl.when((fl & _LAST) != 0)
    def _():
        out = acc_scr[...] / jnp.maximum(l_scr[:, :dv], 1e-30)
        o_ref[0, 0] = out.reshape(hg, tq, dv).astype(o_ref.dtype)


def _flash(q, k, v, selb, *, tq, tk, slopes=None, window=None, out_dtype=F32):
    bsz, ng, hg, seq, d = q.shape
    dv = v.shape[-1]
    kv_per_head = k.ndim == 5
    sched = _flash_schedule(seq, tq, tk, window)
    rows = hg * tq
    if kv_per_head:
        kv_spec = lambda w: pl.BlockSpec((1, 1, hg, tk, w), lambda b, g, s, qi, kb, fl: (b, g, 0, kb[s], 0))
    else:
        kv_spec = lambda w: pl.BlockSpec((1, 1, tk, w), lambda b, g, s, qi, kb, fl: (b, g, kb[s], 0))
    in_specs = [pl.BlockSpec((1, 1, hg, tq, d), lambda b, g, s, qi, kb, fl: (b, g, 0, qi[s], 0)),
                kv_spec(d), kv_spec(dv)]
    args = [q, k, v]
    if selb is not None:
        in_specs.append(pl.BlockSpec((1, 1, tq, LANES), lambda b, g, s, qi, kb, fl: (b, g, qi[s], 0)))
        args.append(selb)
    return pl.pallas_call(
        functools.partial(_flash_kernel, hg=hg, tq=tq, tk=tk, dv=dv, slopes=slopes, window=window,
                          has_sel=selb is not None, kv_per_head=kv_per_head),
        grid_spec=pltpu.PrefetchScalarGridSpec(
            num_scalar_prefetch=3,
            grid=(bsz, ng, int(sched[0].shape[0])),
            in_specs=in_specs,
            out_specs=pl.BlockSpec((1, 1, hg, tq, dv), lambda b, g, s, qi, kb, fl: (b, g, 0, qi[s], 0)),
            scratch_shapes=[pltpu.VMEM((rows, LANES), F32), pltpu.VMEM((rows, LANES), F32),
                            pltpu.VMEM((rows, LANES), F32),
                            pltpu.VMEM((rows, dv), F32), pltpu.VMEM((rows, tk), F32),
                            pltpu.VMEM((rows, tk), BF16), pltpu.VMEM((tq, tk), F32)]),
        out_shape=jax.ShapeDtypeStruct((bsz, ng, hg, seq, dv), out_dtype),
        compiler_params=_cparams(("arbitrary", "arbitrary", "arbitrary")),
        name="flash",
    )(*sched, *args)


def _gla_kernel(q_ref, k_ref, v_ref, gg_ref, misc_ref, w2_ref, bgk_ref, ng_ref, o_ref, st_scr):
    L = GLA_CHUNK

    @pl.when(pl.program_id(2) == 0)
    def _():
        st_scr[...] = jnp.zeros_like(st_scr)

    zg = _dot(misc_ref[...], w2_ref[...], HI) + bgk_ref[...]
    log_a = -_softplus(-zg) * (1.0 / GLA_GATE_NORM)
    tril = _tril(L)
    bc = _dot(jnp.where(tril, 1.0, 0.0), log_a, HI)
    q = q_ref[...] * (GLA_DK ** -0.5)
    k = k_ref[...]
    v16 = v_ref[...].astype(BF16)
    st = st_scr[...]
    o = _dot_nt((q * jnp.exp(bc)).astype(BF16), st.astype(BF16))
    row = lax.broadcasted_iota(jnp.int32, (L, GLA_DK), 0)
    att = []
    for i in range(L // GLA_SUB):
        r0 = i * GLA_SUB
        ref = bc[r0:r0 + 1, :]
        qi = q[r0:r0 + GLA_SUB] * jnp.exp(bc[r0:r0 + GLA_SUB] - ref)
        ki = k * jnp.exp(jnp.where(row < r0 + GLA_SUB, ref - bc, 0.0))
        att.append(_dot_nt(qi.astype(BF16), ki.astype(BF16)))
    att = jnp.where(tril, jnp.concatenate(att, axis=0), 0.0)
    o = o + _dot(att.astype(BF16), v16)
    b_last = bc[L - 1:L, :]
    kd = (k * jnp.exp(b_last - bc)).astype(BF16)
    st_scr[...] = st * jnp.exp(b_last) + _dot(v_ref[...].T.astype(BF16), kd)
    o_ref[...] = (_rms(o, ng_ref[...]) * _silu(gg_ref[...])).astype(o_ref.dtype)


def _gla(gq, gk, gv, gg, misc, w_gk2, b_gk, norm_g, bsz, seq):
    L = GLA_CHUNK
    nc = seq // L
    m = bsz * seq
    w2 = jnp.zeros((LANES, GLA_HEADS * GLA_DK), F32).at[:GLA_GATE_RANK].set(w_gk2)
    rowh = lambda b, h, c: (b * nc + c, h)
    return pl.pallas_call(
        _gla_kernel,
        grid=(bsz, GLA_HEADS, nc),
        in_specs=[pl.BlockSpec((L, GLA_DK), rowh),
                  pl.BlockSpec((L, GLA_DK), rowh),
                  pl.BlockSpec((L, GLA_DV), rowh),
                  pl.BlockSpec((L, GLA_DV), rowh),
                  pl.BlockSpec((L, LANES), lambda b, h, c: (b * nc + c, 0)),
                  pl.BlockSpec((LANES, GLA_DK), lambda b, h, c: (0, h)),
                  pl.BlockSpec((1, GLA_DK), lambda b, h, c: (0, h)),
                  pl.BlockSpec((1, GLA_DV), lambda b, h, c: (0, 0))],
        out_specs=pl.BlockSpec((L, GLA_DV), rowh),
        out_shape=jax.ShapeDtypeStruct((m, GLA_HEADS * GLA_DV), BF16),
        scratch_shapes=[pltpu.VMEM((GLA_DV, GLA_DK), F32)],
        compiler_params=_cparams(("arbitrary", "arbitrary", "arbitrary")),
        name="gla",
    )(gq, gk, gv, gg, misc, w2, b_gk.reshape(1, -1), norm_g.reshape(1, -1))


def _cmp_kernel(z_ref, pa_ref, pb_ref, wa_ref, wb_ref, w2_ref, o_ref, *, prec):
    z = z_ref[0]
    if prec is None:
        cast = lambda t: t.astype(BF16)
    else:
        cast = lambda t: t
    first = _dot(cast(z + pa_ref[...]), cast(wa_ref[...]), prec)
    second = _dot(cast(z + pb_ref[...]), cast(wb_ref[...]), prec)
    n = first.shape[0]
    hid = _silu(first + pltpu.roll(second, n - 1, 0))
    for g in range(NSA_GROUPS):
        o_ref[0, g] = _dot(cast(hid[:, g * NSA_CMP_HIDDEN:(g + 1) * NSA_CMP_HIDDEN]), cast(w2_ref[...]), prec)


def _compress(z, cmp_pos, w1, w2, bsz, seq, prec):
    nb = seq // NSA_CMP_STRIDE
    width = NSA_CMP_STRIDE * NSA_GROUPS * NSA_HEAD_DIM
    zr = z.reshape(bsz, nb, width)
    per = NSA_CMP_BLOCK // NSA_CMP_STRIDE
    eye = jnp.eye(NSA_GROUPS, dtype=F32)
    wbig = jnp.einsum("ldj,gh->lgdhj", w1.reshape(NSA_CMP_BLOCK, NSA_HEAD_DIM, NSA_CMP_HIDDEN), eye)
    wbig = wbig.reshape(per, width, NSA_GROUPS * NSA_CMP_HIDDEN)
    posb = jnp.broadcast_to(cmp_pos[:, None, :], (NSA_CMP_BLOCK, NSA_GROUPS, NSA_HEAD_DIM)).reshape(per, 1, width)
    assert per == 2
    c2 = lambda b: (0, 0)
    return pl.pallas_call(
        functools.partial(_cmp_kernel, prec=prec),
        grid=(bsz,),
        in_specs=[pl.BlockSpec((1, nb, width), lambda b: (b, 0, 0)),
                  pl.BlockSpec((1, width), c2), pl.BlockSpec((1, width), c2),
                  pl.BlockSpec((width, NSA_GROUPS * NSA_CMP_HIDDEN), c2),
                  pl.BlockSpec((width, NSA_GROUPS * NSA_CMP_HIDDEN), c2),
                  pl.BlockSpec((NSA_CMP_HIDDEN, NSA_HEAD_DIM), c2)],
        out_specs=pl.BlockSpec((1, NSA_GROUPS, nb, NSA_HEAD_DIM), lambda b: (b, 0, 0, 0)),
        out_shape=jax.ShapeDtypeStruct((bsz, NSA_GROUPS, nb, NSA_HEAD_DIM), F32),
        compiler_params=_cparams(("arbitrary",)),
        name="nsa_compress",
    )(zr, posb[0], posb[1], wbig[0], wbig[1], w2)


def _nsa_select_kernel(q_ref, kc_ref, vct_ref, ovt_ref, oct_ref, selt_ref, qs_ref, *, tq, n_sel, slopes, scale):
    g = pl.program_id(1)
    qi = pl.program_id(2)
    ncmp = kc_ref.shape[2]
    nslc = ovt_ref.shape[0]
    qpos_c = qi * tq + lax.broadcasted_iota(jnp.int32, (ncmp, tq), 1)
    cmp_end = lax.broadcasted_iota(jnp.int32, (ncmp, tq), 0) * NSA_CMP_STRIDE + (NSA_CMP_BLOCK - 1)
    dist = (qpos_c - cmp_end).astype(F32)
    mask = dist >= 0.0
    kc = kc_ref[0, 0]
    vct = vct_ref[0, 0]
    psum = jnp.zeros((ncmp, tq), F32)
    for h in range(NSA_HG):
        q = q_ref[0, 0, h]
        qs_ref[0, 0, h] = (q * (scale * LOG2E)).astype(qs_ref.dtype)
        s = _dot_nt(kc, q, HI) * scale - _by_group(g, slopes, h) * dist
        s = jnp.where(mask, s, NEG_INF)
        mx = jnp.max(s, axis=0, keepdims=True)
        ex = jnp.where(mask, jnp.exp(s - mx), 0.0)
        p = ex / jnp.maximum(jnp.sum(ex, axis=0, keepdims=True), 1e-30)
        oct_ref[0, 0, h] = _dot(vct, p, HI)
        psum = psum + p
    imp = _dot(ovt_ref[...], psum, HI)
    blk = lax.broadcasted_iota(jnp.int32, (nslc, tq), 0)
    qpos = qi * tq + lax.broadcasted_iota(jnp.int32, (nslc, tq), 1)
    forced = (blk == 0) | (blk == (qpos >> _SEL_SHIFT))
    avail = blk * NSA_SEL_BLOCK <= qpos
    imp = jnp.where(forced, NSA_FORCE, jnp.where(avail, imp, -1.0))
    blkf = blk.astype(F32)
    selb = jnp.full((nslc, tq), NEG_INF, F32)
    for _ in range(n_sel):
        mx = jnp.max(imp, axis=0, keepdims=True)
        first = jnp.min(jnp.where(imp == mx, blkf, float(nslc)), axis=0, keepdims=True)
        one = blkf == first
        selb = jnp.where(one, 0.0, selb)
        imp = jnp.where(one, -2.0, imp)
    selt_ref[0, 0] = selb


def _nsa_select(qf, k_cmp, v_cmp_t, seq, tq=128):
    bsz, ng, hg, _, dh = qf.shape
    ncmp = k_cmp.shape[2]
    nslc = seq // NSA_SEL_BLOCK
    n_sel = min(NSA_N_SEL, nslc)
    c_start = np.arange(ncmp) * NSA_CMP_STRIDE
    s_start = np.arange(nslc) * NSA_SEL_BLOCK
    ovt = ((c_start[None, :] < s_start[:, None] + NSA_SEL_BLOCK)
           & (c_start[None, :] + NSA_CMP_BLOCK > s_start[:, None])).astype(np.float32)
    ovt[:, (seq - NSA_CMP_BLOCK) // NSA_CMP_STRIDE + 1:] = 0.0
    return pl.pallas_call(
        functools.partial(_nsa_select_kernel, tq=tq, n_sel=n_sel, slopes=_alibi_table(NSA_HEADS, NSA_GROUPS),
                          scale=NSA_HEAD_DIM ** -0.5),
        grid=(bsz, ng, seq // tq),
        in_specs=[pl.BlockSpec((1, 1, hg, tq, dh), lambda b, g, i: (b, g, 0, i, 0)),
                  pl.BlockSpec((1, 1, ncmp, dh), lambda b, g, i: (b, g, 0, 0)),
                  pl.BlockSpec((1, 1, dh, ncmp), lambda b, g, i: (b, g, 0, 0)),
                  pl.BlockSpec((nslc, ncmp), lambda b, g, i: (0, 0))],
        out_specs=[pl.BlockSpec((1, 1, hg, dh, tq), lambda b, g, i: (b, g, 0, 0, i)),
                   pl.BlockSpec((1, 1, nslc, tq), lambda b, g, i: (b, g, 0, i)),
                   pl.BlockSpec((1, 1, hg, tq, dh), lambda b, g, i: (b, g, 0, i, 0))],
        out_shape=[jax.ShapeDtypeStruct((bsz, ng, hg, dh, seq), F32),
                   jax.ShapeDtypeStruct((bsz, ng, nslc, seq), F32),
                   jax.ShapeDtypeStruct((bsz, ng, hg, seq, dh), BF16)],
        compiler_params=_cparams(("arbitrary", "arbitrary", "arbitrary")),
        name="nsa_select",
    )(qf, k_cmp, v_cmp_t, jnp.asarray(ovt))


def _nsa_combine_kernel(oc_ref, os_ref, ow_ref, misc_ref, o_ref):
    gates = 1.0 / (1.0 + jnp.exp(-misc_ref[...]))
    for h in range(NSA_HEADS):
        c0 = GLA_GATE_RANK + 3 * h
        o = gates[:, c0:c0 + 1] * oc_ref[0, h] + gates[:, c0 + 1:c0 + 2] * os_ref[0, h] \
            + gates[:, c0 + 2:c0 + 3] * ow_ref[0, h]
        o_ref[0, h] = o.astype(o_ref.dtype)


def _nsa_combine(o_c, o_s, o_w, misc, bsz, seq, tq=512):
    tpb = seq // tq
    hm = lambda i: (i // tpb, 0, i % tpb, 0)
    spec = pl.BlockSpec((1, NSA_HEADS, tq, NSA_HEAD_DIM), hm)
    return pl.pallas_call(
        _nsa_combine_kernel,
        grid=(bsz * tpb,),
        in_specs=[spec, spec, spec, pl.BlockSpec((tq, LANES), lambda i: (i, 0))],
        out_specs=spec,
        out_shape=jax.ShapeDtypeStruct((bsz, NSA_HEADS, seq, NSA_HEAD_DIM), BF16),
        compiler_params=_cparams(("arbitrary",)),
        name="nsa_combine",
    )(o_c, o_s, o_w, misc)


L0_SEGS = ((0, 1024), (1024, 2560), (2560, 2688), (2688, 3072), (3072, 3328))
L1_SEGS = ((0, 512), (512, 1024), (1024, 2048), (2048, 3072), (3072, 3584), (3584, 3712), (3712, 3840),
           (3840, 3968), (3968, 4096), (4096, 4224), (4224, 4352), (4352, 4480))


def _pack_w_in0(w):
    d = w.shape[0]
    a = SSD_INNER + SSD_CONV_CH
    dt = w[:, a:a + SSD_HEADS]
    qa = w[:, a + SSD_HEADS:a + SSD_HEADS + MLA_Q_RANK]
    kva = w[:, a + SSD_HEADS + MLA_Q_RANK:a + SSD_HEADS + MLA_Q_RANK + MLA_KV_RANK]
    kpe = w[:, a + SSD_HEADS + MLA_Q_RANK + MLA_KV_RANK:]
    pad = jnp.zeros((d, LANES - SSD_HEADS - MLA_ROPE), F32)
    return jnp.concatenate([w[:, :a], dt, kpe, pad, qa, kva], axis=1).astype(BF16)


def _pack_w_in1(w):
    d = w.shape[0]
    qk = 2 * GLA_HEADS * GLA_DK
    vv = GLA_HEADS * GLA_DV
    o = qk + vv
    glr = w[:, o:o + GLA_GATE_RANK]
    gg = w[:, o + GLA_GATE_RANK:o + GLA_GATE_RANK + vv]
    o2 = o + GLA_GATE_RANK + vv
    nsa = w[:, o2:o2 + NSA_HEADS * NSA_HEAD_DIM + 6 * NSA_GROUPS * NSA_HEAD_DIM]
    ngate = w[:, o2 + NSA_HEADS * NSA_HEAD_DIM + 6 * NSA_GROUPS * NSA_HEAD_DIM:]
    pad = jnp.zeros((d, LANES - GLA_GATE_RANK - 3 * NSA_HEADS), F32)
    return jnp.concatenate([w[:, :o], gg, nsa, glr, ngate, pad], axis=1).astype(BF16)


def _mixer0_parts(h_args, positions, bsz, seq, w_in, conv_w, conv_b, dt_bias, a_log, d_skip, ssm_norm_g,
                  q_a_norm_g, w_q_b, kv_a_norm_g, w_kv_b):
    z, xbc, misc, q_a, kv_a = _proj_in(*h_args, _pack_w_in0(w_in), L0_SEGS, seq)
    y_ssd = _ssd(z, xbc, misc, conv_w, conv_b, dt_bias, a_log, d_skip, ssm_norm_g, bsz, seq)
    q, k, v = _mla_prep(q_a, kv_a, misc, positions, q_a_norm_g, kv_a_norm_g, w_q_b, w_kv_b, bsz, seq)
    pair = lambda t: t.reshape(bsz, MLA_HEADS // MLA_HEADS_PER_STEP, MLA_HEADS_PER_STEP, seq, t.shape[-1])
    o = _flash(pair(q), pair(k), pair(v), None, tq=512, tk=512, out_dtype=BF16)
    o_mla = o.reshape(bsz, MLA_HEADS, seq, MLA_V).transpose(0, 2, 1, 3).reshape(bsz * seq, MLA_HEADS * MLA_V)
    return y_ssd, o_mla


def _heads_major(t, bsz, seq, lead):
    return t.reshape((bsz, seq) + lead + (NSA_HEAD_DIM,)).transpose((0,) + tuple(range(2, 2 + len(lead))) + (1, len(lead) + 2))


def _mixer1_parts(h_args, bsz, seq, w_in, w_gk2, b_gk, gla_norm_g, cmp_pos, cmp_k_w1, cmp_k_w2, cmp_v_w1, cmp_v_w2):
    gq, gk, gv, gg, nq, kc, vc, ks, vs, kw, vw, misc = _proj_in(*h_args, _pack_w_in1(w_in), L1_SEGS, seq)
    o_gla = _gla(gq, gk, gv, gg, misc, w_gk2, b_gk, gla_norm_g, bsz, seq)
    k_cmp = _compress(kc, cmp_pos, cmp_k_w1, cmp_k_w2, bsz, seq, HI)
    v_cmp = _compress(vc, cmp_pos, cmp_v_w1, cmp_v_w2, bsz, seq, None)
    qf = _heads_major(nq, bsz, seq, (NSA_GROUPS, NSA_HG))
    oc_t, selb_t, qs = _nsa_select(qf, k_cmp, v_cmp.transpose(0, 1, 3, 2), seq)
    selb = selb_t.transpose(0, 1, 3, 2).astype(BF16)
    if selb.shape[-1] < LANES:
        selb = jnp.pad(selb, ((0, 0), (0, 0), (0, 0), (0, LANES - selb.shape[-1])))
    hm = lambda t: _heads_major(t, bsz, seq, (NSA_GROUPS,)).astype(BF16)
    slopes = _alibi_table(NSA_HEADS, NSA_GROUPS)
    o_s = _flash(qs, hm(ks), hm(vs), selb, tq=256, tk=512, slopes=slopes)
    o_w = _flash(qs, hm(kw), hm(vw), None, tq=NSA_WINDOW, tk=NSA_WINDOW, slopes=slopes, window=NSA_WINDOW)
    shp = (bsz, NSA_HEADS, seq, NSA_HEAD_DIM)
    o_nsa = _nsa_combine(oc_t.transpose(0, 1, 2, 4, 3).reshape(shp), o_s.reshape(shp), o_w.reshape(shp), misc,
                         bsz, seq)
    o_nsa = o_nsa.transpose(0, 2, 1, 3).reshape(bsz * seq, NSA_HEADS * NSA_HEAD_DIM)
    return o_gla, o_nsa


def kernel(x, c, positions, l0_ada_w, l0_ada_b, l0_mix_pre_g, l0_mix_post_g, l0_w_in, l0_conv_w, l0_conv_b, l0_dt_bias, l0_a_log, l0_d_skip, l0_ssm_norm_g, l0_q_a_norm_g, l0_w_q_b, l0_kv_a_norm_g, l0_w_kv_b, l0_w_out, l0_ffn_pre_g, l0_ffn_post_g, l0_w_gate, l0_w_up, l0_w_down, l1_ada_w, l1_ada_b, l1_mix_pre_g, l1_mix_post_g, l1_w_in, l1_w_gk2, l1_b_gk, l1_gla_norm_g, l1_cmp_pos, l1_cmp_k_w1, l1_cmp_k_w2, l1_cmp_v_w1, l1_cmp_v_w2, l1_w_out, l1_ffn_pre_g, l1_ffn_post_g, l1_w_gate, l1_w_up, l1_w_down):
    bsz, seq, d = x.shape
    x2 = x.reshape(bsz * seq, d)

    def sublayers(x2, ada_w, ada_b, pre_m, post_m, mixer, w_out, pre_f, post_f, w_gate, w_up, w_down):
        shift_m, scale_m, gate_m, shift_f, scale_f, gate_f = _ada(c, ada_w, ada_b)
        a, b = mixer((x2, pre_m, scale_m, shift_m))
        ka = a.shape[1]
        x2 = _out_res(x2, a, b, w_out[:ka].astype(BF16), w_out[ka:].astype(BF16), post_m, gate_m, seq)
        return _ffn(x2, pre_f, scale_f, shift_f, w_gate.astype(BF16), w_up.astype(BF16), w_down.astype(BF16),
                    post_f, gate_f, seq)

    x2 = sublayers(
        x2, l0_ada_w, l0_ada_b, l0_mix_pre_g, l0_mix_post_g,
        lambda h: _mixer0_parts(h, positions, bsz, seq, l0_w_in, l0_conv_w, l0_conv_b, l0_dt_bias, l0_a_log,
                                l0_d_skip, l0_ssm_norm_g, l0_q_a_norm_g, l0_w_q_b, l0_kv_a_norm_g, l0_w_kv_b),
        l0_w_out, l0_ffn_pre_g, l0_ffn_post_g, l0_w_gate, l0_w_up, l0_w_down)
    x2 = sublayers(
        x2, l1_ada_w, l1_ada_b, l1_mix_pre_g, l1_mix_post_g,
        lambda h: _mixer1_parts(h, bsz, seq, l1_w_in, l1_w_gk2, l1_b_gk, l1_gla_norm_g, l1_cmp_pos,
                                l1_cmp_k_w1, l1_cmp_k_w2, l1_cmp_v_w1, l1_cmp_v_w2),
        l1_w_out, l1_ffn_pre_g, l1_ffn_post_g, l1_w_gate, l1_w_up, l1_w_down)
    return x2.reshape(bsz, seq, d)
```

```python
import functools

import numpy as np
import jax
import jax.numpy as jnp
from jax import lax
from jax.experimental import pallas as pl
from jax.experimental.pallas import tpu as pltpu

F32, BF16 = jnp.float32, jnp.bfloat16
HI = lax.Precision.HIGHEST
LANES = 128
VMEM_LIMIT = 48 * 1024 * 1024

NORM_EPS = 1e-6
NEG_INF = -1e30
LOG2E = 1.4426950408889634
N_MOD = 6

SSD_HEADS, SSD_HEAD_DIM, SSD_STATE, SSD_GROUPS, SSD_CONV = 16, 64, 128, 2, 4
SSD_INNER = SSD_HEADS * SSD_HEAD_DIM
SSD_CONV_CH = SSD_INNER + 2 * SSD_GROUPS * SSD_STATE
SSD_CHUNK = 128

MLA_HEADS, MLA_Q_RANK, MLA_KV_RANK, MLA_NOPE, MLA_ROPE, MLA_V = 8, 384, 256, 64, 32, 64
ROPE_THETA = 10000.0

GLA_HEADS, GLA_DK, GLA_DV, GLA_GATE_RANK, GLA_GATE_NORM = 4, 128, 256, 16, 16.0
GLA_CHUNK = 128
GLA_SUB = 16

NSA_HEADS, NSA_GROUPS, NSA_HEAD_DIM = 8, 2, 64
NSA_HG = NSA_HEADS // NSA_GROUPS
NSA_CMP_BLOCK, NSA_CMP_STRIDE, NSA_CMP_HIDDEN = 32, 16, 256
NSA_SEL_BLOCK, NSA_N_SEL, NSA_WINDOW, NSA_FORCE = 64, 16, 512, 1e4
_SEL_SHIFT = NSA_SEL_BLOCK.bit_length() - 1
assert 1 << _SEL_SHIFT == NSA_SEL_BLOCK


def _cparams(sem):
    return pltpu.CompilerParams(dimension_semantics=sem, vmem_limit_bytes=VMEM_LIMIT)


def _dot(a, b, prec=None):
    return jnp.dot(a, b, preferred_element_type=F32, precision=prec)


def _dot_nt(a, b, prec=None):
    return lax.dot_general(a, b, (((1,), (1,)), ((), ())), preferred_element_type=F32, precision=prec)


def _split(x, n):
    parts = []
    for _ in range(n):
        p = x.astype(BF16)
        parts.append(p)
        x = x - p.astype(F32)
    return parts


def _dot_01_left(sel, x, n):
    return sum(_dot(sel, p) for p in _split(x, n))


def _dot_01_right(x, sel, n):
    return sum(_dot(p, sel) for p in _split(x, n))


def _dot3(a, b, nt=False):
    f = _dot_nt if nt else _dot
    (ah, al), (bh, bl) = _split(a, 2), _split(b, 2)
    return f(ah, bh) + (f(ah, bl) + f(al, bh))


def _silu(x):
    return x * (1.0 / (1.0 + jnp.exp(-x)))


def _softplus(x):
    return jnp.maximum(x, 0.0) + jnp.log1p(jnp.exp(-jnp.abs(x)))


def _rms(x, g):
    return x * lax.rsqrt(jnp.mean(x * x, axis=-1, keepdims=True) + NORM_EPS) * g


def _tril(n):
    return lax.broadcasted_iota(jnp.int32, (n, n), 0) >= lax.broadcasted_iota(jnp.int32, (n, n), 1)


def _alibi_table(n, groups):
    s = 2.0 ** (-8.0 * np.arange(1, n + 1) / n)
    return [[float(v) for v in row] for row in s.reshape(groups, n // groups)]


def _by_group(g, table, h):
    val = table[0][h]
    for gi in range(1, len(table)):
        val = jnp.where(g == gi, table[gi][h], val)
    return val


def _ada_kernel(c_ref, w_ref, b_ref, o_ref):
    o_ref[...] = _dot(_silu(c_ref[...]), w_ref[...], HI) + b_ref[...]


def _ada(c, w, b):
    bsz, d = c.shape
    n = w.shape[1]
    rows = 8
    cp = jnp.zeros((rows, d), F32).at[:bsz].set(c)
    tn = 1024
    out = pl.pallas_call(
        _ada_kernel,
        grid=(n // tn,),
        in_specs=[pl.BlockSpec((rows, d), lambda j: (0, 0)),
                  pl.BlockSpec((d, tn), lambda j: (0, j)),
                  pl.BlockSpec((1, tn), lambda j: (0, j))],
        out_specs=pl.BlockSpec((rows, tn), lambda j: (0, j)),
        out_shape=jax.ShapeDtypeStruct((rows, n), F32),
        compiler_params=_cparams(("arbitrary",)),
        name="ada",
    )(cp, w, b.reshape(1, n))
    return [m.reshape(bsz, 1, d) for m in jnp.split(out[:bsz], N_MOD, axis=-1)]


def _proj_in_kernel(x_ref, g_ref, sc_ref, sh_ref, w_ref, *o_refs, segs):
    h = (_rms(x_ref[...], g_ref[...]) * (1.0 + sc_ref[0]) + sh_ref[0]).astype(BF16)
    for (a, b), o_ref in zip(segs, o_refs):
        o_ref[...] = _dot(h, w_ref[:, a:b]).astype(o_ref.dtype)


def _proj_in(x2, g, scale, shift, w, segs, seq, tm=256):
    m, d = x2.shape
    n = w.shape[1]
    tpb = seq // tm
    return pl.pallas_call(
        functools.partial(_proj_in_kernel, segs=segs),
        grid=(m // tm,),
        in_specs=[pl.BlockSpec((tm, d), lambda i: (i, 0)),
                  pl.BlockSpec((1, d), lambda i: (0, 0)),
                  pl.BlockSpec((1, 1, d), lambda i: (i // tpb, 0, 0)),
                  pl.BlockSpec((1, 1, d), lambda i: (i // tpb, 0, 0)),
                  pl.BlockSpec((d, n), lambda i: (0, 0))],
        out_specs=[pl.BlockSpec((tm, b - a), lambda i: (i, 0)) for a, b in segs],
        out_shape=[jax.ShapeDtypeStruct((m, b - a), F32) for a, b in segs],
        compiler_params=_cparams(("arbitrary",)),
        name="proj_in",
    )(x2, g.reshape(1, d), scale, shift, w)


def _out_res_kernel(x_ref, a_ref, b_ref, wa_ref, wb_ref, g_ref, gate_ref, o_ref):
    y = _dot(a_ref[...], wa_ref[...]) + _dot(b_ref[...], wb_ref[...])
    o_ref[...] = x_ref[...] + gate_ref[0] * _rms(y, g_ref[...])


def _out_res(x2, a, b, wa, wb, g, gate, seq, tm=512):
    m, d = x2.shape
    ka, kb = a.shape[1], b.shape[1]
    tpb = seq // tm
    return pl.pallas_call(
        _out_res_kernel,
        grid=(m // tm,),
        in_specs=[pl.BlockSpec((tm, d), lambda i: (i, 0)),
                  pl.BlockSpec((tm, ka), lambda i: (i, 0)),
                  pl.BlockSpec((tm, kb), lambda i: (i, 0)),
                  pl.BlockSpec((ka, d), lambda i: (0, 0)),
                  pl.BlockSpec((kb, d), lambda i: (0, 0)),
                  pl.BlockSpec((1, d), lambda i: (0, 0)),
                  pl.BlockSpec((1, 1, d), lambda i: (i // tpb, 0, 0))],
        out_specs=pl.BlockSpec((tm, d), lambda i: (i, 0)),
        out_shape=jax.ShapeDtypeStruct((m, d), F32),
        compiler_params=_cparams(("arbitrary",)),
        name="out_res",
    )(x2, a, b, wa, wb, g.reshape(1, d), gate)


def _ffn_kernel(x_ref, gpre_ref, sc_ref, sh_ref, wg_ref, wu_ref, wd_ref, gpost_ref, gate_ref, o_ref,
                h_scr, acc_scr):
    j = pl.program_id(1)

    @pl.when(j == 0)
    def _():
        h_scr[...] = (_rms(x_ref[...], gpre_ref[...]) * (1.0 + sc_ref[0]) + sh_ref[0]).astype(BF16)
        acc_scr[...] = jnp.zeros_like(acc_scr)

    h = h_scr[...]
    act = (_silu(_dot(h, wg_ref[...])) * _dot(h, wu_ref[...])).astype(BF16)
    acc_scr[...] += _dot(act, wd_ref[...])

    @pl.when(j == pl.num_programs(1) - 1)
    def _():
        o_ref[...] = x_ref[...] + gate_ref[0] * _rms(acc_scr[...], gpost_ref[...])


def _ffn(x2, gpre, scale, shift, wg, wu, wd, gpost, gate, seq, tm=512, th=1408):
    m, d = x2.shape
    hid = wg.shape[1]
    tpb = seq // tm
    return pl.pallas_call(
        _ffn_kernel,
        grid=(m // tm, hid // th),
        in_specs=[pl.BlockSpec((tm, d), lambda i, j: (i, 0)),
                  pl.BlockSpec((1, d), lambda i, j: (0, 0)),
                  pl.BlockSpec((1, 1, d), lambda i, j: (i // tpb, 0, 0)),
                  pl.BlockSpec((1, 1, d), lambda i, j: (i // tpb, 0, 0)),
                  pl.BlockSpec((d, th), lambda i, j: (0, j)),
                  pl.BlockSpec((d, th), lambda i, j: (0, j)),
                  pl.BlockSpec((th, d), lambda i, j: (j, 0)),
                  pl.BlockSpec((1, d), lambda i, j: (0, 0)),
                  pl.BlockSpec((1, 1, d), lambda i, j: (i // tpb, 0, 0))],
        out_specs=pl.BlockSpec((tm, d), lambda i, j: (i, 0)),
        out_shape=jax.ShapeDtypeStruct((m, d), F32),
        scratch_shapes=[pltpu.VMEM((tm, d), BF16), pltpu.VMEM((tm, d), F32)],
        compiler_params=_cparams(("arbitrary", "arbitrary")),
        name="ffn",
    )(x2, gpre.reshape(1, d), scale, shift, wg, wu, wd, gpost.reshape(1, d), gate)


def _ssd_kernel(z_ref, xbc_ref, misc_ref, cw_ref, cb_ref, dtb_ref, alog_ref, dsk_ref, ng_ref, e_ref, o_ref,
                ext_scr, st_scr):
    L = SSD_CHUNK
    gsz = SSD_INNER // SSD_GROUPS
    hpg = SSD_HEADS // SSD_GROUPS
    pad = 8

    @pl.when(pl.program_id(1) == 0)
    def _():
        ext_scr[0:pad, :] = jnp.zeros((pad, SSD_CONV_CH), F32)
        st_scr[...] = jnp.zeros_like(st_scr)

    xt = xbc_ref[...]
    ext_scr[pad:pad + L, :] = xt
    acc = cb_ref[...] + cw_ref[0:1, :] * ext_scr[pad - 3:pad - 3 + L, :]
    for k in range(1, SSD_CONV):
        acc = acc + cw_ref[k:k + 1, :] * ext_scr[pad - 3 + k:pad - 3 + k + L, :]
    ext_scr[0:pad, :] = xt[L - pad:L, :]
    xbc = _silu(acc)
    xs = xbc[:, :SSD_INNER]

    e = e_ref[...]
    dt = _softplus(misc_ref[...] + dtb_ref[...])
    adt = dt * (-jnp.exp(alog_ref[...]))
    tril = _tril(L)
    a_cs = _dot_01_left(jnp.where(tril, 1.0, 0.0).astype(BF16), adt, 3)
    a_cs_t = a_cs.T
    ea = jnp.exp(a_cs)
    ea_e = _dot_01_right(ea, e, 2)
    dec_e = _dot_01_right(jnp.exp(a_cs[L - 1:L, :] - a_cs), e, 2)
    xd = xs * _dot_01_right(dt, e, 2)

    ys = []
    for g in range(SSD_GROUPS):
        bg = xbc[:, SSD_INNER + g * SSD_STATE:SSD_INNER + (g + 1) * SSD_STATE]
        cg = xbc[:, SSD_INNER + (SSD_GROUPS + g) * SSD_STATE:SSD_INNER + (SSD_GROUPS + g + 1) * SSD_STATE]
        bg16, cg16 = bg.astype(BF16), cg.astype(BF16)
        gmat = _dot_nt(cg16, bg16)
        cols = slice(g * gsz, (g + 1) * gsz)
        xdg = xd[:, cols]
        st = st_scr[g]
        y_off = _dot(cg16, st.astype(BF16)) * ea_e[:, cols]
        st_scr[g] = st * ea_e[L - 1:L, cols] + _dot(bg.T.astype(BF16), (xdg * dec_e[:, cols]).astype(BF16))
        yd = []
        for h in range(hpg):
            hh = g * hpg + h
            seg = a_cs[:, hh:hh + 1] - a_cs_t[hh:hh + 1, :]
            lmat = jnp.where(tril, jnp.exp(seg), 0.0)
            yd.append(_dot((gmat * lmat).astype(BF16),
                           xdg[:, h * SSD_HEAD_DIM:(h + 1) * SSD_HEAD_DIM].astype(BF16)))
        y = jnp.concatenate(yd, axis=-1) + y_off + dsk_ref[:, cols] * xs[:, cols]
        y = y * _silu(z_ref[:, cols])
        ys.append(_rms(y, ng_ref[:, cols]))
    o_ref[...] = jnp.concatenate(ys, axis=-1).astype(o_ref.dtype)


def _ssd(z, xbc, misc, conv_w, conv_b, dt_bias, a_log, d_skip, norm_g, bsz, seq):
    L = SSD_CHUNK
    nc = seq // L
    m = bsz * seq
    e = np.zeros((LANES, SSD_INNER), np.float32)
    for h in range(SSD_HEADS):
        e[h, h * SSD_HEAD_DIM:(h + 1) * SSD_HEAD_DIM] = 1.0
    pad128 = lambda v: jnp.zeros((1, LANES), F32).at[0, :v.shape[0]].set(v)
    row = lambda i, c: (i * nc + c, 0)
    const = lambda i, c: (0, 0)
    return pl.pallas_call(
        _ssd_kernel,
        grid=(bsz, nc),
        in_specs=[pl.BlockSpec((L, SSD_INNER), row),
                  pl.BlockSpec((L, SSD_CONV_CH), row),
                  pl.BlockSpec((L, LANES), row),
                  pl.BlockSpec((SSD_CONV, SSD_CONV_CH), const),
                  pl.BlockSpec((1, SSD_CONV_CH), const),
                  pl.BlockSpec((1, LANES), const),
                  pl.BlockSpec((1, LANES), const),
                  pl.BlockSpec((1, SSD_INNER), const),
                  pl.BlockSpec((1, SSD_INNER), const),
                  pl.BlockSpec((LANES, SSD_INNER), const)],
        out_specs=pl.BlockSpec((L, SSD_INNER), row),
        out_shape=jax.ShapeDtypeStruct((m, SSD_INNER), BF16),
        scratch_shapes=[pltpu.VMEM((L + 8, SSD_CONV_CH), F32),
                        pltpu.VMEM((SSD_GROUPS, SSD_STATE, SSD_INNER // SSD_GROUPS), F32)],
        compiler_params=_cparams(("arbitrary", "arbitrary")),
        name="ssd",
    )(z, xbc, misc, conv_w, conv_b.reshape(1, -1), pad128(dt_bias), pad128(a_log),
      jnp.repeat(d_skip, SSD_HEAD_DIM).reshape(1, -1), norm_g.reshape(1, -1), jnp.asarray(e, BF16))


MLA_D = LANES
MLA_HEADS_PER_STEP = 2
_R1 = MLA_NOPE
_R2 = MLA_NOPE + MLA_ROPE // 2


def _mla_prep_kernel(qa_ref, kva_ref, misc_ref, pos_ref, gq_ref, gkv_ref, wq_ref, wqs_ref, wk_ref, wv_ref,
                     pk_ref, pks_ref, invf_ref, sgn_ref, q_ref, k_ref, v_ref):
    ang = pos_ref[...].astype(F32) * invf_ref[...]
    cos = jnp.cos(ang)
    sin = jnp.sin(ang) * sgn_ref[...]
    nq = _rms(qa_ref[...], gq_ref[...]).astype(BF16)
    nkv = _rms(kva_ref[...], gkv_ref[...]).astype(BF16)
    misc = misc_ref[...]
    k_rot = _dot_01_right(misc, pk_ref[...], 3) * cos + _dot_01_right(misc, pks_ref[...], 3) * sin
    for h in range(MLA_HEADS):
        qh = _dot(nq, wq_ref[h]) * cos + _dot(nq, wqs_ref[h]) * sin
        q_ref[0, h] = (qh * ((MLA_NOPE + MLA_ROPE) ** -0.5 * LOG2E)).astype(q_ref.dtype)
        k_ref[0, h] = (_dot(nkv, wk_ref[h]) + k_rot).astype(k_ref.dtype)
        v_ref[0, h] = _dot(nkv, wv_ref[h]).astype(v_ref.dtype)


def _mla_prep(q_a, kv_a, misc, positions, gq, gkv, w_q_b, w_kv_b, bsz, seq, tm=512):
    m = bsz * seq
    half = MLA_ROPE // 2
    dq = MLA_NOPE + MLA_ROPE
    wq3 = w_q_b.reshape(MLA_Q_RANK, MLA_HEADS, dq).transpose(1, 0, 2)
    zq = jnp.zeros((MLA_HEADS, MLA_Q_RANK, MLA_D - dq), F32)
    wq = jnp.concatenate([wq3, zq], axis=-1).astype(BF16)
    wqs = jnp.concatenate([jnp.zeros((MLA_HEADS, MLA_Q_RANK, MLA_NOPE), F32), wq3[..., _R2:dq], wq3[..., _R1:_R2], zq],
                          axis=-1).astype(BF16)
    wkv3 = w_kv_b.reshape(MLA_KV_RANK, MLA_HEADS, MLA_NOPE + MLA_V).transpose(1, 0, 2)
    wk = jnp.concatenate([wkv3[..., :MLA_NOPE], jnp.zeros((MLA_HEADS, MLA_KV_RANK, MLA_D - MLA_NOPE), F32)],
                         axis=-1).astype(BF16)
    wv = wkv3[..., MLA_NOPE:].astype(BF16)
    pk = np.zeros((LANES, MLA_D), np.float32)
    pks = np.zeros((LANES, MLA_D), np.float32)
    invf = np.zeros((1, MLA_D), np.float32)
    sgn = np.zeros((1, MLA_D), np.float32)
    inv = (ROPE_THETA ** (-np.arange(0, MLA_ROPE, 2) / MLA_ROPE)).astype(np.float32)
    for i in range(half):
        pk[SSD_HEADS + i, _R1 + i] = 1.0
        pk[SSD_HEADS + half + i, _R2 + i] = 1.0
        pks[SSD_HEADS + half + i, _R1 + i] = 1.0
        pks[SSD_HEADS + i, _R2 + i] = 1.0
        invf[0, _R1 + i] = invf[0, _R2 + i] = inv[i]
        sgn[0, _R1 + i], sgn[0, _R2 + i] = -1.0, 1.0
    tpb = seq // tm
    c2 = lambda i: (0, 0)
    c3 = lambda i: (0, 0, 0)
    hm = lambda i: (i // tpb, 0, i % tpb, 0)
    return pl.pallas_call(
        _mla_prep_kernel,
        grid=(m // tm,),
        in_specs=[pl.BlockSpec((tm, MLA_Q_RANK), lambda i: (i, 0)),
                  pl.BlockSpec((tm, MLA_KV_RANK), lambda i: (i, 0)),
                  pl.BlockSpec((tm, LANES), lambda i: (i, 0)),
                  pl.BlockSpec((tm, 1), lambda i: (i, 0)),
                  pl.BlockSpec((1, MLA_Q_RANK), c2),
                  pl.BlockSpec((1, MLA_KV_RANK), c2),
                  pl.BlockSpec((MLA_HEADS, MLA_Q_RANK, MLA_D), c3),
                  pl.BlockSpec((MLA_HEADS, MLA_Q_RANK, MLA_D), c3),
                  pl.BlockSpec((MLA_HEADS, MLA_KV_RANK, MLA_D), c3),
                  pl.BlockSpec((MLA_HEADS, MLA_KV_RANK, MLA_V), c3),
                  pl.BlockSpec((LANES, MLA_D), c2),
                  pl.BlockSpec((LANES, MLA_D), c2),
                  pl.BlockSpec((1, MLA_D), c2),
                  pl.BlockSpec((1, MLA_D), c2)],
        out_specs=[pl.BlockSpec((1, MLA_HEADS, tm, MLA_D), hm),
                   pl.BlockSpec((1, MLA_HEADS, tm, MLA_D), hm),
                   pl.BlockSpec((1, MLA_HEADS, tm, MLA_V), hm)],
        out_shape=[jax.ShapeDtypeStruct((bsz, MLA_HEADS, seq, MLA_D), BF16),
                   jax.ShapeDtypeStruct((bsz, MLA_HEADS, seq, MLA_D), BF16),
                   jax.ShapeDtypeStruct((bsz, MLA_HEADS, seq, MLA_V), BF16)],
        compiler_params=_cparams(("arbitrary",)),
        name="mla_prep",
    )(q_a, kv_a, misc, positions.reshape(m, 1), gq.reshape(1, -1), gkv.reshape(1, -1), wq, wqs, wk, wv,
      jnp.asarray(pk, BF16), jnp.asarray(pks, BF16), jnp.asarray(invf), jnp.asarray(sgn))


_FIRST, _LAST, _MASKED = 1, 2, 4
SOFTMAX_ROWS = 64


def _flash_schedule(seq, tq, tk, window):
    qi_l, kb_l, fl_l = [], [], []
    for qi in range(seq // tq):
        q0, q1 = qi * tq, qi * tq + tq - 1
        first = q0 // tk
        lo = 0 if window is None else max(0, q0 - window + 1) // tk
        blocks = [first] + [b for b in range(lo, q1 // tk + 1) if b != first]
        for n, kb in enumerate(blocks):
            k0, k1 = kb * tk, kb * tk + tk - 1
            masked = k1 > q0 or (window is not None and q1 - k0 >= window)
            qi_l.append(qi)
            kb_l.append(kb)
            fl_l.append((_FIRST if n == 0 else 0) | (_LAST if n == len(blocks) - 1 else 0) | (_MASKED if masked else 0))
    return [jnp.asarray(np.asarray(t, np.int32)) for t in (qi_l, kb_l, fl_l)]


def _flash_kernel(qi_ref, kb_ref, fl_ref, *refs, hg, tq, tk, dv, slopes, window, has_sel, kv_per_head):
    if has_sel:
        q_ref, k_ref, v_ref, selb_ref, o_ref, m_scr, l_scr, alpha_scr, acc_scr, s_scr, p_scr, bias_scr = refs
    else:
        q_ref, k_ref, v_ref, o_ref, m_scr, l_scr, alpha_scr, acc_scr, s_scr, p_scr, bias_scr = refs
    g = pl.program_id(1)
    st = pl.program_id(2)
    qi, kb, fl = qi_ref[st], kb_ref[st], fl_ref[st]
    masked = (fl & _MASKED) != 0
    rows = hg * tq
    rc = SOFTMAX_ROWS

    @pl.when((fl & _FIRST) != 0)
    def _():
        m_scr[...] = jnp.full_like(m_scr, NEG_INF)
        l_scr[...] = jnp.zeros_like(l_scr)
        acc_scr[...] = jnp.zeros_like(acc_scr)

    def mask_bias():
        dist = (qi * tq + lax.broadcasted_iota(jnp.int32, (tq, tk), 0)) - \
               (kb * tk + lax.broadcasted_iota(jnp.int32, (tq, tk), 1))
        ok = dist >= 0
        if window is not None:
            ok = ok & (dist < window)
        return jnp.where(ok, 0.0, NEG_INF)

    def step(with_mask):
        with_bias = has_sel or with_mask
        if has_sel:
            blk = lax.broadcasted_iota(jnp.int32, (LANES, tk), 0)
            key = kb * tk + lax.broadcasted_iota(jnp.int32, (LANES, tk), 1)
            expand = jnp.where((key >> _SEL_SHIFT) == blk, 1.0, 0.0).astype(BF16)
            bias = _dot(selb_ref[0, 0], expand)
            bias_scr[...] = bias + mask_bias() if with_mask else bias
        elif with_mask:
            bias_scr[...] = mask_bias()
        kv = (lambda ref, h: ref[0, 0, h]) if kv_per_head else (lambda ref, h: ref[0, 0])
        for h in range(hg):
            s_scr[h * tq:(h + 1) * tq, :] = _dot_nt(q_ref[0, 0, h], kv(k_ref, h))
        for h in range(hg):
            if slopes is not None:
                rel = (kb * tk - qi * tq + lax.broadcasted_iota(jnp.int32, (1, tk), 1)).astype(F32)
                arow = (_by_group(g, slopes, h) * LOG2E) * rel
            for c in range(tq // rc):
                rs = slice(h * tq + c * rc, h * tq + (c + 1) * rc)
                s = s_scr[rs, :]
                if with_bias:
                    s = s + bias_scr[c * rc:(c + 1) * rc, :]
                if slopes is not None:
                    s = s + arow
                if with_bias or slopes is not None:
                    s_scr[rs, :] = s
                m_prev = m_scr[rs, :]
                m_next = jnp.maximum(m_prev, jnp.max(s, axis=1, keepdims=True))
                alpha_scr[rs, :] = jnp.exp2(m_prev - m_next)
                m_scr[rs, :] = m_next
        for h in range(hg):
            for c in range(tq // rc):
                rs = slice(h * tq + c * rc, h * tq + (c + 1) * rc)
                p = jnp.exp2(s_scr[rs, :] - jnp.tile(m_scr[rs, :], (1, tk // LANES)))
                alpha = alpha_scr[rs, :]
                l_scr[rs, :] = alpha * l_scr[rs, :] + jnp.sum(p, axis=1, keepdims=True)
                acc_scr[rs, :] = acc_scr[rs, :] * alpha[:, :dv]
                p_scr[rs, :] = p.astype(BF16)
        for h in range(hg):
            hs = slice(h * tq, (h + 1) * tq)
            acc_scr[hs, :] += _dot(p_scr[hs, :], kv(v_ref, h))

    pl.when(masked)(lambda: step(True))
    pl.when(jnp.logical_not(masked))(lambda: step(False))

    @pl.when((fl & _LAST) != 0)
    def _():
        out = acc_scr[...] / jnp.maximum(l_scr[:, :dv], 1e-30)
        o_ref[0, 0] = out.reshape(hg, tq, dv).astype(o_ref.dtype)


def _flash(q, k, v, selb, *, tq, tk, slopes=None, window=None, out_dtype=F32):
    bsz, ng, hg, seq, d = q.shape
    dv = v.shape[-1]
    kv_per_head = k.ndim == 5
    sched = _flash_schedule(seq, tq, tk, window)
    rows = hg * tq
    if kv_per_head:
        kv_spec = lambda w: pl.BlockSpec((1, 1, hg, tk, w), lambda b, g, s, qi, kb, fl: (b, g, 0, kb[s], 0))
    else:
        kv_spec = lambda w: pl.BlockSpec((1, 1, tk, w), lambda b, g, s, qi, kb, fl: (b, g, kb[s], 0))
    in_specs = [pl.BlockSpec((1, 1, hg, tq, d), lambda b, g, s, qi, kb, fl: (b, g, 0, qi[s], 0)),
                kv_spec(d), kv_spec(dv)]
    args = [q, k, v]
    if selb is not None:
        in_specs.append(pl.BlockSpec((1, 1, tq, LANES), lambda b, g, s, qi, kb, fl: (b, g, qi[s], 0)))
        args.append(selb)
    return pl.pallas_call(
        functools.partial(_flash_kernel, hg=hg, tq=tq, tk=tk, dv=dv, slopes=slopes, window=window,
                          has_sel=selb is not None, kv_per_head=kv_per_head),
        grid_spec=pltpu.PrefetchScalarGridSpec(
            num_scalar_prefetch=3,
            grid=(bsz, ng, int(sched[0].shape[0])),
            in_specs=in_specs,
            out_specs=pl.BlockSpec((1, 1, hg, tq, dv), lambda b, g, s, qi, kb, fl: (b, g, 0, qi[s], 0)),
            scratch_shapes=[pltpu.VMEM((rows, LANES), F32), pltpu.VMEM((rows, LANES), F32),
                            pltpu.VMEM((rows, LANES), F32),
                            pltpu.VMEM((rows, dv), F32), pltpu.VMEM((rows, tk), F32),
                            pltpu.VMEM((rows, tk), BF16), pltpu.VMEM((tq, tk), F32)]),
        out_shape=jax.ShapeDtypeStruct((bsz, ng, hg, seq, dv), out_dtype),
        compiler_params=_cparams(("arbitrary", "arbitrary", "arbitrary")),
        name="flash",
    )(*sched, *args)


def _gla_kernel(q_ref, k_ref, v_ref, gg_ref, misc_ref, w2_ref, bgk_ref, ng_ref, o_ref, st_scr):
    L = GLA_CHUNK

    @pl.when(pl.program_id(2) == 0)
    def _():
        st_scr[...] = jnp.zeros_like(st_scr)

    zg = _dot3(misc_ref[...], w2_ref[...]) + bgk_ref[...]
    log_a = -_softplus(-zg) * (1.0 / GLA_GATE_NORM)
    tril = _tril(L)
    bc = _dot_01_left(jnp.where(tril, 1.0, 0.0).astype(BF16), log_a, 3)
    q = q_ref[...] * (GLA_DK ** -0.5)
    k = k_ref[...]
    v16 = v_ref[...].astype(BF16)
    st = st_scr[...]
    o = _dot_nt((q * jnp.exp(bc)).astype(BF16), st.astype(BF16))
    row = lax.broadcasted_iota(jnp.int32, (L, GLA_DK), 0)
    att = []
    for i in range(L // GLA_SUB):
        r0 = i * GLA_SUB
        ref = bc[r0:r0 + 1, :]
        qi = q[r0:r0 + GLA_SUB] * jnp.exp(bc[r0:r0 + GLA_SUB] - ref)
        ki = k * jnp.exp(jnp.where(row < r0 + GLA_SUB, ref - bc, 0.0))
        att.append(_dot_nt(qi.astype(BF16), ki.astype(BF16)))
    att = jnp.where(tril, jnp.concatenate(att, axis=0), 0.0)
    o = o + _dot(att.astype(BF16), v16)
    b_last = bc[L - 1:L, :]
    kd = (k * jnp.exp(b_last - bc)).astype(BF16)
    st_scr[...] = st * jnp.exp(b_last) + _dot(v_ref[...].T.astype(BF16), kd)
    o_ref[...] = (_rms(o, ng_ref[...]) * _silu(gg_ref[...])).astype(o_ref.dtype)


def _gla(gq, gk, gv, gg, misc, w_gk2, b_gk, norm_g, bsz, seq):
    L = GLA_CHUNK
    nc = seq // L
    m = bsz * seq
    w2 = jnp.zeros((LANES, GLA_HEADS * GLA_DK), F32).at[:GLA_GATE_RANK].set(w_gk2)
    rowh = lambda b, h, c: (b * nc + c, h)
    return pl.pallas_call(
        _gla_kernel,
        grid=(bsz, GLA_HEADS, nc),
        in_specs=[pl.BlockSpec((L, GLA_DK), rowh),
                  pl.BlockSpec((L, GLA_DK), rowh),
                  pl.BlockSpec((L, GLA_DV), rowh),
                  pl.BlockSpec((L, GLA_DV), rowh),
                  pl.BlockSpec((L, LANES), lambda b, h, c: (b * nc + c, 0)),
                  pl.BlockSpec((LANES, GLA_DK), lambda b, h, c: (0, h)),
                  pl.BlockSpec((1, GLA_DK), lambda b, h, c: (0, h)),
                  pl.BlockSpec((1, GLA_DV), lambda b, h, c: (0, 0))],
        out_specs=pl.BlockSpec((L, GLA_DV), rowh),
        out_shape=jax.ShapeDtypeStruct((m, GLA_HEADS * GLA_DV), BF16),
        scratch_shapes=[pltpu.VMEM((GLA_DV, GLA_DK), F32)],
        compiler_params=_cparams(("arbitrary", "arbitrary", "arbitrary")),
        name="gla",
    )(gq, gk, gv, gg, misc, w2, b_gk.reshape(1, -1), norm_g.reshape(1, -1))


def _cmp_kernel(z_ref, pa_ref, pb_ref, wa_ref, wb_ref, w2_ref, o_ref, *, prec):
    z = z_ref[0]
    if prec is None:
        cast = lambda t: t.astype(BF16)
    else:
        cast = lambda t: t
    first = _dot(cast(z + pa_ref[...]), cast(wa_ref[...]), prec)
    second = _dot(cast(z + pb_ref[...]), cast(wb_ref[...]), prec)
    n = first.shape[0]
    hid = _silu(first + pltpu.roll(second, n - 1, 0))
    for g in range(NSA_GROUPS):
        o_ref[0, g] = _dot(cast(hid[:, g * NSA_CMP_HIDDEN:(g + 1) * NSA_CMP_HIDDEN]), cast(w2_ref[...]), prec)


def _compress(z, cmp_pos, w1, w2, bsz, seq, prec):
    nb = seq // NSA_CMP_STRIDE
    width = NSA_CMP_STRIDE * NSA_GROUPS * NSA_HEAD_DIM
    zr = z.reshape(bsz, nb, width)
    per = NSA_CMP_BLOCK // NSA_CMP_STRIDE
    eye = jnp.eye(NSA_GROUPS, dtype=F32)
    wbig = jnp.einsum("ldj,gh->lgdhj", w1.reshape(NSA_CMP_BLOCK, NSA_HEAD_DIM, NSA_CMP_HIDDEN), eye)
    wbig = wbig.reshape(per, width, NSA_GROUPS * NSA_CMP_HIDDEN)
    posb = jnp.broadcast_to(cmp_pos[:, None, :], (NSA_CMP_BLOCK, NSA_GROUPS, NSA_HEAD_DIM)).reshape(per, 1, width)
    assert per == 2
    c2 = lambda b: (0, 0)
    return pl.pallas_call(
        functools.partial(_cmp_kernel, prec=prec),
        grid=(bsz,),
        in_specs=[pl.BlockSpec((1, nb, width), lambda b: (b, 0, 0)),
                  pl.BlockSpec((1, width), c2), pl.BlockSpec((1, width), c2),
                  pl.BlockSpec((width, NSA_GROUPS * NSA_CMP_HIDDEN), c2),
                  pl.BlockSpec((width, NSA_GROUPS * NSA_CMP_HIDDEN), c2),
                  pl.BlockSpec((NSA_CMP_HIDDEN, NSA_HEAD_DIM), c2)],
        out_specs=pl.BlockSpec((1, NSA_GROUPS, nb, NSA_HEAD_DIM), lambda b: (b, 0, 0, 0)),
        out_shape=jax.ShapeDtypeStruct((bsz, NSA_GROUPS, nb, NSA_HEAD_DIM), F32),
        compiler_params=_cparams(("arbitrary",)),
        name="nsa_compress",
    )(zr, posb[0], posb[1], wbig[0], wbig[1], w2)


def _nsa_select_kernel(q_ref, kc_ref, vct_ref, ovt_ref, oct_ref, selt_ref, qs_ref, *, tq, n_sel, slopes, scale):
    g = pl.program_id(1)
    qi = pl.program_id(2)
    ncmp = kc_ref.shape[2]
    nslc = ovt_ref.shape[0]
    qpos_c = qi * tq + lax.broadcasted_iota(jnp.int32, (ncmp, tq), 1)
    cmp_end = lax.broadcasted_iota(jnp.int32, (ncmp, tq), 0) * NSA_CMP_STRIDE + (NSA_CMP_BLOCK - 1)
    dist = (qpos_c - cmp_end).astype(F32)
    mask = dist >= 0.0
    kc_hi, kc_lo = _split(kc_ref[0, 0], 2)
    vct = vct_ref[0, 0].astype(BF16)
    psum = jnp.zeros((ncmp, tq), F32)
    for h in range(NSA_HG):
        q = q_ref[0, 0, h]
        qs_ref[0, 0, h] = (q * (scale * LOG2E)).astype(qs_ref.dtype)
        q_hi, q_lo = _split(q, 2)
        qk = _dot_nt(kc_hi, q_hi) + (_dot_nt(kc_hi, q_lo) + _dot_nt(kc_lo, q_hi))
        s = qk * scale - _by_group(g, slopes, h) * dist
        s = jnp.where(mask, s, NEG_INF)
        mx = jnp.max(s, axis=0, keepdims=True)
        ex = jnp.where(mask, jnp.exp(s - mx), 0.0)
        p = ex / jnp.maximum(jnp.sum(ex, axis=0, keepdims=True), 1e-30)
        oct_ref[0, 0, h] = _dot(vct, p.astype(BF16))
        psum = psum + p
    imp = _dot_01_left(ovt_ref[...], psum, 2)
    blk = lax.broadcasted_iota(jnp.int32, (nslc, tq), 0)
    qpos = qi * tq + lax.broadcasted_iota(jnp.int32, (nslc, tq), 1)
    forced = (blk == 0) | (blk == (qpos >> _SEL_SHIFT))
    avail = blk * NSA_SEL_BLOCK <= qpos
    imp = jnp.where(forced, NSA_FORCE, jnp.where(avail, imp, -1.0))
    blkf = blk.astype(F32)
    selb = jnp.full((nslc, tq), NEG_INF, F32)
    for _ in range(n_sel):
        mx = jnp.max(imp, axis=0, keepdims=True)
        first = jnp.min(jnp.where(imp == mx, blkf, float(nslc)), axis=0, keepdims=True)
        one = blkf == first
        selb = jnp.where(one, 0.0, selb)
        imp = jnp.where(one, -2.0, imp)
    selt_ref[0, 0] = selb


def _nsa_select(qf, k_cmp, v_cmp_t, seq, tq=256):
    bsz, ng, hg, _, dh = qf.shape
    ncmp = k_cmp.shape[2]
    nslc = seq // NSA_SEL_BLOCK
    n_sel = min(NSA_N_SEL, nslc)
    c_start = np.arange(ncmp) * NSA_CMP_STRIDE
    s_start = np.arange(nslc) * NSA_SEL_BLOCK
    ovt = ((c_start[None, :] < s_start[:, None] + NSA_SEL_BLOCK)
           & (c_start[None, :] + NSA_CMP_BLOCK > s_start[:, None])).astype(np.float32)
    ovt[:, (seq - NSA_CMP_BLOCK) // NSA_CMP_STRIDE + 1:] = 0.0
    return pl.pallas_call(
        functools.partial(_nsa_select_kernel, tq=tq, n_sel=n_sel, slopes=_alibi_table(NSA_HEADS, NSA_GROUPS),
                          scale=NSA_HEAD_DIM ** -0.5),
        grid=(bsz, ng, seq // tq),
        in_specs=[pl.BlockSpec((1, 1, hg, tq, dh), lambda b, g, i: (b, g, 0, i, 0)),
                  pl.BlockSpec((1, 1, ncmp, dh), lambda b, g, i: (b, g, 0, 0)),
                  pl.BlockSpec((1, 1, dh, ncmp), lambda b, g, i: (b, g, 0, 0)),
                  pl.BlockSpec((nslc, ncmp), lambda b, g, i: (0, 0))],
        out_specs=[pl.BlockSpec((1, 1, hg, dh, tq), lambda b, g, i: (b, g, 0, 0, i)),
                   pl.BlockSpec((1, 1, nslc, tq), lambda b, g, i: (b, g, 0, i)),
                   pl.BlockSpec((1, 1, hg, tq, dh), lambda b, g, i: (b, g, 0, i, 0))],
        out_shape=[jax.ShapeDtypeStruct((bsz, ng, hg, dh, seq), F32),
                   jax.ShapeDtypeStruct((bsz, ng, nslc, seq), F32),
                   jax.ShapeDtypeStruct((bsz, ng, hg, seq, dh), BF16)],
        compiler_params=_cparams(("arbitrary", "arbitrary", "arbitrary")),
        name="nsa_select",
    )(qf, k_cmp, v_cmp_t, jnp.asarray(ovt, BF16))


def _nsa_combine_kernel(oc_ref, os_ref, ow_ref, misc_ref, o_ref):
    gates = 1.0 / (1.0 + jnp.exp(-misc_ref[...]))
    for h in range(NSA_HEADS):
        c0 = GLA_GATE_RANK + 3 * h
        o = gates[:, c0:c0 + 1] * oc_ref[0, h] + gates[:, c0 + 1:c0 + 2] * os_ref[0, h] \
            + gates[:, c0 + 2:c0 + 3] * ow_ref[0, h]
        o_ref[0, h] = o.astype(o_ref.dtype)


def _nsa_combine(o_c, o_s, o_w, misc, bsz, seq, tq=512):
    tpb = seq // tq
    hm = lambda i: (i // tpb, 0, i % tpb, 0)
    spec = pl.BlockSpec((1, NSA_HEADS, tq, NSA_HEAD_DIM), hm)
    return pl.pallas_call(
        _nsa_combine_kernel,
        grid=(bsz * tpb,),
        in_specs=[spec, spec, spec, pl.BlockSpec((tq, LANES), lambda i: (i, 0))],
        out_specs=spec,
        out_shape=jax.ShapeDtypeStruct((bsz, NSA_HEADS, seq, NSA_HEAD_DIM), BF16),
        compiler_params=_cparams(("arbitrary",)),
        name="nsa_combine",
    )(o_c, o_s, o_w, misc)


L0_SEGS = ((0, 1024), (1024, 2560), (2560, 2688), (2688, 3072), (3072, 3328))
L1_SEGS = ((0, 512), (512, 1024), (1024, 2048), (2048, 3072), (3072, 3584), (3584, 3712), (3712, 3840),
           (3840, 3968), (3968, 4096), (4096, 4224), (4224, 4352), (4352, 4480))


def _pack_w_in0(w):
    d = w.shape[0]
    a = SSD_INNER + SSD_CONV_CH
    dt = w[:, a:a + SSD_HEADS]
    qa = w[:, a + SSD_HEADS:a + SSD_HEADS + MLA_Q_RANK]
    kva = w[:, a + SSD_HEADS + MLA_Q_RANK:a + SSD_HEADS + MLA_Q_RANK + MLA_KV_RANK]
    kpe = w[:, a + SSD_HEADS + MLA_Q_RANK + MLA_KV_RANK:]
    pad = jnp.zeros((d, LANES - SSD_HEADS - MLA_ROPE), F32)
    return jnp.concatenate([w[:, :a], dt, kpe, pad, qa, kva], axis=1).astype(BF16)


def _pack_w_in1(w):
    d = w.shape[0]
    qk = 2 * GLA_HEADS * GLA_DK
    vv = GLA_HEADS * GLA_DV
    o = qk + vv
    glr = w[:, o:o + GLA_GATE_RANK]
    gg = w[:, o + GLA_GATE_RANK:o + GLA_GATE_RANK + vv]
    o2 = o + GLA_GATE_RANK + vv
    nsa = w[:, o2:o2 + NSA_HEADS * NSA_HEAD_DIM + 6 * NSA_GROUPS * NSA_HEAD_DIM]
    ngate = w[:, o2 + NSA_HEADS * NSA_HEAD_DIM + 6 * NSA_GROUPS * NSA_HEAD_DIM:]
    pad = jnp.zeros((d, LANES - GLA_GATE_RANK - 3 * NSA_HEADS), F32)
    return jnp.concatenate([w[:, :o], gg, nsa, glr, ngate, pad], axis=1).astype(BF16)


def _mixer0_parts(h_args, positions, bsz, seq, w_in, conv_w, conv_b, dt_bias, a_log, d_skip, ssm_norm_g,
                  q_a_norm_g, w_q_b, kv_a_norm_g, w_kv_b):
    z, xbc, misc, q_a, kv_a = _proj_in(*h_args, _pack_w_in0(w_in), L0_SEGS, seq)
    y_ssd = _ssd(z, xbc, misc, conv_w, conv_b, dt_bias, a_log, d_skip, ssm_norm_g, bsz, seq)
    q, k, v = _mla_prep(q_a, kv_a, misc, positions, q_a_norm_g, kv_a_norm_g, w_q_b, w_kv_b, bsz, seq)
    pair = lambda t: t.reshape(bsz, MLA_HEADS // MLA_HEADS_PER_STEP, MLA_HEADS_PER_STEP, seq, t.shape[-1])
    o = _flash(pair(q), pair(k), pair(v), None, tq=512, tk=512, out_dtype=BF16)
    o_mla = o.reshape(bsz, MLA_HEADS, seq, MLA_V).transpose(0, 2, 1, 3).reshape(bsz * seq, MLA_HEADS * MLA_V)
    return y_ssd, o_mla


def _heads_major(t, bsz, seq, lead):
    return t.reshape((bsz, seq) + lead + (NSA_HEAD_DIM,)).transpose((0,) + tuple(range(2, 2 + len(lead))) + (1, len(lead) + 2))


def _mixer1_parts(h_args, bsz, seq, w_in, w_gk2, b_gk, gla_norm_g, cmp_pos, cmp_k_w1, cmp_k_w2, cmp_v_w1, cmp_v_w2):
    gq, gk, gv, gg, nq, kc, vc, ks, vs, kw, vw, misc = _proj_in(*h_args, _pack_w_in1(w_in), L1_SEGS, seq)
    o_gla = _gla(gq, gk, gv, gg, misc, w_gk2, b_gk, gla_norm_g, bsz, seq)
    k_cmp = _compress(kc, cmp_pos, cmp_k_w1, cmp_k_w2, bsz, seq, HI)
    v_cmp = _compress(vc, cmp_pos, cmp_v_w1, cmp_v_w2, bsz, seq, None)
    qf = _heads_major(nq, bsz, seq, (NSA_GROUPS, NSA_HG))
    oc_t, selb_t, qs = _nsa_select(qf, k_cmp, v_cmp.transpose(0, 1, 3, 2), seq)
    selb = selb_t.transpose(0, 1, 3, 2).astype(BF16)
    if selb.shape[-1] < LANES:
        selb = jnp.pad(selb, ((0, 0), (0, 0), (0, 0), (0, LANES - selb.shape[-1])))
    hm = lambda t: _heads_major(t, bsz, seq, (NSA_GROUPS,)).astype(BF16)
    slopes = _alibi_table(NSA_HEADS, NSA_GROUPS)
    o_s = _flash(qs, hm(ks), hm(vs), selb, tq=256, tk=512, slopes=slopes)
    o_w = _flash(qs, hm(kw), hm(vw), None, tq=NSA_WINDOW, tk=NSA_WINDOW, slopes=slopes, window=NSA_WINDOW)
    shp = (bsz, NSA_HEADS, seq, NSA_HEAD_DIM)
    o_nsa = _nsa_combine(oc_t.transpose(0, 1, 2, 4, 3).reshape(shp), o_s.reshape(shp), o_w.reshape(shp), misc,
                         bsz, seq)
    o_nsa = o_nsa.transpose(0, 2, 1, 3).reshape(bsz * seq, NSA_HEADS * NSA_HEAD_DIM)
    return o_gla, o_nsa


def kernel(x, c, positions, l0_ada_w, l0_ada_b, l0_mix_pre_g, l0_mix_post_g, l0_w_in, l0_conv_w, l0_conv_b, l0_dt_bias, l0_a_log, l0_d_skip, l0_ssm_norm_g, l0_q_a_norm_g, l0_w_q_b, l0_kv_a_norm_g, l0_w_kv_b, l0_w_out, l0_ffn_pre_g, l0_ffn_post_g, l0_w_gate, l0_w_up, l0_w_down, l1_ada_w, l1_ada_b, l1_mix_pre_g, l1_mix_post_g, l1_w_in, l1_w_gk2, l1_b_gk, l1_gla_norm_g, l1_cmp_pos, l1_cmp_k_w1, l1_cmp_k_w2, l1_cmp_v_w1, l1_cmp_v_w2, l1_w_out, l1_ffn_pre_g, l1_ffn_post_g, l1_w_gate, l1_w_up, l1_w_down):
    bsz, seq, d = x.shape
    x2 = x.reshape(bsz * seq, d)

    def sublayers(x2, ada_w, ada_b, pre_m, post_m, mixer, w_out, pre_f, post_f, w_gate, w_up, w_down):
        shift_m, scale_m, gate_m, shift_f, scale_f, gate_f = _ada(c, ada_w, ada_b)
        a, b = mixer((x2, pre_m, scale_m, shift_m))
        ka = a.shape[1]
        x2 = _out_res(x2, a, b, w_out[:ka].astype(BF16), w_out[ka:].astype(BF16), post_m, gate_m, seq)
        return _ffn(x2, pre_f, scale_f, shift_f, w_gate.astype(BF16), w_up.astype(BF16), w_down.astype(BF16),
                    post_f, gate_f, seq)

    x2 = sublayers(
        x2, l0_ada_w, l0_ada_b, l0_mix_pre_g, l0_mix_post_g,
        lambda h: _mixer0_parts(h, positions, bsz, seq, l0_w_in, l0_conv_w, l0_conv_b, l0_dt_bias, l0_a_log,
                                l0_d_skip, l0_ssm_norm_g, l0_q_a_norm_g, l0_w_q_b, l0_kv_a_norm_g, l0_w_kv_b),
        l0_w_out, l0_ffn_pre_g, l0_ffn_post_g, l0_w_gate, l0_w_up, l0_w_down)
    x2 = sublayers(
        x2, l1_ada_w, l1_ada_b, l1_mix_pre_g, l1_mix_post_g,
        lambda h: _mixer1_parts(h, bsz, seq, l1_w_in, l1_w_gk2, l1_b_gk, l1_gla_norm_g, l1_cmp_pos,
                                l1_cmp_k_w1, l1_cmp_k_w2, l1_cmp_v_w1, l1_cmp_v_w2),
        l1_w_out, l1_ffn_pre_g, l1_ffn_post_g, l1_w_gate, l1_w_up, l1_w_down)
    return x2.reshape(bsz, seq, d)
```

```python
import functools

import numpy as np
import jax
import jax.numpy as jnp
from jax import lax
from jax.experimental import pallas as pl
from jax.experimental.pallas import tpu as pltpu

F32, BF16 = jnp.float32, jnp.bfloat16
HI = lax.Precision.HIGHEST
LANES = 128
VMEM_LIMIT = 48 * 1024 * 1024

NORM_EPS = 1e-6
NEG_INF = -1e30
LOG2E = 1.4426950408889634
N_MOD = 6

SSD_HEADS, SSD_HEAD_DIM, SSD_STATE, SSD_GROUPS, SSD_CONV = 16, 64, 128, 2, 4
SSD_INNER = SSD_HEADS * SSD_HEAD_DIM
SSD_CONV_CH = SSD_INNER + 2 * SSD_GROUPS * SSD_STATE
SSD_CHUNK = 128

MLA_HEADS, MLA_Q_RANK, MLA_KV_RANK, MLA_NOPE, MLA_ROPE, MLA_V = 8, 384, 256, 64, 32, 64
ROPE_THETA = 10000.0

GLA_HEADS, GLA_DK, GLA_DV, GLA_GATE_RANK, GLA_GATE_NORM = 4, 128, 256, 16, 16.0
GLA_CHUNK = 128
GLA_SUB = 16

NSA_HEADS, NSA_GROUPS, NSA_HEAD_DIM = 8, 2, 64
NSA_HG = NSA_HEADS // NSA_GROUPS
NSA_CMP_BLOCK, NSA_CMP_STRIDE, NSA_CMP_HIDDEN = 32, 16, 256
NSA_SEL_BLOCK, NSA_N_SEL, NSA_WINDOW, NSA_FORCE = 64, 16, 512, 1e4
_SEL_SHIFT = NSA_SEL_BLOCK.bit_length() - 1
assert 1 << _SEL_SHIFT == NSA_SEL_BLOCK


def _cparams(sem):
    return pltpu.CompilerParams(dimension_semantics=sem, vmem_limit_bytes=VMEM_LIMIT)


def _dot(a, b, prec=None):
    return jnp.dot(a, b, preferred_element_type=F32, precision=prec)


def _dot_nt(a, b, prec=None):
    return lax.dot_general(a, b, (((1,), (1,)), ((), ())), preferred_element_type=F32, precision=prec)


def _split(x, n):
    parts = []
    for _ in range(n):
        p = x.astype(BF16)
        parts.append(p)
        x = x - p.astype(F32)
    return parts


def _dot_01_left(sel, x, n):
    return sum(_dot(sel, p) for p in _split(x, n))


def _dot_01_right(x, sel, n):
    return sum(_dot(p, sel) for p in _split(x, n))


def _dot3(a, b, nt=False):
    f = _dot_nt if nt else _dot
    (ah, al), (bh, bl) = _split(a, 2), _split(b, 2)
    return f(ah, bh) + (f(ah, bl) + f(al, bh))


def _silu(x):
    return x * (1.0 / (1.0 + jnp.exp(-x)))


def _softplus(x):
    return jnp.maximum(x, 0.0) + jnp.log1p(jnp.exp(-jnp.abs(x)))


def _rms(x, g):
    return x * lax.rsqrt(jnp.mean(x * x, axis=-1, keepdims=True) + NORM_EPS) * g


def _tril(n):
    return lax.broadcasted_iota(jnp.int32, (n, n), 0) >= lax.broadcasted_iota(jnp.int32, (n, n), 1)


def _alibi_table(n, groups):
    s = 2.0 ** (-8.0 * np.arange(1, n + 1) / n)
    return [[float(v) for v in row] for row in s.reshape(groups, n // groups)]


def _by_group(g, table, h):
    val = table[0][h]
    for gi in range(1, len(table)):
        val = jnp.where(g == gi, table[gi][h], val)
    return val


def _ada_kernel(c_ref, w_ref, b_ref, o_ref):
    o_ref[...] = _dot(_silu(c_ref[...]), w_ref[...], HI) + b_ref[...]


def _ada(c, w, b):
    bsz, d = c.shape
    n = w.shape[1]
    rows = 8
    cp = jnp.zeros((rows, d), F32).at[:bsz].set(c)
    tn = 1024
    out = pl.pallas_call(
        _ada_kernel,
        grid=(n // tn,),
        in_specs=[pl.BlockSpec((rows, d), lambda j: (0, 0)),
                  pl.BlockSpec((d, tn), lambda j: (0, j)),
                  pl.BlockSpec((1, tn), lambda j: (0, j))],
        out_specs=pl.BlockSpec((rows, tn), lambda j: (0, j)),
        out_shape=jax.ShapeDtypeStruct((rows, n), F32),
        compiler_params=_cparams(("arbitrary",)),
        name="ada",
    )(cp, w, b.reshape(1, n))
    return [m.reshape(bsz, 1, d) for m in jnp.split(out[:bsz], N_MOD, axis=-1)]


def _proj_in_kernel(x_ref, g_ref, sc_ref, sh_ref, w_ref, *o_refs, segs):
    h = (_rms(x_ref[...], g_ref[...]) * (1.0 + sc_ref[0]) + sh_ref[0]).astype(BF16)
    for (a, b), o_ref in zip(segs, o_refs):
        o_ref[...] = _dot(h, w_ref[:, a:b]).astype(o_ref.dtype)


def _proj_in(x2, g, scale, shift, w, segs, seq, tm=256):
    m, d = x2.shape
    n = w.shape[1]
    tpb = seq // tm
    return pl.pallas_call(
        functools.partial(_proj_in_kernel, segs=segs),
        grid=(m // tm,),
        in_specs=[pl.BlockSpec((tm, d), lambda i: (i, 0)),
                  pl.BlockSpec((1, d), lambda i: (0, 0)),
                  pl.BlockSpec((1, 1, d), lambda i: (i // tpb, 0, 0)),
                  pl.BlockSpec((1, 1, d), lambda i: (i // tpb, 0, 0)),
                  pl.BlockSpec((d, n), lambda i: (0, 0))],
        out_specs=[pl.BlockSpec((tm, b - a), lambda i: (i, 0)) for a, b in segs],
        out_shape=[jax.ShapeDtypeStruct((m, b - a), F32) for a, b in segs],
        compiler_params=_cparams(("arbitrary",)),
        name="proj_in",
    )(x2, g.reshape(1, d), scale, shift, w)


def _out_res_kernel(x_ref, a_ref, b_ref, wa_ref, wb_ref, g_ref, gate_ref, o_ref):
    y = _dot(a_ref[...], wa_ref[...]) + _dot(b_ref[...], wb_ref[...])
    o_ref[...] = x_ref[...] + gate_ref[0] * _rms(y, g_ref[...])


def _out_res(x2, a, b, wa, wb, g, gate, seq, tm=512):
    m, d = x2.shape
    ka, kb = a.shape[1], b.shape[1]
    tpb = seq // tm
    return pl.pallas_call(
        _out_res_kernel,
        grid=(m // tm,),
        in_specs=[pl.BlockSpec((tm, d), lambda i: (i, 0)),
                  pl.BlockSpec((tm, ka), lambda i: (i, 0)),
                  pl.BlockSpec((tm, kb), lambda i: (i, 0)),
                  pl.BlockSpec((ka, d), lambda i: (0, 0)),
                  pl.BlockSpec((kb, d), lambda i: (0, 0)),
                  pl.BlockSpec((1, d), lambda i: (0, 0)),
                  pl.BlockSpec((1, 1, d), lambda i: (i // tpb, 0, 0))],
        out_specs=pl.BlockSpec((tm, d), lambda i: (i, 0)),
        out_shape=jax.ShapeDtypeStruct((m, d), F32),
        compiler_params=_cparams(("arbitrary",)),
        name="out_res",
    )(x2, a, b, wa, wb, g.reshape(1, d), gate)


def _ffn_kernel(x_ref, gpre_ref, sc_ref, sh_ref, wg_ref, wu_ref, wd_ref, gpost_ref, gate_ref, o_ref,
                h_scr, acc_scr):
    j = pl.program_id(1)

    @pl.when(j == 0)
    def _():
        h_scr[...] = (_rms(x_ref[...], gpre_ref[...]) * (1.0 + sc_ref[0]) + sh_ref[0]).astype(BF16)
        acc_scr[...] = jnp.zeros_like(acc_scr)

    h = h_scr[...]
    act = (_silu(_dot(h, wg_ref[...])) * _dot(h, wu_ref[...])).astype(BF16)
    acc_scr[...] += _dot(act, wd_ref[...])

    @pl.when(j == pl.num_programs(1) - 1)
    def _():
        o_ref[...] = x_ref[...] + gate_ref[0] * _rms(acc_scr[...], gpost_ref[...])


def _ffn(x2, gpre, scale, shift, wg, wu, wd, gpost, gate, seq, tm=512, th=1408):
    m, d = x2.shape
    hid = wg.shape[1]
    tpb = seq // tm
    return pl.pallas_call(
        _ffn_kernel,
        grid=(m // tm, hid // th),
        in_specs=[pl.BlockSpec((tm, d), lambda i, j: (i, 0)),
                  pl.BlockSpec((1, d), lambda i, j: (0, 0)),
                  pl.BlockSpec((1, 1, d), lambda i, j: (i // tpb, 0, 0)),
                  pl.BlockSpec((1, 1, d), lambda i, j: (i // tpb, 0, 0)),
                  pl.BlockSpec((d, th), lambda i, j: (0, j)),
                  pl.BlockSpec((d, th), lambda i, j: (0, j)),
                  pl.BlockSpec((th, d), lambda i, j: (j, 0)),
                  pl.BlockSpec((1, d), lambda i, j: (0, 0)),
                  pl.BlockSpec((1, 1, d), lambda i, j: (i // tpb, 0, 0))],
        out_specs=pl.BlockSpec((tm, d), lambda i, j: (i, 0)),
        out_shape=jax.ShapeDtypeStruct((m, d), F32),
        scratch_shapes=[pltpu.VMEM((tm, d), BF16), pltpu.VMEM((tm, d), F32)],
        compiler_params=_cparams(("arbitrary", "arbitrary")),
        name="ffn",
    )(x2, gpre.reshape(1, d), scale, shift, wg, wu, wd, gpost.reshape(1, d), gate)


def _ssd_kernel(z_ref, xbc_ref, misc_ref, cw_ref, cb_ref, dtb_ref, alog_ref, dsk_ref, ng_ref, e_ref, o_ref,
                ext_scr, st_scr):
    L = SSD_CHUNK
    gsz = SSD_INNER // SSD_GROUPS
    hpg = SSD_HEADS // SSD_GROUPS
    pad = 8

    @pl.when(pl.program_id(1) == 0)
    def _():
        ext_scr[0:pad, :] = jnp.zeros((pad, SSD_CONV_CH), F32)
        st_scr[...] = jnp.zeros_like(st_scr)

    xt = xbc_ref[...]
    ext_scr[pad:pad + L, :] = xt
    acc = cb_ref[...] + cw_ref[0:1, :] * ext_scr[pad - 3:pad - 3 + L, :]
    for k in range(1, SSD_CONV):
        acc = acc + cw_ref[k:k + 1, :] * ext_scr[pad - 3 + k:pad - 3 + k + L, :]
    ext_scr[0:pad, :] = xt[L - pad:L, :]
    xbc = _silu(acc)
    xs = xbc[:, :SSD_INNER]

    e = e_ref[...]
    dt = _softplus(misc_ref[...] + dtb_ref[...])
    adt = dt * (-jnp.exp(alog_ref[...]))
    tril = _tril(L)
    a_cs = _dot_01_left(jnp.where(tril, 1.0, 0.0).astype(BF16), adt, 3)
    a_cs_t = a_cs.T
    ea = jnp.exp(a_cs)
    ea_e = _dot_01_right(ea, e, 2)
    dec_e = _dot_01_right(jnp.exp(a_cs[L - 1:L, :] - a_cs), e, 2)
    xd = xs * _dot_01_right(dt, e, 2)

    ys = []
    for g in range(SSD_GROUPS):
        bg = xbc[:, SSD_INNER + g * SSD_STATE:SSD_INNER + (g + 1) * SSD_STATE]
        cg = xbc[:, SSD_INNER + (SSD_GROUPS + g) * SSD_STATE:SSD_INNER + (SSD_GROUPS + g + 1) * SSD_STATE]
        bg16, cg16 = bg.astype(BF16), cg.astype(BF16)
        gmat = _dot_nt(cg16, bg16)
        cols = slice(g * gsz, (g + 1) * gsz)
        xdg = xd[:, cols]
        st = st_scr[g]
        y_off = _dot(cg16, st.astype(BF16)) * ea_e[:, cols]
        st_scr[g] = st * ea_e[L - 1:L, cols] + _dot(bg.T.astype(BF16), (xdg * dec_e[:, cols]).astype(BF16))
        yd = []
        for h in range(hpg):
            hh = g * hpg + h
            seg = a_cs[:, hh:hh + 1] - a_cs_t[hh:hh + 1, :]
            lmat = jnp.where(tril, jnp.exp(seg), 0.0)
            yd.append(_dot((gmat * lmat).astype(BF16),
                           xdg[:, h * SSD_HEAD_DIM:(h + 1) * SSD_HEAD_DIM].astype(BF16)))
        y = jnp.concatenate(yd, axis=-1) + y_off + dsk_ref[:, cols] * xs[:, cols]
        y = y * _silu(z_ref[:, cols])
        ys.append(_rms(y, ng_ref[:, cols]))
    o_ref[...] = jnp.concatenate(ys, axis=-1).astype(o_ref.dtype)


def _ssd(z, xbc, misc, conv_w, conv_b, dt_bias, a_log, d_skip, norm_g, bsz, seq):
    L = SSD_CHUNK
    nc = seq // L
    m = bsz * seq
    e = np.zeros((LANES, SSD_INNER), np.float32)
    for h in range(SSD_HEADS):
        e[h, h * SSD_HEAD_DIM:(h + 1) * SSD_HEAD_DIM] = 1.0
    pad128 = lambda v: jnp.zeros((1, LANES), F32).at[0, :v.shape[0]].set(v)
    row = lambda i, c: (i * nc + c, 0)
    const = lambda i, c: (0, 0)
    return pl.pallas_call(
        _ssd_kernel,
        grid=(bsz, nc),
        in_specs=[pl.BlockSpec((L, SSD_INNER), row),
                  pl.BlockSpec((L, SSD_CONV_CH), row),
                  pl.BlockSpec((L, LANES), row),
                  pl.BlockSpec((SSD_CONV, SSD_CONV_CH), const),
                  pl.BlockSpec((1, SSD_CONV_CH), const),
                  pl.BlockSpec((1, LANES), const),
                  pl.BlockSpec((1, LANES), const),
                  pl.BlockSpec((1, SSD_INNER), const),
                  pl.BlockSpec((1, SSD_INNER), const),
                  pl.BlockSpec((LANES, SSD_INNER), const)],
        out_specs=pl.BlockSpec((L, SSD_INNER), row),
        out_shape=jax.ShapeDtypeStruct((m, SSD_INNER), BF16),
        scratch_shapes=[pltpu.VMEM((L + 8, SSD_CONV_CH), F32),
                        pltpu.VMEM((SSD_GROUPS, SSD_STATE, SSD_INNER // SSD_GROUPS), F32)],
        compiler_params=_cparams(("arbitrary", "arbitrary")),
        name="ssd",
    )(z, xbc, misc, conv_w, conv_b.reshape(1, -1), pad128(dt_bias), pad128(a_log),
      jnp.repeat(d_skip, SSD_HEAD_DIM).reshape(1, -1), norm_g.reshape(1, -1), jnp.asarray(e, BF16))


MLA_D = LANES
MLA_HEADS_PER_STEP = 8
_R1 = MLA_NOPE
_R2 = MLA_NOPE + MLA_ROPE // 2


def _mla_prep_kernel(qa_ref, kva_ref, misc_ref, pos_ref, gq_ref, gkv_ref, wq_ref, wqs_ref, wk_ref, wv_ref,
                     pk_ref, pks_ref, invf_ref, sgn_ref, q_ref, k_ref, v_ref):
    ang = pos_ref[...].astype(F32) * invf_ref[...]
    cos = jnp.cos(ang)
    sin = jnp.sin(ang) * sgn_ref[...]
    nq = _rms(qa_ref[...], gq_ref[...]).astype(BF16)
    nkv = _rms(kva_ref[...], gkv_ref[...]).astype(BF16)
    misc = misc_ref[...]
    k_rot = _dot_01_right(misc, pk_ref[...], 3) * cos + _dot_01_right(misc, pks_ref[...], 3) * sin
    for h in range(MLA_HEADS):
        qh = _dot(nq, wq_ref[h]) * cos + _dot(nq, wqs_ref[h]) * sin
        q_ref[0, h] = (qh * ((MLA_NOPE + MLA_ROPE) ** -0.5 * LOG2E)).astype(q_ref.dtype)
        k_ref[0, h] = (_dot(nkv, wk_ref[h]) + k_rot).astype(k_ref.dtype)
        v_ref[0, h] = _dot(nkv, wv_ref[h]).astype(v_ref.dtype)


def _mla_prep(q_a, kv_a, misc, positions, gq, gkv, w_q_b, w_kv_b, bsz, seq, tm=512):
    m = bsz * seq
    half = MLA_ROPE // 2
    dq = MLA_NOPE + MLA_ROPE
    wq3 = w_q_b.reshape(MLA_Q_RANK, MLA_HEADS, dq).transpose(1, 0, 2)
    zq = jnp.zeros((MLA_HEADS, MLA_Q_RANK, MLA_D - dq), F32)
    wq = jnp.concatenate([wq3, zq], axis=-1).astype(BF16)
    wqs = jnp.concatenate([jnp.zeros((MLA_HEADS, MLA_Q_RANK, MLA_NOPE), F32), wq3[..., _R2:dq], wq3[..., _R1:_R2], zq],
                          axis=-1).astype(BF16)
    wkv3 = w_kv_b.reshape(MLA_KV_RANK, MLA_HEADS, MLA_NOPE + MLA_V).transpose(1, 0, 2)
    wk = jnp.concatenate([wkv3[..., :MLA_NOPE], jnp.zeros((MLA_HEADS, MLA_KV_RANK, MLA_D - MLA_NOPE), F32)],
                         axis=-1).astype(BF16)
    wv = wkv3[..., MLA_NOPE:].astype(BF16)
    pk = np.zeros((LANES, MLA_D), np.float32)
    pks = np.zeros((LANES, MLA_D), np.float32)
    invf = np.zeros((1, MLA_D), np.float32)
    sgn = np.zeros((1, MLA_D), np.float32)
    inv = (ROPE_THETA ** (-np.arange(0, MLA_ROPE, 2) / MLA_ROPE)).astype(np.float32)
    for i in range(half):
        pk[SSD_HEADS + i, _R1 + i] = 1.0
        pk[SSD_HEADS + half + i, _R2 + i] = 1.0
        pks[SSD_HEADS + half + i, _R1 + i] = 1.0
        pks[SSD_HEADS + i, _R2 + i] = 1.0
        invf[0, _R1 + i] = invf[0, _R2 + i] = inv[i]
        sgn[0, _R1 + i], sgn[0, _R2 + i] = -1.0, 1.0
    tpb = seq // tm
    c2 = lambda i: (0, 0)
    c3 = lambda i: (0, 0, 0)
    hm = lambda i: (i // tpb, 0, i % tpb, 0)
    return pl.pallas_call(
        _mla_prep_kernel,
        grid=(m // tm,),
        in_specs=[pl.BlockSpec((tm, MLA_Q_RANK), lambda i: (i, 0)),
                  pl.BlockSpec((tm, MLA_KV_RANK), lambda i: (i, 0)),
                  pl.BlockSpec((tm, LANES), lambda i: (i, 0)),
                  pl.BlockSpec((tm, 1), lambda i: (i, 0)),
                  pl.BlockSpec((1, MLA_Q_RANK), c2),
                  pl.BlockSpec((1, MLA_KV_RANK), c2),
                  pl.BlockSpec((MLA_HEADS, MLA_Q_RANK, MLA_D), c3),
                  pl.BlockSpec((MLA_HEADS, MLA_Q_RANK, MLA_D), c3),
                  pl.BlockSpec((MLA_HEADS, MLA_KV_RANK, MLA_D), c3),
                  pl.BlockSpec((MLA_HEADS, MLA_KV_RANK, MLA_V), c3),
                  pl.BlockSpec((LANES, MLA_D), c2),
                  pl.BlockSpec((LANES, MLA_D), c2),
                  pl.BlockSpec((1, MLA_D), c2),
                  pl.BlockSpec((1, MLA_D), c2)],
        out_specs=[pl.BlockSpec((1, MLA_HEADS, tm, MLA_D), hm),
                   pl.BlockSpec((1, MLA_HEADS, tm, MLA_D), hm),
                   pl.BlockSpec((1, MLA_HEADS, tm, MLA_V), hm)],
        out_shape=[jax.ShapeDtypeStruct((bsz, MLA_HEADS, seq, MLA_D), BF16),
                   jax.ShapeDtypeStruct((bsz, MLA_HEADS, seq, MLA_D), BF16),
                   jax.ShapeDtypeStruct((bsz, MLA_HEADS, seq, MLA_V), BF16)],
        compiler_params=_cparams(("arbitrary",)),
        name="mla_prep",
    )(q_a, kv_a, misc, positions.reshape(m, 1), gq.reshape(1, -1), gkv.reshape(1, -1), wq, wqs, wk, wv,
      jnp.asarray(pk, BF16), jnp.asarray(pks, BF16), jnp.asarray(invf), jnp.asarray(sgn))


_FIRST, _LAST, _MASKED = 1, 2, 4
SOFTMAX_ROWS = 64


def _flash_schedule(seq, tq, tk, window):
    qi_l, kb_l, fl_l = [], [], []
    for qi in range(seq // tq):
        q0, q1 = qi * tq, qi * tq + tq - 1
        first = q0 // tk
        lo = 0 if window is None else max(0, q0 - window + 1) // tk
        blocks = [first] + [b for b in range(lo, q1 // tk + 1) if b != first]
        for n, kb in enumerate(blocks):
            k0, k1 = kb * tk, kb * tk + tk - 1
            masked = k1 > q0 or (window is not None and q1 - k0 >= window)
            qi_l.append(qi)
            kb_l.append(kb)
            fl_l.append((_FIRST if n == 0 else 0) | (_LAST if n == len(blocks) - 1 else 0) | (_MASKED if masked else 0))
    return [jnp.asarray(np.asarray(t, np.int32)) for t in (qi_l, kb_l, fl_l)]


def _flash_kernel(qi_ref, kb_ref, fl_ref, *refs, hg, tq, tk, dv, slopes, window, has_sel, kv_per_head):
    if has_sel:
        q_ref, k_ref, v_ref, selb_ref, o_ref, m_scr, l_scr, alpha_scr, acc_scr, s_scr, p_scr, bias_scr = refs
    else:
        q_ref, k_ref, v_ref, o_ref, m_scr, l_scr, alpha_scr, acc_scr, s_scr, p_scr, bias_scr = refs
    g = pl.program_id(1)
    st = pl.program_id(2)
    qi, kb, fl = qi_ref[st], kb_ref[st], fl_ref[st]
    masked = (fl & _MASKED) != 0
    rows = hg * tq
    rc = SOFTMAX_ROWS

    @pl.when((fl & _FIRST) != 0)
    def _():
        m_scr[...] = jnp.full_like(m_scr, NEG_INF)
        l_scr[...] = jnp.zeros_like(l_scr)
        acc_scr[...] = jnp.zeros_like(acc_scr)

    def mask_bias():
        dist = (qi * tq + lax.broadcasted_iota(jnp.int32, (tq, tk), 0)) - \
               (kb * tk + lax.broadcasted_iota(jnp.int32, (tq, tk), 1))
        ok = dist >= 0
        if window is not None:
            ok = ok & (dist < window)
        return jnp.where(ok, 0.0, NEG_INF)

    def step(with_mask):
        with_bias = has_sel or with_mask
        if has_sel:
            blk = lax.broadcasted_iota(jnp.int32, (LANES, tk), 0)
            key = kb * tk + lax.broadcasted_iota(jnp.int32, (LANES, tk), 1)
            expand = jnp.where((key >> _SEL_SHIFT) == blk, 1.0, 0.0).astype(BF16)
            bias = _dot(selb_ref[0, 0], expand)
            bias_scr[...] = bias + mask_bias() if with_mask else bias
        elif with_mask:
            bias_scr[...] = mask_bias()
        kv = (lambda ref, h: ref[0, 0, h]) if kv_per_head else (lambda ref, h: ref[0, 0])
        for h in range(hg):
            s_scr[h * tq:(h + 1) * tq, :] = _dot_nt(q_ref[0, 0, h], kv(k_ref, h))
        for h in range(hg):
            if slopes is not None:
                rel = (kb * tk - qi * tq + lax.broadcasted_iota(jnp.int32, (1, tk), 1)).astype(F32)
                arow = (_by_group(g, slopes, h) * LOG2E) * rel
            for c in range(tq // rc):
                rs = slice(h * tq + c * rc, h * tq + (c + 1) * rc)
                s = s_scr[rs, :]
                if with_bias:
                    s = s + bias_scr[c * rc:(c + 1) * rc, :]
                if slopes is not None:
                    s = s + arow
                if with_bias or slopes is not None:
                    s_scr[rs, :] = s
                m_prev = m_scr[rs, :]
                m_next = jnp.maximum(m_prev, jnp.max(s, axis=1, keepdims=True))
                alpha_scr[rs, :] = jnp.exp2(m_prev - m_next)
                m_scr[rs, :] = m_next
        for h in range(hg):
            for c in range(tq // rc):
                rs = slice(h * tq + c * rc, h * tq + (c + 1) * rc)
                p = jnp.exp2(s_scr[rs, :] - jnp.tile(m_scr[rs, :], (1, tk // LANES)))
                alpha = alpha_scr[rs, :]
                l_scr[rs, :] = alpha * l_scr[rs, :] + jnp.sum(p, axis=1, keepdims=True)
                acc_scr[rs, :] = acc_scr[rs, :] * alpha[:, :dv]
                p_scr[rs, :] = p.astype(BF16)
        for h in range(hg):
            hs = slice(h * tq, (h + 1) * tq)
            acc_scr[hs, :] += _dot(p_scr[hs, :], kv(v_ref, h))

    pl.when(masked)(lambda: step(True))
    pl.when(jnp.logical_not(masked))(lambda: step(False))

    @pl.when((fl & _LAST) != 0)
    def _():
        out = acc_scr[...] / jnp.maximum(l_scr[:, :dv], 1e-30)
        o_ref[0, 0] = out.reshape(hg, tq, dv).astype(o_ref.dtype)


def _flash(q, k, v, selb, *, tq, tk, slopes=None, window=None, out_dtype=F32):
    bsz, ng, hg, seq, d = q.shape
    dv = v.shape[-1]
    kv_per_head = k.ndim == 5
    sched = _flash_schedule(seq, tq, tk, window)
    rows = hg * tq
    if kv_per_head:
        kv_spec = lambda w: pl.BlockSpec((1, 1, hg, tk, w), lambda b, g, s, qi, kb, fl: (b, g, 0, kb[s], 0))
    else:
        kv_spec = lambda w: pl.BlockSpec((1, 1, tk, w), lambda b, g, s, qi, kb, fl: (b, g, kb[s], 0))
    in_specs = [pl.BlockSpec((1, 1, hg, tq, d), lambda b, g, s, qi, kb, fl: (b, g, 0, qi[s], 0)),
                kv_spec(d), kv_spec(dv)]
    args = [q, k, v]
    if selb is not None:
        in_specs.append(pl.BlockSpec((1, 1, tq, LANES), lambda b, g, s, qi, kb, fl: (b, g, qi[s], 0)))
        args.append(selb)
    return pl.pallas_call(
        functools.partial(_flash_kernel, hg=hg, tq=tq, tk=tk, dv=dv, slopes=slopes, window=window,
                          has_sel=selb is not None, kv_per_head=kv_per_head),
        grid_spec=pltpu.PrefetchScalarGridSpec(
            num_scalar_prefetch=3,
            grid=(bsz, ng, int(sched[0].shape[0])),
            in_specs=in_specs,
            out_specs=pl.BlockSpec((1, 1, hg, tq, dv), lambda b, g, s, qi, kb, fl: (b, g, 0, qi[s], 0)),
            scratch_shapes=[pltpu.VMEM((rows, LANES), F32), pltpu.VMEM((rows, LANES), F32),
                            pltpu.VMEM((rows, LANES), F32),
                            pltpu.VMEM((rows, dv), F32), pltpu.VMEM((rows, tk), F32),
                            pltpu.VMEM((rows, tk), BF16), pltpu.VMEM((tq, tk), F32)]),
        out_shape=jax.ShapeDtypeStruct((bsz, ng, hg, seq, dv), out_dtype),
        compiler_params=_cparams(("arbitrary", "arbitrary", "arbitrary")),
        name="flash",
    )(*sched, *args)


def _gla_kernel(q_ref, k_ref, v_ref, gg_ref, misc_ref, w2_ref, bgk_ref, ng_ref, o_ref, st_scr):
    L = GLA_CHUNK

    @pl.when(pl.program_id(1) == 0)
    def _():
        st_scr[...] = jnp.zeros_like(st_scr)

    zg = _dot3(misc_ref[...], w2_ref[...]) + bgk_ref[...]
    log_a = -_softplus(-zg) * (1.0 / GLA_GATE_NORM)
    tril = _tril(L)
    bc_all = _dot_01_left(jnp.where(tril, 1.0, 0.0).astype(BF16), log_a, 3)
    row = lax.broadcasted_iota(jnp.int32, (L, GLA_DK), 0)
    for h in range(GLA_HEADS):
        kc, vc = slice(h * GLA_DK, (h + 1) * GLA_DK), slice(h * GLA_DV, (h + 1) * GLA_DV)
        bc = bc_all[:, kc]
        q = q_ref[:, kc] * (GLA_DK ** -0.5)
        k = k_ref[:, kc]
        v = v_ref[:, vc]
        st = st_scr[h]
        o = _dot_nt((q * jnp.exp(bc)).astype(BF16), st.astype(BF16))
        att = []
        for i in range(L // GLA_SUB):
            r0 = i * GLA_SUB
            ref = bc[r0:r0 + 1, :]
            qi = q[r0:r0 + GLA_SUB] * jnp.exp(bc[r0:r0 + GLA_SUB] - ref)
            ki = k * jnp.exp(jnp.where(row < r0 + GLA_SUB, ref - bc, 0.0))
            att.append(_dot_nt(qi.astype(BF16), ki.astype(BF16)))
        att = jnp.where(tril, jnp.concatenate(att, axis=0), 0.0)
        o = o + _dot(att.astype(BF16), v.astype(BF16))
        b_last = bc[L - 1:L, :]
        kd = (k * jnp.exp(b_last - bc)).astype(BF16)
        st_scr[h] = st * jnp.exp(b_last) + _dot(v.T.astype(BF16), kd)
        o_ref[:, vc] = (_rms(o, ng_ref[...]) * _silu(gg_ref[:, vc])).astype(o_ref.dtype)


def _gla(gq, gk, gv, gg, misc, w_gk2, b_gk, norm_g, bsz, seq):
    L = GLA_CHUNK
    nc = seq // L
    m = bsz * seq
    nk, nv = GLA_HEADS * GLA_DK, GLA_HEADS * GLA_DV
    w2 = jnp.zeros((LANES, nk), F32).at[:GLA_GATE_RANK].set(w_gk2)
    row = lambda b, c: (b * nc + c, 0)
    const = lambda b, c: (0, 0)
    return pl.pallas_call(
        _gla_kernel,
        grid=(bsz, nc),
        in_specs=[pl.BlockSpec((L, nk), row),
                  pl.BlockSpec((L, nk), row),
                  pl.BlockSpec((L, nv), row),
                  pl.BlockSpec((L, nv), row),
                  pl.BlockSpec((L, LANES), row),
                  pl.BlockSpec((LANES, nk), const),
                  pl.BlockSpec((1, nk), const),
                  pl.BlockSpec((1, GLA_DV), const)],
        out_specs=pl.BlockSpec((L, nv), row),
        out_shape=jax.ShapeDtypeStruct((m, nv), BF16),
        scratch_shapes=[pltpu.VMEM((GLA_HEADS, GLA_DV, GLA_DK), F32)],
        compiler_params=_cparams(("arbitrary", "arbitrary")),
        name="gla",
    )(gq, gk, gv, gg, misc, w2, b_gk.reshape(1, -1), norm_g.reshape(1, -1))


def _cmp_kernel(z_ref, pa_ref, pb_ref, wa_ref, wb_ref, w2_ref, o_ref, *, prec):
    z = z_ref[0]
    if prec is None:
        cast = lambda t: t.astype(BF16)
    else:
        cast = lambda t: t
    first = _dot(cast(z + pa_ref[...]), cast(wa_ref[...]), prec)
    second = _dot(cast(z + pb_ref[...]), cast(wb_ref[...]), prec)
    n = first.shape[0]
    hid = _silu(first + pltpu.roll(second, n - 1, 0))
    for g in range(NSA_GROUPS):
        o_ref[0, g] = _dot(cast(hid[:, g * NSA_CMP_HIDDEN:(g + 1) * NSA_CMP_HIDDEN]), cast(w2_ref[...]), prec)


def _compress(z, cmp_pos, w1, w2, bsz, seq, prec):
    nb = seq // NSA_CMP_STRIDE
    width = NSA_CMP_STRIDE * NSA_GROUPS * NSA_HEAD_DIM
    zr = z.reshape(bsz, nb, width)
    per = NSA_CMP_BLOCK // NSA_CMP_STRIDE
    eye = jnp.eye(NSA_GROUPS, dtype=F32)
    wbig = jnp.einsum("ldj,gh->lgdhj", w1.reshape(NSA_CMP_BLOCK, NSA_HEAD_DIM, NSA_CMP_HIDDEN), eye)
    wbig = wbig.reshape(per, width, NSA_GROUPS * NSA_CMP_HIDDEN)
    posb = jnp.broadcast_to(cmp_pos[:, None, :], (NSA_CMP_BLOCK, NSA_GROUPS, NSA_HEAD_DIM)).reshape(per, 1, width)
    assert per == 2
    c2 = lambda b: (0, 0)
    return pl.pallas_call(
        functools.partial(_cmp_kernel, prec=prec),
        grid=(bsz,),
        in_specs=[pl.BlockSpec((1, nb, width), lambda b: (b, 0, 0)),
                  pl.BlockSpec((1, width), c2), pl.BlockSpec((1, width), c2),
                  pl.BlockSpec((width, NSA_GROUPS * NSA_CMP_HIDDEN), c2),
                  pl.BlockSpec((width, NSA_GROUPS * NSA_CMP_HIDDEN), c2),
                  pl.BlockSpec((NSA_CMP_HIDDEN, NSA_HEAD_DIM), c2)],
        out_specs=pl.BlockSpec((1, NSA_GROUPS, nb, NSA_HEAD_DIM), lambda b: (b, 0, 0, 0)),
        out_shape=jax.ShapeDtypeStruct((bsz, NSA_GROUPS, nb, NSA_HEAD_DIM), F32),
        compiler_params=_cparams(("arbitrary",)),
        name="nsa_compress",
    )(zr, posb[0], posb[1], wbig[0], wbig[1], w2)


def _nsa_select_kernel(q_ref, kc_ref, vct_ref, ovt_ref, oct_ref, selt_ref, qs_ref, *, tq, n_sel, slopes, scale):
    g = pl.program_id(1)
    qi = pl.program_id(2)
    ncmp = kc_ref.shape[2]
    nslc = ovt_ref.shape[0]
    qpos_c = qi * tq + lax.broadcasted_iota(jnp.int32, (ncmp, tq), 1)
    cmp_end = lax.broadcasted_iota(jnp.int32, (ncmp, tq), 0) * NSA_CMP_STRIDE + (NSA_CMP_BLOCK - 1)
    dist = (qpos_c - cmp_end).astype(F32)
    mask = dist >= 0.0
    kc_hi, kc_lo = _split(kc_ref[0, 0], 2)
    vct = vct_ref[0, 0].astype(BF16)
    psum = jnp.zeros((ncmp, tq), F32)
    for h in range(NSA_HG):
        q = q_ref[0, 0, h]
        qs_ref[0, 0, h] = (q * (scale * LOG2E)).astype(qs_ref.dtype)
        q_hi, q_lo = _split(q, 2)
        qk = _dot_nt(kc_hi, q_hi) + (_dot_nt(kc_hi, q_lo) + _dot_nt(kc_lo, q_hi))
        s = qk * scale - _by_group(g, slopes, h) * dist
        s = jnp.where(mask, s, NEG_INF)
        mx = jnp.max(s, axis=0, keepdims=True)
        ex = jnp.where(mask, jnp.exp(s - mx), 0.0)
        p = ex / jnp.maximum(jnp.sum(ex, axis=0, keepdims=True), 1e-30)
        oct_ref[0, 0, h] = _dot(vct, p.astype(BF16))
        psum = psum + p
    imp = _dot_01_left(ovt_ref[...], psum, 2)
    blk = lax.broadcasted_iota(jnp.int32, (nslc, tq), 0)
    qpos = qi * tq + lax.broadcasted_iota(jnp.int32, (nslc, tq), 1)
    forced = (blk == 0) | (blk == (qpos >> _SEL_SHIFT))
    avail = blk * NSA_SEL_BLOCK <= qpos
    imp = jnp.where(forced, NSA_FORCE, jnp.where(avail, imp, -1.0))
    blkf = blk.astype(F32)
    selb = jnp.full((nslc, tq), NEG_INF, F32)
    for _ in range(n_sel):
        mx = jnp.max(imp, axis=0, keepdims=True)
        first = jnp.min(jnp.where(imp == mx, blkf, float(nslc)), axis=0, keepdims=True)
        one = blkf == first
        selb = jnp.where(one, 0.0, selb)
        imp = jnp.where(one, -2.0, imp)
    selt_ref[0, 0] = selb


def _nsa_select(qf, k_cmp, v_cmp_t, seq, tq=256):
    bsz, ng, hg, _, dh = qf.shape
    ncmp = k_cmp.shape[2]
    nslc = seq // NSA_SEL_BLOCK
    n_sel = min(NSA_N_SEL, nslc)
    c_start = np.arange(ncmp) * NSA_CMP_STRIDE
    s_start = np.arange(nslc) * NSA_SEL_BLOCK
    ovt = ((c_start[None, :] < s_start[:, None] + NSA_SEL_BLOCK)
           & (c_start[None, :] + NSA_CMP_BLOCK > s_start[:, None])).astype(np.float32)
    ovt[:, (seq - NSA_CMP_BLOCK) // NSA_CMP_STRIDE + 1:] = 0.0
    return pl.pallas_call(
        functools.partial(_nsa_select_kernel, tq=tq, n_sel=n_sel, slopes=_alibi_table(NSA_HEADS, NSA_GROUPS),
                          scale=NSA_HEAD_DIM ** -0.5),
        grid=(bsz, ng, seq // tq),
        in_specs=[pl.BlockSpec((1, 1, hg, tq, dh), lambda b, g, i: (b, g, 0, i, 0)),
                  pl.BlockSpec((1, 1, ncmp, dh), lambda b, g, i: (b, g, 0, 0)),
                  pl.BlockSpec((1, 1, dh, ncmp), lambda b, g, i: (b, g, 0, 0)),
                  pl.BlockSpec((nslc, ncmp), lambda b, g, i: (0, 0))],
        out_specs=[pl.BlockSpec((1, 1, hg, dh, tq), lambda b, g, i: (b, g, 0, 0, i)),
                   pl.BlockSpec((1, 1, nslc, tq), lambda b, g, i: (b, g, 0, i)),
                   pl.BlockSpec((1, 1, hg, tq, dh), lambda b, g, i: (b, g, 0, i, 0))],
        out_shape=[jax.ShapeDtypeStruct((bsz, ng, hg, dh, seq), F32),
                   jax.ShapeDtypeStruct((bsz, ng, nslc, seq), F32),
                   jax.ShapeDtypeStruct((bsz, ng, hg, seq, dh), BF16)],
        compiler_params=_cparams(("arbitrary", "arbitrary", "arbitrary")),
        name="nsa_select",
    )(qf, k_cmp, v_cmp_t, jnp.asarray(ovt, BF16))


def _nsa_combine_kernel(oc_ref, os_ref, ow_ref, misc_ref, o_ref):
    gates = 1.0 / (1.0 + jnp.exp(-misc_ref[...]))
    for h in range(NSA_HEADS):
        c0 = GLA_GATE_RANK + 3 * h
        o = gates[:, c0:c0 + 1] * oc_ref[0, h] + gates[:, c0 + 1:c0 + 2] * os_ref[0, h] \
            + gates[:, c0 + 2:c0 + 3] * ow_ref[0, h]
        o_ref[0, h] = o.astype(o_ref.dtype)


def _nsa_combine(o_c, o_s, o_w, misc, bsz, seq, tq=512):
    tpb = seq // tq
    hm = lambda i: (i // tpb, 0, i % tpb, 0)
    spec = pl.BlockSpec((1, NSA_HEADS, tq, NSA_HEAD_DIM), hm)
    return pl.pallas_call(
        _nsa_combine_kernel,
        grid=(bsz * tpb,),
        in_specs=[spec, spec, spec, pl.BlockSpec((tq, LANES), lambda i: (i, 0))],
        out_specs=spec,
        out_shape=jax.ShapeDtypeStruct((bsz, NSA_HEADS, seq, NSA_HEAD_DIM), BF16),
        compiler_params=_cparams(("arbitrary",)),
        name="nsa_combine",
    )(o_c, o_s, o_w, misc)


L0_SEGS = ((0, 1024), (1024, 2560), (2560, 2688), (2688, 3072), (3072, 3328))
L1_SEGS = ((0, 512), (512, 1024), (1024, 2048), (2048, 3072), (3072, 3584), (3584, 3712), (3712, 3840),
           (3840, 3968), (3968, 4096), (4096, 4224), (4224, 4352), (4352, 4480))


def _pack_w_in0(w):
    d = w.shape[0]
    a = SSD_INNER + SSD_CONV_CH
    dt = w[:, a:a + SSD_HEADS]
    qa = w[:, a + SSD_HEADS:a + SSD_HEADS + MLA_Q_RANK]
    kva = w[:, a + SSD_HEADS + MLA_Q_RANK:a + SSD_HEADS + MLA_Q_RANK + MLA_KV_RANK]
    kpe = w[:, a + SSD_HEADS + MLA_Q_RANK + MLA_KV_RANK:]
    pad = jnp.zeros((d, LANES - SSD_HEADS - MLA_ROPE), F32)
    return jnp.concatenate([w[:, :a], dt, kpe, pad, qa, kva], axis=1).astype(BF16)


def _pack_w_in1(w):
    d = w.shape[0]
    qk = 2 * GLA_HEADS * GLA_DK
    vv = GLA_HEADS * GLA_DV
    o = qk + vv
    glr = w[:, o:o + GLA_GATE_RANK]
    gg = w[:, o + GLA_GATE_RANK:o + GLA_GATE_RANK + vv]
    o2 = o + GLA_GATE_RANK + vv
    nsa = w[:, o2:o2 + NSA_HEADS * NSA_HEAD_DIM + 6 * NSA_GROUPS * NSA_HEAD_DIM]
    ngate = w[:, o2 + NSA_HEADS * NSA_HEAD_DIM + 6 * NSA_GROUPS * NSA_HEAD_DIM:]
    pad = jnp.zeros((d, LANES - GLA_GATE_RANK - 3 * NSA_HEADS), F32)
    return jnp.concatenate([w[:, :o], gg, nsa, glr, ngate, pad], axis=1).astype(BF16)


def _mixer0_parts(h_args, positions, bsz, seq, w_in, conv_w, conv_b, dt_bias, a_log, d_skip, ssm_norm_g,
                  q_a_norm_g, w_q_b, kv_a_norm_g, w_kv_b):
    z, xbc, misc, q_a, kv_a = _proj_in(*h_args, _pack_w_in0(w_in), L0_SEGS, seq)
    y_ssd = _ssd(z, xbc, misc, conv_w, conv_b, dt_bias, a_log, d_skip, ssm_norm_g, bsz, seq)
    q, k, v = _mla_prep(q_a, kv_a, misc, positions, q_a_norm_g, kv_a_norm_g, w_q_b, w_kv_b, bsz, seq)
    pair = lambda t: t.reshape(bsz, MLA_HEADS // MLA_HEADS_PER_STEP, MLA_HEADS_PER_STEP, seq, t.shape[-1])
    o = _flash(pair(q), pair(k), pair(v), None, tq=512, tk=512, out_dtype=BF16)
    o_mla = o.reshape(bsz, MLA_HEADS, seq, MLA_V).transpose(0, 2, 1, 3).reshape(bsz * seq, MLA_HEADS * MLA_V)
    return y_ssd, o_mla


def _heads_major(t, bsz, seq, lead):
    return t.reshape((bsz, seq) + lead + (NSA_HEAD_DIM,)).transpose((0,) + tuple(range(2, 2 + len(lead))) + (1, len(lead) + 2))


def _mixer1_parts(h_args, bsz, seq, w_in, w_gk2, b_gk, gla_norm_g, cmp_pos, cmp_k_w1, cmp_k_w2, cmp_v_w1, cmp_v_w2):
    gq, gk, gv, gg, nq, kc, vc, ks, vs, kw, vw, misc = _proj_in(*h_args, _pack_w_in1(w_in), L1_SEGS, seq)
    o_gla = _gla(gq, gk, gv, gg, misc, w_gk2, b_gk, gla_norm_g, bsz, seq)
    k_cmp = _compress(kc, cmp_pos, cmp_k_w1, cmp_k_w2, bsz, seq, HI)
    v_cmp = _compress(vc, cmp_pos, cmp_v_w1, cmp_v_w2, bsz, seq, None)
    qf = _heads_major(nq, bsz, seq, (NSA_GROUPS, NSA_HG))
    oc_t, selb_t, qs = _nsa_select(qf, k_cmp, v_cmp.transpose(0, 1, 3, 2), seq)
    selb = selb_t.transpose(0, 1, 3, 2).astype(BF16)
    if selb.shape[-1] < LANES:
        selb = jnp.pad(selb, ((0, 0), (0, 0), (0, 0), (0, LANES - selb.shape[-1])))
    hm = lambda t: _heads_major(t, bsz, seq, (NSA_GROUPS,)).astype(BF16)
    slopes = _alibi_table(NSA_HEADS, NSA_GROUPS)
    o_s = _flash(qs, hm(ks), hm(vs), selb, tq=512, tk=512, slopes=slopes)
    o_w = _flash(qs, hm(kw), hm(vw), None, tq=NSA_WINDOW, tk=NSA_WINDOW, slopes=slopes, window=NSA_WINDOW)
    shp = (bsz, NSA_HEADS, seq, NSA_HEAD_DIM)
    o_nsa = _nsa_combine(oc_t.transpose(0, 1, 2, 4, 3).reshape(shp), o_s.reshape(shp), o_w.reshape(shp), misc,
                         bsz, seq)
    o_nsa = o_nsa.transpose(0, 2, 1, 3).reshape(bsz * seq, NSA_HEADS * NSA_HEAD_DIM)
    return o_gla, o_nsa


def kernel(x, c, positions, l0_ada_w, l0_ada_b, l0_mix_pre_g, l0_mix_post_g, l0_w_in, l0_conv_w, l0_conv_b, l0_dt_bias, l0_a_log, l0_d_skip, l0_ssm_norm_g, l0_q_a_norm_g, l0_w_q_b, l0_kv_a_norm_g, l0_w_kv_b, l0_w_out, l0_ffn_pre_g, l0_ffn_post_g, l0_w_gate, l0_w_up, l0_w_down, l1_ada_w, l1_ada_b, l1_mix_pre_g, l1_mix_post_g, l1_w_in, l1_w_gk2, l1_b_gk, l1_gla_norm_g, l1_cmp_pos, l1_cmp_k_w1, l1_cmp_k_w2, l1_cmp_v_w1, l1_cmp_v_w2, l1_w_out, l1_ffn_pre_g, l1_ffn_post_g, l1_w_gate, l1_w_up, l1_w_down):
    bsz, seq, d = x.shape
    x2 = x.reshape(bsz * seq, d)

    def sublayers(x2, ada_w, ada_b, pre_m, post_m, mixer, w_out, pre_f, post_f, w_gate, w_up, w_down):
        shift_m, scale_m, gate_m, shift_f, scale_f, gate_f = _ada(c, ada_w, ada_b)
        a, b = mixer((x2, pre_m, scale_m, shift_m))
        ka = a.shape[1]
        x2 = _out_res(x2, a, b, w_out[:ka].astype(BF16), w_out[ka:].astype(BF16), post_m, gate_m, seq)
        return _ffn(x2, pre_f, scale_f, shift_f, w_gate.astype(BF16), w_up.astype(BF16), w_down.astype(BF16),
                    post_f, gate_f, seq)

    x2 = sublayers(
        x2, l0_ada_w, l0_ada_b, l0_mix_pre_g, l0_mix_post_g,
        lambda h: _mixer0_parts(h, positions, bsz, seq, l0_w_in, l0_conv_w, l0_conv_b, l0_dt_bias, l0_a_log,
                                l0_d_skip, l0_ssm_norm_g, l0_q_a_norm_g, l0_w_q_b, l0_kv_a_norm_g, l0_w_kv_b),
        l0_w_out, l0_ffn_pre_g, l0_ffn_post_g, l0_w_gate, l0_w_up, l0_w_down)
    x2 = sublayers(
        x2, l1_ada_w, l1_ada_b, l1_mix_pre_g, l1_mix_post_g,
        lambda h: _mixer1_parts(h, bsz, seq, l1_w_in, l1_w_gk2, l1_b_gk, l1_gla_norm_g, l1_cmp_pos,
                                l1_cmp_k_w1, l1_cmp_k_w2, l1_cmp_v_w1, l1_cmp_v_w2),
        l1_w_out, l1_ffn_pre_g, l1_ffn_post_g, l1_w_gate, l1_w_up, l1_w_down)
    return x2.reshape(bsz, seq, d)
```

```python
import functools

import numpy as np
import jax
import jax.numpy as jnp
from jax import lax
from jax.experimental import pallas as pl
from jax.experimental.pallas import tpu as pltpu

F32, BF16 = jnp.float32, jnp.bfloat16
HI = lax.Precision.HIGHEST
LANES = 128
VMEM_LIMIT = 48 * 1024 * 1024

NORM_EPS = 1e-6
NEG_INF = -1e30
LOG2E = 1.4426950408889634
MASK_DIST = 1e33
N_MOD = 6

SSD_HEADS, SSD_HEAD_DIM, SSD_STATE, SSD_GROUPS, SSD_CONV = 16, 64, 128, 2, 4
SSD_INNER = SSD_HEADS * SSD_HEAD_DIM
SSD_CONV_CH = SSD_INNER + 2 * SSD_GROUPS * SSD_STATE
SSD_CHUNK = 128

MLA_HEADS, MLA_Q_RANK, MLA_KV_RANK, MLA_NOPE, MLA_ROPE, MLA_V = 8, 384, 256, 64, 32, 64
ROPE_THETA = 10000.0

GLA_HEADS, GLA_DK, GLA_DV, GLA_GATE_RANK, GLA_GATE_NORM = 4, 128, 256, 16, 16.0
GLA_CHUNK = 128
GLA_SUB = 16

NSA_HEADS, NSA_GROUPS, NSA_HEAD_DIM = 8, 2, 64
NSA_HG = NSA_HEADS // NSA_GROUPS
NSA_CMP_BLOCK, NSA_CMP_STRIDE, NSA_CMP_HIDDEN = 32, 16, 256
NSA_SEL_BLOCK, NSA_N_SEL, NSA_WINDOW, NSA_FORCE = 64, 16, 512, 1e4
_SEL_SHIFT = NSA_SEL_BLOCK.bit_length() - 1
assert 1 << _SEL_SHIFT == NSA_SEL_BLOCK


def _cparams(sem):
    return pltpu.CompilerParams(dimension_semantics=sem, vmem_limit_bytes=VMEM_LIMIT)


def _dot(a, b, prec=None):
    return jnp.dot(a, b, preferred_element_type=F32, precision=prec)


def _dot_nt(a, b, prec=None):
    return lax.dot_general(a, b, (((1,), (1,)), ((), ())), preferred_element_type=F32, precision=prec)


def _split(x, n):
    parts = []
    for _ in range(n):
        p = x.astype(BF16)
        parts.append(p)
        x = x - p.astype(F32)
    return parts


def _dot_01_left(sel, x, n):
    return sum(_dot(sel, p) for p in _split(x, n))


def _dot_01_right(x, sel, n):
    return sum(_dot(p, sel) for p in _split(x, n))


def _dot3(a, b, nt=False):
    f = _dot_nt if nt else _dot
    (ah, al), (bh, bl) = _split(a, 2), _split(b, 2)
    return f(ah, bh) + (f(ah, bl) + f(al, bh))


def _silu(x):
    return x * (1.0 / (1.0 + jnp.exp(-x)))


def _softplus(x):
    return jnp.maximum(x, 0.0) + jnp.log1p(jnp.exp(-jnp.abs(x)))


def _rms(x, g):
    return x * lax.rsqrt(jnp.mean(x * x, axis=-1, keepdims=True) + NORM_EPS) * g


def _tril(n):
    return lax.broadcasted_iota(jnp.int32, (n, n), 0) >= lax.broadcasted_iota(jnp.int32, (n, n), 1)


def _alibi_table(n, groups):
    s = 2.0 ** (-8.0 * np.arange(1, n + 1) / n)
    return [[float(v) for v in row] for row in s.reshape(groups, n // groups)]


def _by_group(g, table, h):
    val = table[0][h]
    for gi in range(1, len(table)):
        val = jnp.where(g == gi, table[gi][h], val)
    return val


def _ada_kernel(c_ref, w_ref, b_ref, o_ref):
    o_ref[...] = _dot(_silu(c_ref[...]), w_ref[...], HI) + b_ref[...]


def _ada(c, w, b):
    bsz, d = c.shape
    n = w.shape[1]
    rows = 8
    cp = jnp.zeros((rows, d), F32).at[:bsz].set(c)
    tn = 1024
    out = pl.pallas_call(
        _ada_kernel,
        grid=(n // tn,),
        in_specs=[pl.BlockSpec((rows, d), lambda j: (0, 0)),
                  pl.BlockSpec((d, tn), lambda j: (0, j)),
                  pl.BlockSpec((1, tn), lambda j: (0, j))],
        out_specs=pl.BlockSpec((rows, tn), lambda j: (0, j)),
        out_shape=jax.ShapeDtypeStruct((rows, n), F32),
        compiler_params=_cparams(("arbitrary",)),
        name="ada",
    )(cp, w, b.reshape(1, n))
    return [m.reshape(bsz, 1, d) for m in jnp.split(out[:bsz], N_MOD, axis=-1)]


def _proj_in_kernel(x_ref, g_ref, sc_ref, sh_ref, w_ref, *o_refs, segs):
    h = (_rms(x_ref[...], g_ref[...]) * (1.0 + sc_ref[0]) + sh_ref[0]).astype(BF16)
    for (a, b, heads), o_ref in zip(segs, o_refs):
        res = _dot(h, w_ref[:, a:b])
        if heads:
            for g in range(heads):
                dh = (b - a) // heads
                o_ref[0, g] = res[:, g * dh:(g + 1) * dh].astype(o_ref.dtype)
        else:
            o_ref[...] = res.astype(o_ref.dtype)


def _proj_in(x2, g, scale, shift, w, segs, seq, tm=256):
    m, d = x2.shape
    n = w.shape[1]
    tpb = seq // tm
    return pl.pallas_call(
        functools.partial(_proj_in_kernel, segs=segs),
        grid=(m // tm,),
        in_specs=[pl.BlockSpec((tm, d), lambda i: (i, 0)),
                  pl.BlockSpec((1, d), lambda i: (0, 0)),
                  pl.BlockSpec((1, 1, d), lambda i: (i // tpb, 0, 0)),
                  pl.BlockSpec((1, 1, d), lambda i: (i // tpb, 0, 0)),
                  pl.BlockSpec((d, n), lambda i: (0, 0))],
        out_specs=[pl.BlockSpec((1, hd, tm, (b - a) // hd), lambda i: (i // tpb, 0, i % tpb, 0)) if hd else
                   pl.BlockSpec((tm, b - a), lambda i: (i, 0)) for a, b, hd in segs],
        out_shape=[jax.ShapeDtypeStruct((m // seq, hd, seq, (b - a) // hd), BF16) if hd else
                   jax.ShapeDtypeStruct((m, b - a), F32) for a, b, hd in segs],
        compiler_params=_cparams(("arbitrary",)),
        name="proj_in",
    )(x2, g.reshape(1, d), scale, shift, w)


def _out_res_kernel(x_ref, a_ref, b_ref, wa_ref, wb_ref, g_ref, gate_ref, o_ref):
    y = _dot(a_ref[...], wa_ref[...]) + _dot(b_ref[...], wb_ref[...])
    o_ref[...] = x_ref[...] + gate_ref[0] * _rms(y, g_ref[...])


def _out_res(x2, a, b, wa, wb, g, gate, seq, tm=512):
    m, d = x2.shape
    ka, kb = a.shape[1], b.shape[1]
    tpb = seq // tm
    return pl.pallas_call(
        _out_res_kernel,
        grid=(m // tm,),
        in_specs=[pl.BlockSpec((tm, d), lambda i: (i, 0)),
                  pl.BlockSpec((tm, ka), lambda i: (i, 0)),
                  pl.BlockSpec((tm, kb), lambda i: (i, 0)),
                  pl.BlockSpec((ka, d), lambda i: (0, 0)),
                  pl.BlockSpec((kb, d), lambda i: (0, 0)),
                  pl.BlockSpec((1, d), lambda i: (0, 0)),
                  pl.BlockSpec((1, 1, d), lambda i: (i // tpb, 0, 0))],
        out_specs=pl.BlockSpec((tm, d), lambda i: (i, 0)),
        out_shape=jax.ShapeDtypeStruct((m, d), F32),
        compiler_params=_cparams(("arbitrary",)),
        name="out_res",
    )(x2, a, b, wa, wb, g.reshape(1, d), gate)


def _ffn_kernel(x_ref, gpre_ref, sc_ref, sh_ref, wg_ref, wu_ref, wd_ref, gpost_ref, gate_ref, o_ref,
                h_scr, acc_scr):
    j = pl.program_id(1)

    @pl.when(j == 0)
    def _():
        h_scr[...] = (_rms(x_ref[...], gpre_ref[...]) * (1.0 + sc_ref[0]) + sh_ref[0]).astype(BF16)
        acc_scr[...] = jnp.zeros_like(acc_scr)

    h = h_scr[...]
    act = (_silu(_dot(h, wg_ref[...])) * _dot(h, wu_ref[...])).astype(BF16)
    acc_scr[...] += _dot(act, wd_ref[...])

    @pl.when(j == pl.num_programs(1) - 1)
    def _():
        o_ref[...] = x_ref[...] + gate_ref[0] * _rms(acc_scr[...], gpost_ref[...])


def _ffn(x2, gpre, scale, shift, wg, wu, wd, gpost, gate, seq, tm=512, th=1408):
    m, d = x2.shape
    hid = wg.shape[1]
    tpb = seq // tm
    return pl.pallas_call(
        _ffn_kernel,
        grid=(m // tm, hid // th),
        in_specs=[pl.BlockSpec((tm, d), lambda i, j: (i, 0)),
                  pl.BlockSpec((1, d), lambda i, j: (0, 0)),
                  pl.BlockSpec((1, 1, d), lambda i, j: (i // tpb, 0, 0)),
                  pl.BlockSpec((1, 1, d), lambda i, j: (i // tpb, 0, 0)),
                  pl.BlockSpec((d, th), lambda i, j: (0, j)),
                  pl.BlockSpec((d, th), lambda i, j: (0, j)),
                  pl.BlockSpec((th, d), lambda i, j: (j, 0)),
                  pl.BlockSpec((1, d), lambda i, j: (0, 0)),
                  pl.BlockSpec((1, 1, d), lambda i, j: (i // tpb, 0, 0))],
        out_specs=pl.BlockSpec((tm, d), lambda i, j: (i, 0)),
        out_shape=jax.ShapeDtypeStruct((m, d), F32),
        scratch_shapes=[pltpu.VMEM((tm, d), BF16), pltpu.VMEM((tm, d), F32)],
        compiler_params=_cparams(("arbitrary", "arbitrary")),
        name="ffn",
    )(x2, gpre.reshape(1, d), scale, shift, wg, wu, wd, gpost.reshape(1, d), gate)


def _ssd_kernel(z_ref, xbc_ref, misc_ref, cw_ref, cb_ref, dtb_ref, alog_ref, dsk_ref, ng_ref, e_ref, o_ref,
                ext_scr, st_scr):
    L = SSD_CHUNK
    gsz = SSD_INNER // SSD_GROUPS
    hpg = SSD_HEADS // SSD_GROUPS
    pad = 8

    @pl.when(pl.program_id(1) == 0)
    def _():
        ext_scr[0:pad, :] = jnp.zeros((pad, SSD_CONV_CH), F32)
        st_scr[...] = jnp.zeros_like(st_scr)

    xt = xbc_ref[...]
    ext_scr[pad:pad + L, :] = xt
    acc = cb_ref[...] + cw_ref[0:1, :] * ext_scr[pad - 3:pad - 3 + L, :]
    for k in range(1, SSD_CONV):
        acc = acc + cw_ref[k:k + 1, :] * ext_scr[pad - 3 + k:pad - 3 + k + L, :]
    ext_scr[0:pad, :] = xt[L - pad:L, :]
    xbc = _silu(acc)
    xs = xbc[:, :SSD_INNER]

    e = e_ref[...]
    dt = _softplus(misc_ref[...] + dtb_ref[...])
    adt = dt * (-jnp.exp(alog_ref[...]))
    tril = _tril(L)
    a_cs = _dot_01_left(jnp.where(tril, 1.0, 0.0).astype(BF16), adt, 3)
    a_cs_t = a_cs.T
    ea = jnp.exp(a_cs)
    ea_e = _dot_01_right(ea, e, 2)
    dec_e = _dot_01_right(jnp.exp(a_cs[L - 1:L, :] - a_cs), e, 2)
    xd = xs * _dot_01_right(dt, e, 2)

    ys = []
    for g in range(SSD_GROUPS):
        bg = xbc[:, SSD_INNER + g * SSD_STATE:SSD_INNER + (g + 1) * SSD_STATE]
        cg = xbc[:, SSD_INNER + (SSD_GROUPS + g) * SSD_STATE:SSD_INNER + (SSD_GROUPS + g + 1) * SSD_STATE]
        bg16, cg16 = bg.astype(BF16), cg.astype(BF16)
        gmat = _dot_nt(cg16, bg16)
        cols = slice(g * gsz, (g + 1) * gsz)
        xdg = xd[:, cols]
        st = st_scr[g]
        y_off = _dot(cg16, st.astype(BF16)) * ea_e[:, cols]
        st_scr[g] = st * ea_e[L - 1:L, cols] + _dot(bg.T.astype(BF16), (xdg * dec_e[:, cols]).astype(BF16))
        yd = []
        for h in range(hpg):
            hh = g * hpg + h
            seg = a_cs[:, hh:hh + 1] - a_cs_t[hh:hh + 1, :]
            lmat = jnp.where(tril, jnp.exp(seg), 0.0)
            yd.append(_dot((gmat * lmat).astype(BF16),
                           xdg[:, h * SSD_HEAD_DIM:(h + 1) * SSD_HEAD_DIM].astype(BF16)))
        y = jnp.concatenate(yd, axis=-1) + y_off + dsk_ref[:, cols] * xs[:, cols]
        y = y * _silu(z_ref[:, cols])
        ys.append(_rms(y, ng_ref[:, cols]))
    o_ref[...] = jnp.concatenate(ys, axis=-1).astype(o_ref.dtype)


def _ssd(z, xbc, misc, conv_w, conv_b, dt_bias, a_log, d_skip, norm_g, bsz, seq):
    L = SSD_CHUNK
    nc = seq // L
    m = bsz * seq
    e = np.zeros((LANES, SSD_INNER), np.float32)
    for h in range(SSD_HEADS):
        e[h, h * SSD_HEAD_DIM:(h + 1) * SSD_HEAD_DIM] = 1.0
    pad128 = lambda v: jnp.zeros((1, LANES), F32).at[0, :v.shape[0]].set(v)
    row = lambda i, c: (i * nc + c, 0)
    const = lambda i, c: (0, 0)
    return pl.pallas_call(
        _ssd_kernel,
        grid=(bsz, nc),
        in_specs=[pl.BlockSpec((L, SSD_INNER), row),
                  pl.BlockSpec((L, SSD_CONV_CH), row),
                  pl.BlockSpec((L, LANES), row),
                  pl.BlockSpec((SSD_CONV, SSD_CONV_CH), const),
                  pl.BlockSpec((1, SSD_CONV_CH), const),
                  pl.BlockSpec((1, LANES), const),
                  pl.BlockSpec((1, LANES), const),
                  pl.BlockSpec((1, SSD_INNER), const),
                  pl.BlockSpec((1, SSD_INNER), const),
                  pl.BlockSpec((LANES, SSD_INNER), const)],
        out_specs=pl.BlockSpec((L, SSD_INNER), row),
        out_shape=jax.ShapeDtypeStruct((m, SSD_INNER), BF16),
        scratch_shapes=[pltpu.VMEM((L + 8, SSD_CONV_CH), F32),
                        pltpu.VMEM((SSD_GROUPS, SSD_STATE, SSD_INNER // SSD_GROUPS), F32)],
        compiler_params=_cparams(("arbitrary", "arbitrary")),
        name="ssd",
    )(z, xbc, misc, conv_w, conv_b.reshape(1, -1), pad128(dt_bias), pad128(a_log),
      jnp.repeat(d_skip, SSD_HEAD_DIM).reshape(1, -1), norm_g.reshape(1, -1), jnp.asarray(e, BF16))


MLA_D = LANES
MLA_HEADS_PER_STEP = 8
_R1 = MLA_NOPE
_R2 = MLA_NOPE + MLA_ROPE // 2


def _mla_prep_kernel(qa_ref, kva_ref, misc_ref, pos_ref, gq_ref, gkv_ref, wq_ref, wqs_ref, wk_ref, wv_ref,
                     pk_ref, pks_ref, invf_ref, sgn_ref, q_ref, k_ref, v_ref):
    ang = pos_ref[...].astype(F32) * invf_ref[...]
    cos = jnp.cos(ang)
    sin = jnp.sin(ang) * sgn_ref[...]
    nq = _rms(qa_ref[...], gq_ref[...]).astype(BF16)
    nkv = _rms(kva_ref[...], gkv_ref[...]).astype(BF16)
    misc = misc_ref[...]
    k_rot = _dot_01_right(misc, pk_ref[...], 3) * cos + _dot_01_right(misc, pks_ref[...], 3) * sin
    for h in range(MLA_HEADS):
        qh = _dot(nq, wq_ref[h]) * cos + _dot(nq, wqs_ref[h]) * sin
        q_ref[0, h] = (qh * ((MLA_NOPE + MLA_ROPE) ** -0.5 * LOG2E)).astype(q_ref.dtype)
        k_ref[0, h] = (_dot(nkv, wk_ref[h]) + k_rot).astype(k_ref.dtype)
        v_ref[0, h] = _dot(nkv, wv_ref[h]).astype(v_ref.dtype)


def _mla_prep(q_a, kv_a, misc, positions, gq, gkv, w_q_b, w_kv_b, bsz, seq, tm=512):
    m = bsz * seq
    half = MLA_ROPE // 2
    dq = MLA_NOPE + MLA_ROPE
    wq3 = w_q_b.reshape(MLA_Q_RANK, MLA_HEADS, dq).transpose(1, 0, 2)
    zq = jnp.zeros((MLA_HEADS, MLA_Q_RANK, MLA_D - dq), F32)
    wq = jnp.concatenate([wq3, zq], axis=-1).astype(BF16)
    wqs = jnp.concatenate([jnp.zeros((MLA_HEADS, MLA_Q_RANK, MLA_NOPE), F32), wq3[..., _R2:dq], wq3[..., _R1:_R2], zq],
                          axis=-1).astype(BF16)
    wkv3 = w_kv_b.reshape(MLA_KV_RANK, MLA_HEADS, MLA_NOPE + MLA_V).transpose(1, 0, 2)
    wk = jnp.concatenate([wkv3[..., :MLA_NOPE], jnp.zeros((MLA_HEADS, MLA_KV_RANK, MLA_D - MLA_NOPE), F32)],
                         axis=-1).astype(BF16)
    wv = wkv3[..., MLA_NOPE:].astype(BF16)
    pk = np.zeros((LANES, MLA_D), np.float32)
    pks = np.zeros((LANES, MLA_D), np.float32)
    invf = np.zeros((1, MLA_D), np.float32)
    sgn = np.zeros((1, MLA_D), np.float32)
    inv = (ROPE_THETA ** (-np.arange(0, MLA_ROPE, 2) / MLA_ROPE)).astype(np.float32)
    for i in range(half):
        pk[SSD_HEADS + i, _R1 + i] = 1.0
        pk[SSD_HEADS + half + i, _R2 + i] = 1.0
        pks[SSD_HEADS + half + i, _R1 + i] = 1.0
        pks[SSD_HEADS + i, _R2 + i] = 1.0
        invf[0, _R1 + i] = invf[0, _R2 + i] = inv[i]
        sgn[0, _R1 + i], sgn[0, _R2 + i] = -1.0, 1.0
    tpb = seq // tm
    c2 = lambda i: (0, 0)
    c3 = lambda i: (0, 0, 0)
    hm = lambda i: (i // tpb, 0, i % tpb, 0)
    return pl.pallas_call(
        _mla_prep_kernel,
        grid=(m // tm,),
        in_specs=[pl.BlockSpec((tm, MLA_Q_RANK), lambda i: (i, 0)),
                  pl.BlockSpec((tm, MLA_KV_RANK), lambda i: (i, 0)),
                  pl.BlockSpec((tm, LANES), lambda i: (i, 0)),
                  pl.BlockSpec((tm, 1), lambda i: (i, 0)),
                  pl.BlockSpec((1, MLA_Q_RANK), c2),
                  pl.BlockSpec((1, MLA_KV_RANK), c2),
                  pl.BlockSpec((MLA_HEADS, MLA_Q_RANK, MLA_D), c3),
                  pl.BlockSpec((MLA_HEADS, MLA_Q_RANK, MLA_D), c3),
                  pl.BlockSpec((MLA_HEADS, MLA_KV_RANK, MLA_D), c3),
                  pl.BlockSpec((MLA_HEADS, MLA_KV_RANK, MLA_V), c3),
                  pl.BlockSpec((LANES, MLA_D), c2),
                  pl.BlockSpec((LANES, MLA_D), c2),
                  pl.BlockSpec((1, MLA_D), c2),
                  pl.BlockSpec((1, MLA_D), c2)],
        out_specs=[pl.BlockSpec((1, MLA_HEADS, tm, MLA_D), hm),
                   pl.BlockSpec((1, MLA_HEADS, tm, MLA_D), hm),
                   pl.BlockSpec((1, MLA_HEADS, tm, MLA_V), hm)],
        out_shape=[jax.ShapeDtypeStruct((bsz, MLA_HEADS, seq, MLA_D), BF16),
                   jax.ShapeDtypeStruct((bsz, MLA_HEADS, seq, MLA_D), BF16),
                   jax.ShapeDtypeStruct((bsz, MLA_HEADS, seq, MLA_V), BF16)],
        compiler_params=_cparams(("arbitrary",)),
        name="mla_prep",
    )(q_a, kv_a, misc, positions.reshape(m, 1), gq.reshape(1, -1), gkv.reshape(1, -1), wq, wqs, wk, wv,
      jnp.asarray(pk, BF16), jnp.asarray(pks, BF16), jnp.asarray(invf), jnp.asarray(sgn))


_FIRST, _LAST, _MASKED = 1, 2, 4
SOFTMAX_ROWS = 64


def _flash_schedule(seq, tq, tk, window):
    qi_l, kb_l, fl_l = [], [], []
    for qi in range(seq // tq):
        q0, q1 = qi * tq, qi * tq + tq - 1
        first = q0 // tk
        lo = 0 if window is None else max(0, q0 - window + 1) // tk
        blocks = [first] + [b for b in range(lo, q1 // tk + 1) if b != first]
        for n, kb in enumerate(blocks):
            k0, k1 = kb * tk, kb * tk + tk - 1
            masked = k1 > q0 or (window is not None and q1 - k0 >= window)
            qi_l.append(qi)
            kb_l.append(kb)
            fl_l.append((_FIRST if n == 0 else 0) | (_LAST if n == len(blocks) - 1 else 0) | (_MASKED if masked else 0))
    return [jnp.asarray(np.asarray(t, np.int32)) for t in (qi_l, kb_l, fl_l)]


def _flash_kernel(qi_ref, kb_ref, fl_ref, *refs, hg, tq, tk, dv, slopes, window, has_sel, kv_per_head):
    if has_sel:
        q_ref, k_ref, v_ref, selb_ref, o_ref, m_scr, l_scr, alpha_scr, acc_scr, s_scr, p_scr, bias_scr = refs
    else:
        q_ref, k_ref, v_ref, o_ref, m_scr, l_scr, alpha_scr, acc_scr, s_scr, p_scr, bias_scr = refs
    g = pl.program_id(1)
    st = pl.program_id(2)
    qi, kb, fl = qi_ref[st], kb_ref[st], fl_ref[st]
    masked = (fl & _MASKED) != 0
    rows = hg * tq
    rc = SOFTMAX_ROWS

    @pl.when((fl & _FIRST) != 0)
    def _():
        m_scr[...] = jnp.full_like(m_scr, NEG_INF)
        l_scr[...] = jnp.zeros_like(l_scr)
        acc_scr[...] = jnp.zeros_like(acc_scr)

    def mask_bias():
        dist = (qi * tq + lax.broadcasted_iota(jnp.int32, (tq, tk), 0)) - \
               (kb * tk + lax.broadcasted_iota(jnp.int32, (tq, tk), 1))
        ok = dist >= 0
        if window is not None:
            ok = ok & (dist < window)
        return jnp.where(ok, 0.0, NEG_INF)

    def step(with_mask):
        with_bias = has_sel or with_mask
        if has_sel:
            blk = lax.broadcasted_iota(jnp.int32, (LANES, tk), 0)
            key = kb * tk + lax.broadcasted_iota(jnp.int32, (LANES, tk), 1)
            expand = jnp.where((key >> _SEL_SHIFT) == blk, 1.0, 0.0).astype(BF16)
            bias = _dot(selb_ref[0, 0], expand)
            bias_scr[...] = bias + mask_bias() if with_mask else bias
        elif with_mask:
            bias_scr[...] = mask_bias()
        kv = (lambda ref, h: ref[0, 0, h]) if kv_per_head else (lambda ref, h: ref[0, 0])
        for h in range(hg):
            s_scr[h * tq:(h + 1) * tq, :] = _dot_nt(q_ref[0, 0, h], kv(k_ref, h))
        for h in range(hg):
            if slopes is not None:
                rel = (kb * tk - qi * tq + lax.broadcasted_iota(jnp.int32, (1, tk), 1)).astype(F32)
                arow = (_by_group(g, slopes, h) * LOG2E) * rel
            for c in range(tq // rc):
                rs = slice(h * tq + c * rc, h * tq + (c + 1) * rc)
                s = s_scr[rs, :]
                if with_bias:
                    s = s + bias_scr[c * rc:(c + 1) * rc, :]
                if slopes is not None:
                    s = s + arow
                if with_bias or slopes is not None:
                    s_scr[rs, :] = s
                m_prev = m_scr[rs, :]
                m_next = jnp.maximum(m_prev, jnp.max(s, axis=1, keepdims=True))
                alpha_scr[rs, :] = jnp.exp2(m_prev - m_next)
                m_scr[rs, :] = m_next
        for h in range(hg):
            for c in range(tq // rc):
                rs = slice(h * tq + c * rc, h * tq + (c + 1) * rc)
                p = jnp.exp2(s_scr[rs, :] - jnp.tile(m_scr[rs, :], (1, tk // LANES)))
                alpha = alpha_scr[rs, :]
                l_scr[rs, :] = alpha * l_scr[rs, :] + jnp.sum(p, axis=1, keepdims=True)
                acc_scr[rs, :] = acc_scr[rs, :] * alpha[:, :dv]
                p_scr[rs, :] = p.astype(BF16)
        for h in range(hg):
            hs = slice(h * tq, (h + 1) * tq)
            acc_scr[hs, :] += _dot(p_scr[hs, :], kv(v_ref, h))

    pl.when(masked)(lambda: step(True))
    pl.when(jnp.logical_not(masked))(lambda: step(False))

    @pl.when((fl & _LAST) != 0)
    def _():
        out = acc_scr[...] / jnp.maximum(l_scr[:, :dv], 1e-30)
        o_ref[0] = jnp.concatenate([out[h * tq:(h + 1) * tq] for h in range(hg)], axis=-1).astype(o_ref.dtype)


def _flash(q, k, v, selb, *, tq, tk, slopes=None, window=None, out_dtype=F32):
    bsz, ng, hg, seq, d = q.shape
    dv = v.shape[-1]
    kv_per_head = k.ndim == 5
    sched = _flash_schedule(seq, tq, tk, window)
    rows = hg * tq
    if kv_per_head:
        kv_spec = lambda w: pl.BlockSpec((1, 1, hg, tk, w), lambda b, g, s, qi, kb, fl: (b, g, 0, kb[s], 0))
    else:
        kv_spec = lambda w: pl.BlockSpec((1, 1, tk, w), lambda b, g, s, qi, kb, fl: (b, g, kb[s], 0))
    in_specs = [pl.BlockSpec((1, 1, hg, tq, d), lambda b, g, s, qi, kb, fl: (b, g, 0, qi[s], 0)),
                kv_spec(d), kv_spec(dv)]
    args = [q, k, v]
    if selb is not None:
        in_specs.append(pl.BlockSpec((1, 1, tq, LANES), lambda b, g, s, qi, kb, fl: (b, g, qi[s], 0)))
        args.append(selb)
    return pl.pallas_call(
        functools.partial(_flash_kernel, hg=hg, tq=tq, tk=tk, dv=dv, slopes=slopes, window=window,
                          has_sel=selb is not None, kv_per_head=kv_per_head),
        grid_spec=pltpu.PrefetchScalarGridSpec(
            num_scalar_prefetch=3,
            grid=(bsz, ng, int(sched[0].shape[0])),
            in_specs=in_specs,
            out_specs=pl.BlockSpec((1, tq, hg * dv), lambda b, g, s, qi, kb, fl: (b, qi[s], g)),
            scratch_shapes=[pltpu.VMEM((rows, LANES), F32), pltpu.VMEM((rows, LANES), F32),
                            pltpu.VMEM((rows, LANES), F32),
                            pltpu.VMEM((rows, dv), F32), pltpu.VMEM((rows, tk), F32),
                            pltpu.VMEM((rows, tk), BF16), pltpu.VMEM((tq, tk), F32)]),
        out_shape=jax.ShapeDtypeStruct((bsz, seq, ng * hg * dv), out_dtype),
        compiler_params=_cparams(("arbitrary", "arbitrary", "arbitrary")),
        name="flash",
    )(*sched, *args)


def _gla_kernel(q_ref, k_ref, v_ref, gg_ref, misc_ref, w2_ref, bgk_ref, ng_ref, o_ref, st_scr):
    L = GLA_CHUNK

    @pl.when(pl.program_id(1) == 0)
    def _():
        st_scr[...] = jnp.zeros_like(st_scr)

    zg = _dot3(misc_ref[...], w2_ref[...]) + bgk_ref[...]
    log_a = -_softplus(-zg) * (1.0 / GLA_GATE_NORM)
    tril = _tril(L)
    bc_all = _dot_01_left(jnp.where(tril, 1.0, 0.0).astype(BF16), log_a, 3)
    row = lax.broadcasted_iota(jnp.int32, (L, GLA_DK), 0)
    for h in range(GLA_HEADS):
        kc, vc = slice(h * GLA_DK, (h + 1) * GLA_DK), slice(h * GLA_DV, (h + 1) * GLA_DV)
        bc = bc_all[:, kc]
        q = q_ref[:, kc] * (GLA_DK ** -0.5)
        k = k_ref[:, kc]
        v = v_ref[:, vc]
        st = st_scr[h]
        o = _dot_nt((q * jnp.exp(bc)).astype(BF16), st.astype(BF16))
        att = []
        for i in range(L // GLA_SUB):
            r0 = i * GLA_SUB
            ref = bc[r0:r0 + 1, :]
            qi = q[r0:r0 + GLA_SUB] * jnp.exp(bc[r0:r0 + GLA_SUB] - ref)
            ki = k * jnp.exp(jnp.where(row < r0 + GLA_SUB, ref - bc, 0.0))
            att.append(_dot_nt(qi.astype(BF16), ki.astype(BF16)))
        att = jnp.where(tril, jnp.concatenate(att, axis=0), 0.0)
        o = o + _dot(att.astype(BF16), v.astype(BF16))
        b_last = bc[L - 1:L, :]
        kd = (k * jnp.exp(b_last - bc)).astype(BF16)
        st_scr[h] = st * jnp.exp(b_last) + _dot(v.T.astype(BF16), kd)
        o_ref[:, vc] = (_rms(o, ng_ref[...]) * _silu(gg_ref[:, vc])).astype(o_ref.dtype)


def _gla(gq, gk, gv, gg, misc, w_gk2, b_gk, norm_g, bsz, seq):
    L = GLA_CHUNK
    nc = seq // L
    m = bsz * seq
    nk, nv = GLA_HEADS * GLA_DK, GLA_HEADS * GLA_DV
    w2 = jnp.zeros((LANES, nk), F32).at[:GLA_GATE_RANK].set(w_gk2)
    row = lambda b, c: (b * nc + c, 0)
    const = lambda b, c: (0, 0)
    return pl.pallas_call(
        _gla_kernel,
        grid=(bsz, nc),
        in_specs=[pl.BlockSpec((L, nk), row),
                  pl.BlockSpec((L, nk), row),
                  pl.BlockSpec((L, nv), row),
                  pl.BlockSpec((L, nv), row),
                  pl.BlockSpec((L, LANES), row),
                  pl.BlockSpec((LANES, nk), const),
                  pl.BlockSpec((1, nk), const),
                  pl.BlockSpec((1, GLA_DV), const)],
        out_specs=pl.BlockSpec((L, nv), row),
        out_shape=jax.ShapeDtypeStruct((m, nv), BF16),
        scratch_shapes=[pltpu.VMEM((GLA_HEADS, GLA_DV, GLA_DK), F32)],
        compiler_params=_cparams(("arbitrary", "arbitrary")),
        name="gla",
    )(gq, gk, gv, gg, misc, w2, b_gk.reshape(1, -1), norm_g.reshape(1, -1))


def _cmp_kernel(z_ref, pa_ref, pb_ref, wa_ref, wb_ref, w2_ref, o_ref, *, prec):
    z = z_ref[0]
    if prec is None:
        cast = lambda t: t.astype(BF16)
    else:
        cast = lambda t: t
    first = _dot(cast(z + pa_ref[...]), cast(wa_ref[...]), prec)
    second = _dot(cast(z + pb_ref[...]), cast(wb_ref[...]), prec)
    n = first.shape[0]
    hid = _silu(first + pltpu.roll(second, n - 1, 0))
    for g in range(NSA_GROUPS):
        o_ref[0, g] = _dot(cast(hid[:, g * NSA_CMP_HIDDEN:(g + 1) * NSA_CMP_HIDDEN]), cast(w2_ref[...]), prec)


def _compress(z, cmp_pos, w1, w2, bsz, seq, prec):
    nb = seq // NSA_CMP_STRIDE
    width = NSA_CMP_STRIDE * NSA_GROUPS * NSA_HEAD_DIM
    zr = z.reshape(bsz, nb, width)
    per = NSA_CMP_BLOCK // NSA_CMP_STRIDE
    eye = jnp.eye(NSA_GROUPS, dtype=F32)
    wbig = jnp.einsum("ldj,gh->lgdhj", w1.reshape(NSA_CMP_BLOCK, NSA_HEAD_DIM, NSA_CMP_HIDDEN), eye)
    wbig = wbig.reshape(per, width, NSA_GROUPS * NSA_CMP_HIDDEN)
    posb = jnp.broadcast_to(cmp_pos[:, None, :], (NSA_CMP_BLOCK, NSA_GROUPS, NSA_HEAD_DIM)).reshape(per, 1, width)
    assert per == 2
    c2 = lambda b: (0, 0)
    return pl.pallas_call(
        functools.partial(_cmp_kernel, prec=prec),
        grid=(bsz,),
        in_specs=[pl.BlockSpec((1, nb, width), lambda b: (b, 0, 0)),
                  pl.BlockSpec((1, width), c2), pl.BlockSpec((1, width), c2),
                  pl.BlockSpec((width, NSA_GROUPS * NSA_CMP_HIDDEN), c2),
                  pl.BlockSpec((width, NSA_GROUPS * NSA_CMP_HIDDEN), c2),
                  pl.BlockSpec((NSA_CMP_HIDDEN, NSA_HEAD_DIM), c2)],
        out_specs=pl.BlockSpec((1, NSA_GROUPS, nb, NSA_HEAD_DIM), lambda b: (b, 0, 0, 0)),
        out_shape=jax.ShapeDtypeStruct((bsz, NSA_GROUPS, nb, NSA_HEAD_DIM), F32),
        compiler_params=_cparams(("arbitrary",)),
        name="nsa_compress",
    )(zr, posb[0], posb[1], wbig[0], wbig[1], w2)


def _nsa_select_kernel(q_ref, kc_ref, vc_ref, ovt_ref, oc_ref, selb_ref, qs_ref, *, tq, n_sel, slopes, scale):
    g = pl.program_id(1)
    qi = pl.program_id(2)
    ncmp = kc_ref.shape[2]
    nslc = ovt_ref.shape[0]
    qpos_c = qi * tq + lax.broadcasted_iota(jnp.int32, (ncmp, tq), 1)
    cmp_end = lax.broadcasted_iota(jnp.int32, (ncmp, tq), 0) * NSA_CMP_STRIDE + (NSA_CMP_BLOCK - 1)
    dist = (qpos_c - cmp_end).astype(F32)
    dist = jnp.where(dist >= 0.0, dist, MASK_DIST)
    kc_hi, kc_lo = _split(kc_ref[0, 0], 2)
    vc = vc_ref[0, 0].astype(BF16)
    dh = NSA_HEAD_DIM
    o_c = []
    psum = jnp.zeros((ncmp, tq), F32)
    for h in range(NSA_HG):
        q = q_ref[:, h * dh:(h + 1) * dh] * (scale * LOG2E)
        qs_ref[0, 0, h] = q.astype(qs_ref.dtype)
        q_hi, q_lo = _split(q, 2)
        qk = _dot_nt(kc_hi, q_hi) + (_dot_nt(kc_hi, q_lo) + _dot_nt(kc_lo, q_hi))
        s = qk - (_by_group(g, slopes, h) * LOG2E) * dist
        mx = jnp.max(s, axis=0, keepdims=True)
        ex = jnp.exp2(s - mx)
        den = jnp.maximum(jnp.sum(ex, axis=0, keepdims=True), 1e-30)
        p = ex * jnp.where(mx > NEG_INF, 1.0 / den, 0.0)
        o_c.append(lax.dot_general(p.astype(BF16), vc, (((0,), (0,)), ((), ())), preferred_element_type=F32))
        psum = psum + p
    oc_ref[...] = jnp.concatenate(o_c, axis=-1)
    imp = _dot_01_left(ovt_ref[...], psum, 2)
    blk = lax.broadcasted_iota(jnp.int32, (nslc, tq), 0)
    qpos = qi * tq + lax.broadcasted_iota(jnp.int32, (nslc, tq), 1)
    forced = (blk == 0) | (blk == (qpos >> _SEL_SHIFT))
    avail = blk * NSA_SEL_BLOCK <= qpos
    imp = jnp.where(forced, NSA_FORCE, jnp.where(avail, imp, -1.0))
    blkf = blk.astype(F32)
    selb = jnp.full((nslc, tq), NEG_INF, F32)
    for _ in range(n_sel):
        mx = jnp.max(imp, axis=0, keepdims=True)
        first = jnp.min(jnp.where(imp == mx, blkf, float(nslc)), axis=0, keepdims=True)
        one = blkf == first
        selb = jnp.where(one, 0.0, selb)
        imp = jnp.where(one, -2.0, imp)
    selb_ref[0, 0] = selb.T.astype(selb_ref.dtype)


def _nsa_select(nq, k_cmp, v_cmp, bsz, seq, tq=512):
    ng, hg, dh = NSA_GROUPS, NSA_HG, NSA_HEAD_DIM
    ncmp = k_cmp.shape[2]
    nslc = seq // NSA_SEL_BLOCK
    n_sel = min(NSA_N_SEL, nslc)
    nt = seq // tq
    c_start = np.arange(ncmp) * NSA_CMP_STRIDE
    s_start = np.arange(nslc) * NSA_SEL_BLOCK
    ovt = ((c_start[None, :] < s_start[:, None] + NSA_SEL_BLOCK)
           & (c_start[None, :] + NSA_CMP_BLOCK > s_start[:, None])).astype(np.float32)
    ovt[:, (seq - NSA_CMP_BLOCK) // NSA_CMP_STRIDE + 1:] = 0.0
    return pl.pallas_call(
        functools.partial(_nsa_select_kernel, tq=tq, n_sel=n_sel, slopes=_alibi_table(NSA_HEADS, NSA_GROUPS),
                          scale=NSA_HEAD_DIM ** -0.5),
        grid=(bsz, ng, nt),
        in_specs=[pl.BlockSpec((tq, hg * dh), lambda b, g, i: (b * nt + i, g)),
                  pl.BlockSpec((1, 1, ncmp, dh), lambda b, g, i: (b, g, 0, 0)),
                  pl.BlockSpec((1, 1, ncmp, dh), lambda b, g, i: (b, g, 0, 0)),
                  pl.BlockSpec((nslc, ncmp), lambda b, g, i: (0, 0))],
        out_specs=[pl.BlockSpec((tq, hg * dh), lambda b, g, i: (b * nt + i, g)),
                   pl.BlockSpec((1, 1, tq, nslc), lambda b, g, i: (b, g, i, 0)),
                   pl.BlockSpec((1, 1, hg, tq, dh), lambda b, g, i: (b, g, 0, i, 0))],
        out_shape=[jax.ShapeDtypeStruct((bsz * seq, ng * hg * dh), F32),
                   jax.ShapeDtypeStruct((bsz, ng, seq, nslc), BF16),
                   jax.ShapeDtypeStruct((bsz, ng, hg, seq, dh), BF16)],
        compiler_params=_cparams(("arbitrary", "arbitrary", "arbitrary")),
        name="nsa_select",
    )(nq, k_cmp, v_cmp, jnp.asarray(ovt, BF16))


def _nsa_combine_kernel(oc_ref, os_ref, ow_ref, misc_ref, e_ref, o_ref):
    gates = 1.0 / (1.0 + jnp.exp(-misc_ref[...]))
    o = (_dot_01_right(gates, e_ref[0], 2) * oc_ref[...] + _dot_01_right(gates, e_ref[1], 2) * os_ref[...]
         + _dot_01_right(gates, e_ref[2], 2) * ow_ref[...])
    o_ref[...] = o.astype(o_ref.dtype)


def _nsa_combine(o_c, o_s, o_w, misc, tq=512):
    m, width = o_c.shape
    e = np.zeros((3, LANES, width), np.float32)
    for h in range(NSA_HEADS):
        for r in range(3):
            e[r, GLA_GATE_RANK + 3 * h + r, h * NSA_HEAD_DIM:(h + 1) * NSA_HEAD_DIM] = 1.0
    spec = pl.BlockSpec((tq, width), lambda i: (i, 0))
    return pl.pallas_call(
        _nsa_combine_kernel,
        grid=(m // tq,),
        in_specs=[spec, spec, spec, pl.BlockSpec((tq, LANES), lambda i: (i, 0)),
                  pl.BlockSpec((3, LANES, width), lambda i: (0, 0, 0))],
        out_specs=spec,
        out_shape=jax.ShapeDtypeStruct((m, width), BF16),
        compiler_params=_cparams(("arbitrary",)),
        name="nsa_combine",
    )(o_c, o_s, o_w, misc, jnp.asarray(e, BF16))


L0_SEGS = ((0, 1024, 0), (1024, 2560, 0), (2560, 2688, 0), (2688, 3072, 0), (3072, 3328, 0))
_G = NSA_GROUPS
L1_SEGS = ((0, 512, 0), (512, 1024, 0), (1024, 2048, 0), (2048, 3072, 0), (3072, 3584, 0), (3584, 3712, 0),
           (3712, 3840, 0), (3840, 3968, _G), (3968, 4096, _G), (4096, 4224, _G), (4224, 4352, _G), (4352, 4480, 0))


def _pack_w_in0(w):
    d = w.shape[0]
    a = SSD_INNER + SSD_CONV_CH
    dt = w[:, a:a + SSD_HEADS]
    qa = w[:, a + SSD_HEADS:a + SSD_HEADS + MLA_Q_RANK]
    kva = w[:, a + SSD_HEADS + MLA_Q_RANK:a + SSD_HEADS + MLA_Q_RANK + MLA_KV_RANK]
    kpe = w[:, a + SSD_HEADS + MLA_Q_RANK + MLA_KV_RANK:]
    pad = jnp.zeros((d, LANES - SSD_HEADS - MLA_ROPE), F32)
    return jnp.concatenate([w[:, :a], dt, kpe, pad, qa, kva], axis=1).astype(BF16)


def _pack_w_in1(w):
    d = w.shape[0]
    qk = 2 * GLA_HEADS * GLA_DK
    vv = GLA_HEADS * GLA_DV
    o = qk + vv
    glr = w[:, o:o + GLA_GATE_RANK]
    gg = w[:, o + GLA_GATE_RANK:o + GLA_GATE_RANK + vv]
    o2 = o + GLA_GATE_RANK + vv
    nsa = w[:, o2:o2 + NSA_HEADS * NSA_HEAD_DIM + 6 * NSA_GROUPS * NSA_HEAD_DIM]
    ngate = w[:, o2 + NSA_HEADS * NSA_HEAD_DIM + 6 * NSA_GROUPS * NSA_HEAD_DIM:]
    pad = jnp.zeros((d, LANES - GLA_GATE_RANK - 3 * NSA_HEADS), F32)
    return jnp.concatenate([w[:, :o], gg, nsa, glr, ngate, pad], axis=1).astype(BF16)


def _mixer0_parts(h_args, positions, bsz, seq, w_in, conv_w, conv_b, dt_bias, a_log, d_skip, ssm_norm_g,
                  q_a_norm_g, w_q_b, kv_a_norm_g, w_kv_b):
    z, xbc, misc, q_a, kv_a = _proj_in(*h_args, _pack_w_in0(w_in), L0_SEGS, seq)
    y_ssd = _ssd(z, xbc, misc, conv_w, conv_b, dt_bias, a_log, d_skip, ssm_norm_g, bsz, seq)
    q, k, v = _mla_prep(q_a, kv_a, misc, positions, q_a_norm_g, kv_a_norm_g, w_q_b, w_kv_b, bsz, seq)
    pair = lambda t: t.reshape(bsz, MLA_HEADS // MLA_HEADS_PER_STEP, MLA_HEADS_PER_STEP, seq, t.shape[-1])
    o = _flash(pair(q), pair(k), pair(v), None, tq=512, tk=512, out_dtype=BF16)
    return y_ssd, o.reshape(bsz * seq, MLA_HEADS * MLA_V)


def _mixer1_parts(h_args, bsz, seq, w_in, w_gk2, b_gk, gla_norm_g, cmp_pos, cmp_k_w1, cmp_k_w2, cmp_v_w1, cmp_v_w2):
    gq, gk, gv, gg, nq, kc, vc, ks, vs, kw, vw, misc = _proj_in(*h_args, _pack_w_in1(w_in), L1_SEGS, seq)
    o_gla = _gla(gq, gk, gv, gg, misc, w_gk2, b_gk, gla_norm_g, bsz, seq)
    k_cmp = _compress(kc, cmp_pos, cmp_k_w1, cmp_k_w2, bsz, seq, HI)
    v_cmp = _compress(vc, cmp_pos, cmp_v_w1, cmp_v_w2, bsz, seq, None)
    o_c, selb, qs = _nsa_select(nq, k_cmp, v_cmp, bsz, seq)
    if selb.shape[-1] < LANES:
        selb = jnp.pad(selb, ((0, 0), (0, 0), (0, 0), (0, LANES - selb.shape[-1])))
    slopes = _alibi_table(NSA_HEADS, NSA_GROUPS)
    width = NSA_HEADS * NSA_HEAD_DIM
    o_s = _flash(qs, ks, vs, selb, tq=512, tk=512, slopes=slopes).reshape(bsz * seq, width)
    o_w = _flash(qs, kw, vw, None, tq=NSA_WINDOW, tk=NSA_WINDOW, slopes=slopes, window=NSA_WINDOW)
    return o_gla, _nsa_combine(o_c, o_s, o_w.reshape(bsz * seq, width), misc)


def kernel(x, c, positions, l0_ada_w, l0_ada_b, l0_mix_pre_g, l0_mix_post_g, l0_w_in, l0_conv_w, l0_conv_b, l0_dt_bias, l0_a_log, l0_d_skip, l0_ssm_norm_g, l0_q_a_norm_g, l0_w_q_b, l0_kv_a_norm_g, l0_w_kv_b, l0_w_out, l0_ffn_pre_g, l0_ffn_post_g, l0_w_gate, l0_w_up, l0_w_down, l1_ada_w, l1_ada_b, l1_mix_pre_g, l1_mix_post_g, l1_w_in, l1_w_gk2, l1_b_gk, l1_gla_norm_g, l1_cmp_pos, l1_cmp_k_w1, l1_cmp_k_w2, l1_cmp_v_w1, l1_cmp_v_w2, l1_w_out, l1_ffn_pre_g, l1_ffn_post_g, l1_w_gate, l1_w_up, l1_w_down):
    bsz, seq, d = x.shape
    x2 = x.reshape(bsz * seq, d)

    def sublayers(x2, ada_w, ada_b, pre_m, post_m, mixer, w_out, pre_f, post_f, w_gate, w_up, w_down):
        shift_m, scale_m, gate_m, shift_f, scale_f, gate_f = _ada(c, ada_w, ada_b)
        a, b = mixer((x2, pre_m, scale_m, shift_m))
        ka = a.shape[1]
        x2 = _out_res(x2, a, b, w_out[:ka].astype(BF16), w_out[ka:].astype(BF16), post_m, gate_m, seq)
        return _ffn(x2, pre_f, scale_f, shift_f, w_gate.astype(BF16), w_up.astype(BF16), w_down.astype(BF16),
                    post_f, gate_f, seq)

    x2 = sublayers(
        x2, l0_ada_w, l0_ada_b, l0_mix_pre_g, l0_mix_post_g,
        lambda h: _mixer0_parts(h, positions, bsz, seq, l0_w_in, l0_conv_w, l0_conv_b, l0_dt_bias, l0_a_log,
                                l0_d_skip, l0_ssm_norm_g, l0_q_a_norm_g, l0_w_q_b, l0_kv_a_norm_g, l0_w_kv_b),
        l0_w_out, l0_ffn_pre_g, l0_ffn_post_g, l0_w_gate, l0_w_up, l0_w_down)
    x2 = sublayers(
        x2, l1_ada_w, l1_ada_b, l1_mix_pre_g, l1_mix_post_g,
        lambda h: _mixer1_parts(h, bsz, seq, l1_w_in, l1_w_gk2, l1_b_gk, l1_gla_norm_g, l1_cmp_pos,
                                l1_cmp_k_w1, l1_cmp_k_w2, l1_cmp_v_w1, l1_cmp_v_w2),
        l1_w_out, l1_ffn_pre_g, l1_ffn_post_g, l1_w_gate, l1_w_up, l1_w_down)
    return x2.reshape(bsz, seq, d)
```

```python
import functools

import numpy as np
import jax
import jax.numpy as jnp
from jax import lax
from jax.experimental import pallas as pl
from jax.experimental.pallas import tpu as pltpu

F32, BF16 = jnp.float32, jnp.bfloat16
HI = lax.Precision.HIGHEST
LANES = 128
VMEM_LIMIT = 48 * 1024 * 1024

NORM_EPS = 1e-6
NEG_INF = -1e30
LOG2E = 1.4426950408889634
MASK_DIST = 1e33
N_MOD = 6

SSD_HEADS, SSD_HEAD_DIM, SSD_STATE, SSD_GROUPS, SSD_CONV = 16, 64, 128, 2, 4
SSD_INNER = SSD_HEADS * SSD_HEAD_DIM
SSD_CONV_CH = SSD_INNER + 2 * SSD_GROUPS * SSD_STATE
SSD_CHUNK = 128

MLA_HEADS, MLA_Q_RANK, MLA_KV_RANK, MLA_NOPE, MLA_ROPE, MLA_V = 8, 384, 256, 64, 32, 64
ROPE_THETA = 10000.0

GLA_HEADS, GLA_DK, GLA_DV, GLA_GATE_RANK, GLA_GATE_NORM = 4, 128, 256, 16, 16.0
GLA_CHUNK = 128
GLA_SUB = 16

NSA_HEADS, NSA_GROUPS, NSA_HEAD_DIM = 8, 2, 64
NSA_HG = NSA_HEADS // NSA_GROUPS
NSA_CMP_BLOCK, NSA_CMP_STRIDE, NSA_CMP_HIDDEN = 32, 16, 256
NSA_SEL_BLOCK, NSA_N_SEL, NSA_WINDOW, NSA_FORCE = 64, 16, 512, 1e4
_SEL_SHIFT = NSA_SEL_BLOCK.bit_length() - 1
assert 1 << _SEL_SHIFT == NSA_SEL_BLOCK


def _cparams(sem):
    return pltpu.CompilerParams(dimension_semantics=sem, vmem_limit_bytes=VMEM_LIMIT)


def _dot(a, b, prec=None):
    return jnp.dot(a, b, preferred_element_type=F32, precision=prec)


def _dot_nt(a, b, prec=None):
    return lax.dot_general(a, b, (((1,), (1,)), ((), ())), preferred_element_type=F32, precision=prec)


def _split(x, n):
    parts = []
    for _ in range(n):
        p = x.astype(BF16)
        parts.append(p)
        x = x - p.astype(F32)
    return parts


def _dot_01_left(sel, x, n):
    return sum(_dot(sel, p) for p in _split(x, n))


def _dot_01_right(x, sel, n):
    return sum(_dot(p, sel) for p in _split(x, n))


def _dot3(a, b, nt=False):
    f = _dot_nt if nt else _dot
    (ah, al), (bh, bl) = _split(a, 2), _split(b, 2)
    return f(ah, bh) + (f(ah, bl) + f(al, bh))


def _silu(x):
    return x * (1.0 / (1.0 + jnp.exp(-x)))


def _softplus(x):
    return jnp.maximum(x, 0.0) + jnp.log1p(jnp.exp(-jnp.abs(x)))


def _rms(x, g):
    return x * lax.rsqrt(jnp.mean(x * x, axis=-1, keepdims=True) + NORM_EPS) * g


def _tril(n):
    return lax.broadcasted_iota(jnp.int32, (n, n), 0) >= lax.broadcasted_iota(jnp.int32, (n, n), 1)


def _alibi_table(n, groups):
    s = 2.0 ** (-8.0 * np.arange(1, n + 1) / n)
    return [[float(v) for v in row] for row in s.reshape(groups, n // groups)]


def _by_group(g, table, h):
    val = table[0][h]
    for gi in range(1, len(table)):
        val = jnp.where(g == gi, table[gi][h], val)
    return val


def _ada_kernel(c_ref, w_ref, b_ref, o_ref):
    o_ref[...] = _dot(_silu(c_ref[...]), w_ref[...], HI) + b_ref[...]


def _ada(c, w, b):
    bsz, d = c.shape
    n = w.shape[1]
    rows = 8
    cp = jnp.zeros((rows, d), F32).at[:bsz].set(c)
    tn = 1024
    out = pl.pallas_call(
        _ada_kernel,
        grid=(n // tn,),
        in_specs=[pl.BlockSpec((rows, d), lambda j: (0, 0)),
                  pl.BlockSpec((d, tn), lambda j: (0, j)),
                  pl.BlockSpec((1, tn), lambda j: (0, j))],
        out_specs=pl.BlockSpec((rows, tn), lambda j: (0, j)),
        out_shape=jax.ShapeDtypeStruct((rows, n), F32),
        compiler_params=_cparams(("arbitrary",)),
        name="ada",
    )(cp, w, b.reshape(1, n))
    return [m.reshape(bsz, 1, d) for m in jnp.split(out[:bsz], N_MOD, axis=-1)]


def _proj_in_kernel(x_ref, g_ref, sc_ref, sh_ref, w_ref, *o_refs, segs):
    h = (_rms(x_ref[...], g_ref[...]) * (1.0 + sc_ref[0]) + sh_ref[0]).astype(BF16)
    for (a, b, heads, _), o_ref in zip(segs, o_refs):
        res = _dot(h, w_ref[:, a:b])
        if heads:
            for g in range(heads):
                dh = (b - a) // heads
                o_ref[0, g] = res[:, g * dh:(g + 1) * dh].astype(o_ref.dtype)
        else:
            o_ref[...] = res.astype(o_ref.dtype)


def _proj_in(x2, g, scale, shift, w, segs, seq, tm=512):
    m, d = x2.shape
    n = w.shape[1]
    tpb = seq // tm
    return pl.pallas_call(
        functools.partial(_proj_in_kernel, segs=segs),
        grid=(m // tm,),
        in_specs=[pl.BlockSpec((tm, d), lambda i: (i, 0)),
                  pl.BlockSpec((1, d), lambda i: (0, 0)),
                  pl.BlockSpec((1, 1, d), lambda i: (i // tpb, 0, 0)),
                  pl.BlockSpec((1, 1, d), lambda i: (i // tpb, 0, 0)),
                  pl.BlockSpec((d, n), lambda i: (0, 0))],
        out_specs=[pl.BlockSpec((1, hd, tm, (b - a) // hd), lambda i: (i // tpb, 0, i % tpb, 0)) if hd else
                   pl.BlockSpec((tm, b - a), lambda i: (i, 0)) for a, b, hd, _ in segs],
        out_shape=[jax.ShapeDtypeStruct((m // seq, hd, seq, (b - a) // hd), dt) if hd else
                   jax.ShapeDtypeStruct((m, b - a), dt) for a, b, hd, dt in segs],
        compiler_params=_cparams(("arbitrary",)),
        name="proj_in",
    )(x2, g.reshape(1, d), scale, shift, w)


def _out_res_kernel(x_ref, a_ref, b_ref, wa_ref, wb_ref, g_ref, gate_ref, o_ref):
    y = _dot(a_ref[...], wa_ref[...]) + _dot(b_ref[...], wb_ref[...])
    o_ref[...] = x_ref[...] + gate_ref[0] * _rms(y, g_ref[...])


def _out_res(x2, a, b, wa, wb, g, gate, seq, tm=512):
    m, d = x2.shape
    ka, kb = a.shape[1], b.shape[1]
    tpb = seq // tm
    return pl.pallas_call(
        _out_res_kernel,
        grid=(m // tm,),
        in_specs=[pl.BlockSpec((tm, d), lambda i: (i, 0)),
                  pl.BlockSpec((tm, ka), lambda i: (i, 0)),
                  pl.BlockSpec((tm, kb), lambda i: (i, 0)),
                  pl.BlockSpec((ka, d), lambda i: (0, 0)),
                  pl.BlockSpec((kb, d), lambda i: (0, 0)),
                  pl.BlockSpec((1, d), lambda i: (0, 0)),
                  pl.BlockSpec((1, 1, d), lambda i: (i // tpb, 0, 0))],
        out_specs=pl.BlockSpec((tm, d), lambda i: (i, 0)),
        out_shape=jax.ShapeDtypeStruct((m, d), F32),
        compiler_params=_cparams(("arbitrary",)),
        name="out_res",
    )(x2, a, b, wa, wb, g.reshape(1, d), gate)


def _ffn_kernel(x_ref, gpre_ref, sc_ref, sh_ref, wg_ref, wu_ref, wd_ref, gpost_ref, gate_ref, o_ref,
                h_scr, acc_scr):
    j = pl.program_id(1)

    @pl.when(j == 0)
    def _():
        h_scr[...] = (_rms(x_ref[...], gpre_ref[...]) * (1.0 + sc_ref[0]) + sh_ref[0]).astype(BF16)
        acc_scr[...] = jnp.zeros_like(acc_scr)

    h = h_scr[...]
    act = (_silu(_dot(h, wg_ref[...])) * _dot(h, wu_ref[...])).astype(BF16)
    acc_scr[...] += _dot(act, wd_ref[...])

    @pl.when(j == pl.num_programs(1) - 1)
    def _():
        o_ref[...] = x_ref[...] + gate_ref[0] * _rms(acc_scr[...], gpost_ref[...])


def _ffn(x2, gpre, scale, shift, wg, wu, wd, gpost, gate, seq, tm=512, th=1408):
    m, d = x2.shape
    hid = wg.shape[1]
    tpb = seq // tm
    return pl.pallas_call(
        _ffn_kernel,
        grid=(m // tm, hid // th),
        in_specs=[pl.BlockSpec((tm, d), lambda i, j: (i, 0)),
                  pl.BlockSpec((1, d), lambda i, j: (0, 0)),
                  pl.BlockSpec((1, 1, d), lambda i, j: (i // tpb, 0, 0)),
                  pl.BlockSpec((1, 1, d), lambda i, j: (i // tpb, 0, 0)),
                  pl.BlockSpec((d, th), lambda i, j: (0, j)),
                  pl.BlockSpec((d, th), lambda i, j: (0, j)),
                  pl.BlockSpec((th, d), lambda i, j: (j, 0)),
                  pl.BlockSpec((1, d), lambda i, j: (0, 0)),
                  pl.BlockSpec((1, 1, d), lambda i, j: (i // tpb, 0, 0))],
        out_specs=pl.BlockSpec((tm, d), lambda i, j: (i, 0)),
        out_shape=jax.ShapeDtypeStruct((m, d), F32),
        scratch_shapes=[pltpu.VMEM((tm, d), BF16), pltpu.VMEM((tm, d), F32)],
        compiler_params=_cparams(("arbitrary", "arbitrary")),
        name="ffn",
    )(x2, gpre.reshape(1, d), scale, shift, wg, wu, wd, gpost.reshape(1, d), gate)


def _ssd_kernel(z_ref, xbc_ref, misc_ref, cw_ref, cb_ref, dtb_ref, alog_ref, dsk_ref, ng_ref, e_ref, o_ref,
                ext_scr, st_scr):
    L = SSD_CHUNK
    gsz = SSD_INNER // SSD_GROUPS
    hpg = SSD_HEADS // SSD_GROUPS
    pad = 8

    @pl.when(pl.program_id(1) == 0)
    def _():
        ext_scr[0:pad, :] = jnp.zeros((pad, SSD_CONV_CH), F32)
        st_scr[...] = jnp.zeros_like(st_scr)

    xt = xbc_ref[...]
    ext_scr[pad:pad + L, :] = xt
    acc = cb_ref[...] + cw_ref[0:1, :] * ext_scr[pad - 3:pad - 3 + L, :]
    for k in range(1, SSD_CONV):
        acc = acc + cw_ref[k:k + 1, :] * ext_scr[pad - 3 + k:pad - 3 + k + L, :]
    ext_scr[0:pad, :] = xt[L - pad:L, :]
    xbc = _silu(acc)
    xs = xbc[:, :SSD_INNER]

    e = e_ref[...]
    dt = _softplus(misc_ref[...] + dtb_ref[...])
    adt = dt * (-jnp.exp(alog_ref[...]))
    tril = _tril(L)
    a_cs = _dot_01_left(jnp.where(tril, 1.0, 0.0).astype(BF16), adt, 3)
    a_cs_t = a_cs.T
    ea = jnp.exp(a_cs)
    ea_e = _dot_01_right(ea, e, 2)
    dec_e = _dot_01_right(jnp.exp(a_cs[L - 1:L, :] - a_cs), e, 2)
    xd = xs * _dot_01_right(dt, e, 2)

    ys = []
    for g in range(SSD_GROUPS):
        bg = xbc[:, SSD_INNER + g * SSD_STATE:SSD_INNER + (g + 1) * SSD_STATE]
        cg = xbc[:, SSD_INNER + (SSD_GROUPS + g) * SSD_STATE:SSD_INNER + (SSD_GROUPS + g + 1) * SSD_STATE]
        bg16, cg16 = bg.astype(BF16), cg.astype(BF16)
        gmat = _dot_nt(cg16, bg16)
        cols = slice(g * gsz, (g + 1) * gsz)
        xdg = xd[:, cols]
        st = st_scr[g]
        y_off = _dot(cg16, st.astype(BF16)) * ea_e[:, cols]
        st_scr[g] = st * ea_e[L - 1:L, cols] + _dot(bg.T.astype(BF16), (xdg * dec_e[:, cols]).astype(BF16))
        yd = []
        for h in range(hpg):
            hh = g * hpg + h
            seg = a_cs[:, hh:hh + 1] - a_cs_t[hh:hh + 1, :]
            lmat = jnp.where(tril, jnp.exp(seg), 0.0)
            yd.append(_dot((gmat * lmat).astype(BF16),
                           xdg[:, h * SSD_HEAD_DIM:(h + 1) * SSD_HEAD_DIM].astype(BF16)))
        y = jnp.concatenate(yd, axis=-1) + y_off + dsk_ref[:, cols] * xs[:, cols]
        y = y * _silu(z_ref[:, cols].astype(F32))
        ys.append(_rms(y, ng_ref[:, cols]))
    o_ref[...] = jnp.concatenate(ys, axis=-1).astype(o_ref.dtype)


def _ssd(z, xbc, misc, conv_w, conv_b, dt_bias, a_log, d_skip, norm_g, bsz, seq):
    L = SSD_CHUNK
    nc = seq // L
    m = bsz * seq
    e = np.zeros((LANES, SSD_INNER), np.float32)
    for h in range(SSD_HEADS):
        e[h, h * SSD_HEAD_DIM:(h + 1) * SSD_HEAD_DIM] = 1.0
    pad128 = lambda v: jnp.zeros((1, LANES), F32).at[0, :v.shape[0]].set(v)
    row = lambda i, c: (i * nc + c, 0)
    const = lambda i, c: (0, 0)
    return pl.pallas_call(
        _ssd_kernel,
        grid=(bsz, nc),
        in_specs=[pl.BlockSpec((L, SSD_INNER), row),
                  pl.BlockSpec((L, SSD_CONV_CH), row),
                  pl.BlockSpec((L, LANES), row),
                  pl.BlockSpec((SSD_CONV, SSD_CONV_CH), const),
                  pl.BlockSpec((1, SSD_CONV_CH), const),
                  pl.BlockSpec((1, LANES), const),
                  pl.BlockSpec((1, LANES), const),
                  pl.BlockSpec((1, SSD_INNER), const),
                  pl.BlockSpec((1, SSD_INNER), const),
                  pl.BlockSpec((LANES, SSD_INNER), const)],
        out_specs=pl.BlockSpec((L, SSD_INNER), row),
        out_shape=jax.ShapeDtypeStruct((m, SSD_INNER), BF16),
        scratch_shapes=[pltpu.VMEM((L + 8, SSD_CONV_CH), F32),
                        pltpu.VMEM((SSD_GROUPS, SSD_STATE, SSD_INNER // SSD_GROUPS), F32)],
        compiler_params=_cparams(("arbitrary", "arbitrary")),
        name="ssd",
    )(z, xbc, misc, conv_w, conv_b.reshape(1, -1), pad128(dt_bias), pad128(a_log),
      jnp.repeat(d_skip, SSD_HEAD_DIM).reshape(1, -1), norm_g.reshape(1, -1), jnp.asarray(e, BF16))


MLA_D = LANES
MLA_HEADS_PER_STEP = 8
_R1 = MLA_NOPE
_R2 = MLA_NOPE + MLA_ROPE // 2


def _mla_prep_kernel(qa_ref, kva_ref, misc_ref, pos_ref, gq_ref, gkv_ref, wq_ref, wqs_ref, wk_ref, wv_ref,
                     pk_ref, pks_ref, invf_ref, sgn_ref, q_ref, k_ref, v_ref):
    ang = pos_ref[...].astype(F32) * invf_ref[...]
    cos = jnp.cos(ang)
    sin = jnp.sin(ang) * sgn_ref[...]
    nq = _rms(qa_ref[...].astype(F32), gq_ref[...]).astype(BF16)
    nkv = _rms(kva_ref[...].astype(F32), gkv_ref[...]).astype(BF16)
    misc = misc_ref[...]
    k_rot = _dot_01_right(misc, pk_ref[...], 3) * cos + _dot_01_right(misc, pks_ref[...], 3) * sin
    for h in range(MLA_HEADS):
        qh = _dot(nq, wq_ref[h]) * cos + _dot(nq, wqs_ref[h]) * sin
        q_ref[0, h] = (qh * ((MLA_NOPE + MLA_ROPE) ** -0.5 * LOG2E)).astype(q_ref.dtype)
        k_ref[0, h] = (_dot(nkv, wk_ref[h]) + k_rot).astype(k_ref.dtype)
        v_ref[0, h] = _dot(nkv, wv_ref[h]).astype(v_ref.dtype)


def _mla_prep(q_a, kv_a, misc, positions, gq, gkv, w_q_b, w_kv_b, bsz, seq, tm=512):
    m = bsz * seq
    half = MLA_ROPE // 2
    dq = MLA_NOPE + MLA_ROPE
    wq3 = w_q_b.reshape(MLA_Q_RANK, MLA_HEADS, dq).transpose(1, 0, 2)
    zq = jnp.zeros((MLA_HEADS, MLA_Q_RANK, MLA_D - dq), F32)
    wq = jnp.concatenate([wq3, zq], axis=-1).astype(BF16)
    wqs = jnp.concatenate([jnp.zeros((MLA_HEADS, MLA_Q_RANK, MLA_NOPE), F32), wq3[..., _R2:dq], wq3[..., _R1:_R2], zq],
                          axis=-1).astype(BF16)
    wkv3 = w_kv_b.reshape(MLA_KV_RANK, MLA_HEADS, MLA_NOPE + MLA_V).transpose(1, 0, 2)
    wk = jnp.concatenate([wkv3[..., :MLA_NOPE], jnp.zeros((MLA_HEADS, MLA_KV_RANK, MLA_D - MLA_NOPE), F32)],
                         axis=-1).astype(BF16)
    wv = wkv3[..., MLA_NOPE:].astype(BF16)
    pk = np.zeros((LANES, MLA_D), np.float32)
    pks = np.zeros((LANES, MLA_D), np.float32)
    invf = np.zeros((1, MLA_D), np.float32)
    sgn = np.zeros((1, MLA_D), np.float32)
    inv = (ROPE_THETA ** (-np.arange(0, MLA_ROPE, 2) / MLA_ROPE)).astype(np.float32)
    for i in range(half):
        pk[SSD_HEADS + i, _R1 + i] = 1.0
        pk[SSD_HEADS + half + i, _R2 + i] = 1.0
        pks[SSD_HEADS + half + i, _R1 + i] = 1.0
        pks[SSD_HEADS + i, _R2 + i] = 1.0
        invf[0, _R1 + i] = invf[0, _R2 + i] = inv[i]
        sgn[0, _R1 + i], sgn[0, _R2 + i] = -1.0, 1.0
    tpb = seq // tm
    c2 = lambda i: (0, 0)
    c3 = lambda i: (0, 0, 0)
    hm = lambda i: (i // tpb, 0, i % tpb, 0)
    return pl.pallas_call(
        _mla_prep_kernel,
        grid=(m // tm,),
        in_specs=[pl.BlockSpec((tm, MLA_Q_RANK), lambda i: (i, 0)),
                  pl.BlockSpec((tm, MLA_KV_RANK), lambda i: (i, 0)),
                  pl.BlockSpec((tm, LANES), lambda i: (i, 0)),
                  pl.BlockSpec((tm, 1), lambda i: (i, 0)),
                  pl.BlockSpec((1, MLA_Q_RANK), c2),
                  pl.BlockSpec((1, MLA_KV_RANK), c2),
                  pl.BlockSpec((MLA_HEADS, MLA_Q_RANK, MLA_D), c3),
                  pl.BlockSpec((MLA_HEADS, MLA_Q_RANK, MLA_D), c3),
                  pl.BlockSpec((MLA_HEADS, MLA_KV_RANK, MLA_D), c3),
                  pl.BlockSpec((MLA_HEADS, MLA_KV_RANK, MLA_V), c3),
                  pl.BlockSpec((LANES, MLA_D), c2),
                  pl.BlockSpec((LANES, MLA_D), c2),
                  pl.BlockSpec((1, MLA_D), c2),
                  pl.BlockSpec((1, MLA_D), c2)],
        out_specs=[pl.BlockSpec((1, MLA_HEADS, tm, MLA_D), hm),
                   pl.BlockSpec((1, MLA_HEADS, tm, MLA_D), hm),
                   pl.BlockSpec((1, MLA_HEADS, tm, MLA_V), hm)],
        out_shape=[jax.ShapeDtypeStruct((bsz, MLA_HEADS, seq, MLA_D), BF16),
                   jax.ShapeDtypeStruct((bsz, MLA_HEADS, seq, MLA_D), BF16),
                   jax.ShapeDtypeStruct((bsz, MLA_HEADS, seq, MLA_V), BF16)],
        compiler_params=_cparams(("arbitrary",)),
        name="mla_prep",
    )(q_a, kv_a, misc, positions.reshape(m, 1), gq.reshape(1, -1), gkv.reshape(1, -1), wq, wqs, wk, wv,
      jnp.asarray(pk, BF16), jnp.asarray(pks, BF16), jnp.asarray(invf), jnp.asarray(sgn))


_FIRST, _LAST, _MASKED = 1, 2, 4
SOFTMAX_ROWS = 64


def _flash_schedule(seq, tq, tk, window):
    qi_l, kb_l, fl_l = [], [], []
    for qi in range(seq // tq):
        q0, q1 = qi * tq, qi * tq + tq - 1
        first = q0 // tk
        lo = 0 if window is None else max(0, q0 - window + 1) // tk
        blocks = [first] + [b for b in range(lo, q1 // tk + 1) if b != first]
        for n, kb in enumerate(blocks):
            k0, k1 = kb * tk, kb * tk + tk - 1
            masked = k1 > q0 or (window is not None and q1 - k0 >= window)
            qi_l.append(qi)
            kb_l.append(kb)
            fl_l.append((_FIRST if n == 0 else 0) | (_LAST if n == len(blocks) - 1 else 0) | (_MASKED if masked else 0))
    return [jnp.asarray(np.asarray(t, np.int32)) for t in (qi_l, kb_l, fl_l)]


def _flash_kernel(qi_ref, kb_ref, fl_ref, *refs, hg, tq, tk, dv, slopes, window, has_sel, kv_per_head):
    if has_sel:
        q_ref, k_ref, v_ref, selb_ref, o_ref, m_scr, l_scr, alpha_scr, acc_scr, s_scr, p_scr, bias_scr = refs
    else:
        q_ref, k_ref, v_ref, o_ref, m_scr, l_scr, alpha_scr, acc_scr, s_scr, p_scr, bias_scr = refs
    g = pl.program_id(1)
    st = pl.program_id(2)
    qi, kb, fl = qi_ref[st], kb_ref[st], fl_ref[st]
    masked = (fl & _MASKED) != 0
    rows = hg * tq
    rc = SOFTMAX_ROWS

    @pl.when((fl & _FIRST) != 0)
    def _():
        m_scr[...] = jnp.full_like(m_scr, NEG_INF)
        l_scr[...] = jnp.zeros_like(l_scr)
        acc_scr[...] = jnp.zeros_like(acc_scr)

    def mask_bias():
        dist = (qi * tq + lax.broadcasted_iota(jnp.int32, (tq, tk), 0)) - \
               (kb * tk + lax.broadcasted_iota(jnp.int32, (tq, tk), 1))
        ok = dist >= 0
        if window is not None:
            ok = ok & (dist < window)
        return jnp.where(ok, 0.0, NEG_INF)

    def step(with_mask):
        with_bias = has_sel or with_mask
        if has_sel:
            blk = lax.broadcasted_iota(jnp.int32, (LANES, tk), 0)
            key = kb * tk + lax.broadcasted_iota(jnp.int32, (LANES, tk), 1)
            expand = jnp.where((key >> _SEL_SHIFT) == blk, 1.0, 0.0).astype(BF16)
            bias = _dot(selb_ref[0, 0], expand)
            bias_scr[...] = bias + mask_bias() if with_mask else bias
        elif with_mask:
            bias_scr[...] = mask_bias()
        kv = (lambda ref, h: ref[0, 0, h]) if kv_per_head else (lambda ref, h: ref[0, 0])
        for h in range(hg):
            s_scr[h * tq:(h + 1) * tq, :] = _dot_nt(q_ref[0, 0, h], kv(k_ref, h))
        for h in range(hg):
            if slopes is not None:
                rel = (kb * tk - qi * tq + lax.broadcasted_iota(jnp.int32, (1, tk), 1)).astype(F32)
                arow = (_by_group(g, slopes, h) * LOG2E) * rel
            for c in range(tq // rc):
                rs = slice(h * tq + c * rc, h * tq + (c + 1) * rc)
                s = s_scr[rs, :]
                if with_bias:
                    s = s + bias_scr[c * rc:(c + 1) * rc, :]
                if slopes is not None:
                    s = s + arow
                if with_bias or slopes is not None:
                    s_scr[rs, :] = s
                m_prev = m_scr[rs, :]
                m_next = jnp.maximum(m_prev, jnp.max(s, axis=1, keepdims=True))
                alpha_scr[rs, :] = jnp.exp2(m_prev - m_next)
                m_scr[rs, :] = m_next
        for h in range(hg):
            for c in range(tq // rc):
                rs = slice(h * tq + c * rc, h * tq + (c + 1) * rc)
                p = jnp.exp2(s_scr[rs, :] - jnp.tile(m_scr[rs, :], (1, tk // LANES)))
                alpha = alpha_scr[rs, :]
                l_scr[rs, :] = alpha * l_scr[rs, :] + jnp.sum(p, axis=1, keepdims=True)
                acc_scr[rs, :] = acc_scr[rs, :] * alpha[:, :dv]
                p_scr[rs, :] = p.astype(BF16)
        for h in range(hg):
            hs = slice(h * tq, (h + 1) * tq)
            acc_scr[hs, :] += _dot(p_scr[hs, :], kv(v_ref, h))

    pl.when(masked)(lambda: step(True))
    pl.when(jnp.logical_not(masked))(lambda: step(False))

    @pl.when((fl & _LAST) != 0)
    def _():
        out = acc_scr[...] / jnp.maximum(l_scr[:, :dv], 1e-30)
        o_ref[0] = jnp.concatenate([out[h * tq:(h + 1) * tq] for h in range(hg)], axis=-1).astype(o_ref.dtype)


def _flash(q, k, v, selb, *, tq, tk, slopes=None, window=None, out_dtype=F32):
    bsz, ng, hg, seq, d = q.shape
    dv = v.shape[-1]
    kv_per_head = k.ndim == 5
    sched = _flash_schedule(seq, tq, tk, window)
    rows = hg * tq
    if kv_per_head:
        kv_spec = lambda w: pl.BlockSpec((1, 1, hg, tk, w), lambda b, g, s, qi, kb, fl: (b, g, 0, kb[s], 0))
    else:
        kv_spec = lambda w: pl.BlockSpec((1, 1, tk, w), lambda b, g, s, qi, kb, fl: (b, g, kb[s], 0))
    in_specs = [pl.BlockSpec((1, 1, hg, tq, d), lambda b, g, s, qi, kb, fl: (b, g, 0, qi[s], 0)),
                kv_spec(d), kv_spec(dv)]
    args = [q, k, v]
    if selb is not None:
        in_specs.append(pl.BlockSpec((1, 1, tq, LANES), lambda b, g, s, qi, kb, fl: (b, g, qi[s], 0)))
        args.append(selb)
    return pl.pallas_call(
        functools.partial(_flash_kernel, hg=hg, tq=tq, tk=tk, dv=dv, slopes=slopes, window=window,
                          has_sel=selb is not None, kv_per_head=kv_per_head),
        grid_spec=pltpu.PrefetchScalarGridSpec(
            num_scalar_prefetch=3,
            grid=(bsz, ng, int(sched[0].shape[0])),
            in_specs=in_specs,
            out_specs=pl.BlockSpec((1, tq, hg * dv), lambda b, g, s, qi, kb, fl: (b, qi[s], g)),
            scratch_shapes=[pltpu.VMEM((rows, LANES), F32), pltpu.VMEM((rows, LANES), F32),
                            pltpu.VMEM((rows, LANES), F32),
                            pltpu.VMEM((rows, dv), F32), pltpu.VMEM((rows, tk), F32),
                            pltpu.VMEM((rows, tk), BF16), pltpu.VMEM((tq, tk), F32)]),
        out_shape=jax.ShapeDtypeStruct((bsz, seq, ng * hg * dv), out_dtype),
        compiler_params=_cparams(("arbitrary", "arbitrary", "arbitrary")),
        name="flash",
    )(*sched, *args)


def _gla_kernel(q_ref, k_ref, v_ref, gg_ref, misc_ref, w2_ref, bgk_ref, ng_ref, o_ref, st_scr):
    L = GLA_CHUNK

    @pl.when(pl.program_id(1) == 0)
    def _():
        st_scr[...] = jnp.zeros_like(st_scr)

    zg = _dot3(misc_ref[...], w2_ref[...]) + bgk_ref[...]
    log_a = -_softplus(-zg) * (1.0 / GLA_GATE_NORM)
    tril = _tril(L)
    bc_all = _dot_01_left(jnp.where(tril, 1.0, 0.0).astype(BF16), log_a, 3)
    row = lax.broadcasted_iota(jnp.int32, (L, GLA_DK), 0)
    for h in range(GLA_HEADS):
        kc, vc = slice(h * GLA_DK, (h + 1) * GLA_DK), slice(h * GLA_DV, (h + 1) * GLA_DV)
        bc = bc_all[:, kc]
        q = q_ref[:, kc] * (GLA_DK ** -0.5)
        k = k_ref[:, kc]
        v = v_ref[:, vc].astype(F32)
        st = st_scr[h]
        o = _dot_nt((q * jnp.exp(bc)).astype(BF16), st.astype(BF16))
        att = []
        for i in range(L // GLA_SUB):
            r0 = i * GLA_SUB
            ref = bc[r0:r0 + 1, :]
            qi = q[r0:r0 + GLA_SUB] * jnp.exp(bc[r0:r0 + GLA_SUB] - ref)
            ki = k * jnp.exp(jnp.where(row < r0 + GLA_SUB, ref - bc, 0.0))
            att.append(_dot_nt(qi.astype(BF16), ki.astype(BF16)))
        att = jnp.where(tril, jnp.concatenate(att, axis=0), 0.0)
        o = o + _dot(att.astype(BF16), v.astype(BF16))
        b_last = bc[L - 1:L, :]
        kd = (k * jnp.exp(b_last - bc)).astype(BF16)
        st_scr[h] = st * jnp.exp(b_last) + _dot(v.T.astype(BF16), kd)
        o_ref[:, vc] = (_rms(o, ng_ref[...]) * _silu(gg_ref[:, vc].astype(F32))).astype(o_ref.dtype)


def _gla(gq, gk, gv, gg, misc, w_gk2, b_gk, norm_g, bsz, seq):
    L = GLA_CHUNK
    nc = seq // L
    m = bsz * seq
    nk, nv = GLA_HEADS * GLA_DK, GLA_HEADS * GLA_DV
    w2 = jnp.zeros((LANES, nk), F32).at[:GLA_GATE_RANK].set(w_gk2)
    row = lambda b, c: (b * nc + c, 0)
    const = lambda b, c: (0, 0)
    return pl.pallas_call(
        _gla_kernel,
        grid=(bsz, nc),
        in_specs=[pl.BlockSpec((L, nk), row),
                  pl.BlockSpec((L, nk), row),
                  pl.BlockSpec((L, nv), row),
                  pl.BlockSpec((L, nv), row),
                  pl.BlockSpec((L, LANES), row),
                  pl.BlockSpec((LANES, nk), const),
                  pl.BlockSpec((1, nk), const),
                  pl.BlockSpec((1, GLA_DV), const)],
        out_specs=pl.BlockSpec((L, nv), row),
        out_shape=jax.ShapeDtypeStruct((m, nv), BF16),
        scratch_shapes=[pltpu.VMEM((GLA_HEADS, GLA_DV, GLA_DK), F32)],
        compiler_params=_cparams(("arbitrary", "arbitrary")),
        name="gla",
    )(gq, gk, gv, gg, misc, w2, b_gk.reshape(1, -1), norm_g.reshape(1, -1))


def _cmp_kernel(z_ref, pa_ref, pb_ref, wa_ref, wb_ref, w2_ref, o_ref, *, prec):
    z = z_ref[0]
    if prec is None:
        cast = lambda t: t.astype(BF16)
    else:
        cast = lambda t: t
    first = _dot(cast(z + pa_ref[...]), cast(wa_ref[...]), prec)
    second = _dot(cast(z + pb_ref[...]), cast(wb_ref[...]), prec)
    n = first.shape[0]
    hid = _silu(first + pltpu.roll(second, n - 1, 0))
    for g in range(NSA_GROUPS):
        o_ref[0, g] = _dot(cast(hid[:, g * NSA_CMP_HIDDEN:(g + 1) * NSA_CMP_HIDDEN]), cast(w2_ref[...]), prec)


def _compress(z, cmp_pos, w1, w2, bsz, seq, prec):
    nb = seq // NSA_CMP_STRIDE
    width = NSA_CMP_STRIDE * NSA_GROUPS * NSA_HEAD_DIM
    zr = z.reshape(bsz, nb, width)
    per = NSA_CMP_BLOCK // NSA_CMP_STRIDE
    eye = jnp.eye(NSA_GROUPS, dtype=F32)
    wbig = jnp.einsum("ldj,gh->lgdhj", w1.reshape(NSA_CMP_BLOCK, NSA_HEAD_DIM, NSA_CMP_HIDDEN), eye)
    wbig = wbig.reshape(per, width, NSA_GROUPS * NSA_CMP_HIDDEN)
    posb = jnp.broadcast_to(cmp_pos[:, None, :], (NSA_CMP_BLOCK, NSA_GROUPS, NSA_HEAD_DIM)).reshape(per, 1, width)
    assert per == 2
    c2 = lambda b: (0, 0)
    return pl.pallas_call(
        functools.partial(_cmp_kernel, prec=prec),
        grid=(bsz,),
        in_specs=[pl.BlockSpec((1, nb, width), lambda b: (b, 0, 0)),
                  pl.BlockSpec((1, width), c2), pl.BlockSpec((1, width), c2),
                  pl.BlockSpec((width, NSA_GROUPS * NSA_CMP_HIDDEN), c2),
                  pl.BlockSpec((width, NSA_GROUPS * NSA_CMP_HIDDEN), c2),
                  pl.BlockSpec((NSA_CMP_HIDDEN, NSA_HEAD_DIM), c2)],
        out_specs=pl.BlockSpec((1, NSA_GROUPS, nb, NSA_HEAD_DIM), lambda b: (b, 0, 0, 0)),
        out_shape=jax.ShapeDtypeStruct((bsz, NSA_GROUPS, nb, NSA_HEAD_DIM), F32),
        compiler_params=_cparams(("arbitrary",)),
        name="nsa_compress",
    )(zr, posb[0], posb[1], wbig[0], wbig[1], w2)


def _nsa_select_kernel(q_ref, kc_ref, vc_ref, ovt_ref, oc_ref, selb_ref, qs_ref, *, tq, n_sel, slopes, scale):
    g = pl.program_id(1)
    qi = pl.program_id(2)
    ncmp = kc_ref.shape[2]
    nslc = ovt_ref.shape[0]
    qpos_c = qi * tq + lax.broadcasted_iota(jnp.int32, (ncmp, tq), 1)
    cmp_end = lax.broadcasted_iota(jnp.int32, (ncmp, tq), 0) * NSA_CMP_STRIDE + (NSA_CMP_BLOCK - 1)
    dist = (qpos_c - cmp_end).astype(F32)
    dist = jnp.where(dist >= 0.0, dist, MASK_DIST)
    kc_hi, kc_lo = _split(kc_ref[0, 0], 2)
    vc = vc_ref[0, 0].astype(BF16)
    dh = NSA_HEAD_DIM
    o_c = []
    psum = jnp.zeros((ncmp, tq), F32)
    for h in range(NSA_HG):
        q = q_ref[:, h * dh:(h + 1) * dh] * (scale * LOG2E)
        qs_ref[0, 0, h] = q.astype(qs_ref.dtype)
        q_hi, q_lo = _split(q, 2)
        qk = _dot_nt(kc_hi, q_hi) + (_dot_nt(kc_hi, q_lo) + _dot_nt(kc_lo, q_hi))
        s = qk - (_by_group(g, slopes, h) * LOG2E) * dist
        mx = jnp.max(s, axis=0, keepdims=True)
        ex = jnp.exp2(s - mx)
        den = jnp.maximum(jnp.sum(ex, axis=0, keepdims=True), 1e-30)
        p = ex * jnp.where(mx > NEG_INF, 1.0 / den, 0.0)
        o_c.append(lax.dot_general(p.astype(BF16), vc, (((0,), (0,)), ((), ())), preferred_element_type=F32))
        psum = psum + p
    oc_ref[...] = jnp.concatenate(o_c, axis=-1)
    imp = _dot_01_left(ovt_ref[...], psum, 2)
    blk = lax.broadcasted_iota(jnp.int32, (nslc, tq), 0)
    qpos = qi * tq + lax.broadcasted_iota(jnp.int32, (nslc, tq), 1)
    forced = (blk == 0) | (blk == (qpos >> _SEL_SHIFT))
    avail = blk * NSA_SEL_BLOCK <= qpos
    imp = jnp.where(forced, NSA_FORCE, jnp.where(avail, imp, -1.0))
    blkf = blk.astype(F32)
    selb = jnp.full((nslc, tq), NEG_INF, F32)
    for _ in range(n_sel):
        mx = jnp.max(imp, axis=0, keepdims=True)
        first = jnp.min(jnp.where(imp == mx, blkf, float(nslc)), axis=0, keepdims=True)
        one = blkf == first
        selb = jnp.where(one, 0.0, selb)
        imp = jnp.where(one, -2.0, imp)
    selb_ref[0, 0] = selb.T.astype(selb_ref.dtype)


def _nsa_select(nq, k_cmp, v_cmp, bsz, seq, tq=512):
    ng, hg, dh = NSA_GROUPS, NSA_HG, NSA_HEAD_DIM
    ncmp = k_cmp.shape[2]
    nslc = seq // NSA_SEL_BLOCK
    n_sel = min(NSA_N_SEL, nslc)
    nt = seq // tq
    c_start = np.arange(ncmp) * NSA_CMP_STRIDE
    s_start = np.arange(nslc) * NSA_SEL_BLOCK
    ovt = ((c_start[None, :] < s_start[:, None] + NSA_SEL_BLOCK)
           & (c_start[None, :] + NSA_CMP_BLOCK > s_start[:, None])).astype(np.float32)
    ovt[:, (seq - NSA_CMP_BLOCK) // NSA_CMP_STRIDE + 1:] = 0.0
    return pl.pallas_call(
        functools.partial(_nsa_select_kernel, tq=tq, n_sel=n_sel, slopes=_alibi_table(NSA_HEADS, NSA_GROUPS),
                          scale=NSA_HEAD_DIM ** -0.5),
        grid=(bsz, ng, nt),
        in_specs=[pl.BlockSpec((tq, hg * dh), lambda b, g, i: (b * nt + i, g)),
                  pl.BlockSpec((1, 1, ncmp, dh), lambda b, g, i: (b, g, 0, 0)),
                  pl.BlockSpec((1, 1, ncmp, dh), lambda b, g, i: (b, g, 0, 0)),
                  pl.BlockSpec((nslc, ncmp), lambda b, g, i: (0, 0))],
        out_specs=[pl.BlockSpec((tq, hg * dh), lambda b, g, i: (b * nt + i, g)),
                   pl.BlockSpec((1, 1, tq, nslc), lambda b, g, i: (b, g, i, 0)),
                   pl.BlockSpec((1, 1, hg, tq, dh), lambda b, g, i: (b, g, 0, i, 0))],
        out_shape=[jax.ShapeDtypeStruct((bsz * seq, ng * hg * dh), F32),
                   jax.ShapeDtypeStruct((bsz, ng, seq, nslc), BF16),
                   jax.ShapeDtypeStruct((bsz, ng, hg, seq, dh), BF16)],
        compiler_params=_cparams(("arbitrary", "arbitrary", "arbitrary")),
        name="nsa_select",
    )(nq, k_cmp, v_cmp, jnp.asarray(ovt, BF16))


def _nsa_combine_kernel(oc_ref, os_ref, ow_ref, misc_ref, e_ref, o_ref):
    gates = 1.0 / (1.0 + jnp.exp(-misc_ref[...]))
    o = (_dot_01_right(gates, e_ref[0], 2) * oc_ref[...] + _dot_01_right(gates, e_ref[1], 2) * os_ref[...]
         + _dot_01_right(gates, e_ref[2], 2) * ow_ref[...])
    o_ref[...] = o.astype(o_ref.dtype)


def _nsa_combine(o_c, o_s, o_w, misc, tq=512):
    m, width = o_c.shape
    e = np.zeros((3, LANES, width), np.float32)
    for h in range(NSA_HEADS):
        for r in range(3):
            e[r, GLA_GATE_RANK + 3 * h + r, h * NSA_HEAD_DIM:(h + 1) * NSA_HEAD_DIM] = 1.0
    spec = pl.BlockSpec((tq, width), lambda i: (i, 0))
    return pl.pallas_call(
        _nsa_combine_kernel,
        grid=(m // tq,),
        in_specs=[spec, spec, spec, pl.BlockSpec((tq, LANES), lambda i: (i, 0)),
                  pl.BlockSpec((3, LANES, width), lambda i: (0, 0, 0))],
        out_specs=spec,
        out_shape=jax.ShapeDtypeStruct((m, width), BF16),
        compiler_params=_cparams(("arbitrary",)),
        name="nsa_combine",
    )(o_c, o_s, o_w, misc, jnp.asarray(e, BF16))


L0_SEGS = ((0, 1024, 0, BF16), (1024, 2560, 0, F32), (2560, 2688, 0, F32), (2688, 3072, 0, BF16),
           (3072, 3328, 0, BF16))
_G = NSA_GROUPS
L1_SEGS = ((0, 512, 0, F32), (512, 1024, 0, F32), (1024, 2048, 0, BF16), (2048, 3072, 0, BF16), (3072, 3584, 0, F32),
           (3584, 3712, 0, F32), (3712, 3840, 0, F32), (3840, 3968, _G, BF16), (3968, 4096, _G, BF16),
           (4096, 4224, _G, BF16), (4224, 4352, _G, BF16), (4352, 4480, 0, F32))


def _pack_w_in0(w):
    d = w.shape[0]
    a = SSD_INNER + SSD_CONV_CH
    dt = w[:, a:a + SSD_HEADS]
    qa = w[:, a + SSD_HEADS:a + SSD_HEADS + MLA_Q_RANK]
    kva = w[:, a + SSD_HEADS + MLA_Q_RANK:a + SSD_HEADS + MLA_Q_RANK + MLA_KV_RANK]
    kpe = w[:, a + SSD_HEADS + MLA_Q_RANK + MLA_KV_RANK:]
    pad = jnp.zeros((d, LANES - SSD_HEADS - MLA_ROPE), F32)
    return jnp.concatenate([w[:, :a], dt, kpe, pad, qa, kva], axis=1).astype(BF16)


def _pack_w_in1(w):
    d = w.shape[0]
    qk = 2 * GLA_HEADS * GLA_DK
    vv = GLA_HEADS * GLA_DV
    o = qk + vv
    glr = w[:, o:o + GLA_GATE_RANK]
    gg = w[:, o + GLA_GATE_RANK:o + GLA_GATE_RANK + vv]
    o2 = o + GLA_GATE_RANK + vv
    nsa = w[:, o2:o2 + NSA_HEADS * NSA_HEAD_DIM + 6 * NSA_GROUPS * NSA_HEAD_DIM]
    ngate = w[:, o2 + NSA_HEADS * NSA_HEAD_DIM + 6 * NSA_GROUPS * NSA_HEAD_DIM:]
    pad = jnp.zeros((d, LANES - GLA_GATE_RANK - 3 * NSA_HEADS), F32)
    return jnp.concatenate([w[:, :o], gg, nsa, glr, ngate, pad], axis=1).astype(BF16)


def _mixer0_parts(h_args, positions, bsz, seq, w_in, conv_w, conv_b, dt_bias, a_log, d_skip, ssm_norm_g,
                  q_a_norm_g, w_q_b, kv_a_norm_g, w_kv_b):
    z, xbc, misc, q_a, kv_a = _proj_in(*h_args, _pack_w_in0(w_in), L0_SEGS, seq)
    y_ssd = _ssd(z, xbc, misc, conv_w, conv_b, dt_bias, a_log, d_skip, ssm_norm_g, bsz, seq)
    q, k, v = _mla_prep(q_a, kv_a, misc, positions, q_a_norm_g, kv_a_norm_g, w_q_b, w_kv_b, bsz, seq)
    pair = lambda t: t.reshape(bsz, MLA_HEADS // MLA_HEADS_PER_STEP, MLA_HEADS_PER_STEP, seq, t.shape[-1])
    o = _flash(pair(q), pair(k), pair(v), None, tq=512, tk=512, out_dtype=BF16)
    return y_ssd, o.reshape(bsz * seq, MLA_HEADS * MLA_V)


def _mixer1_parts(h_args, bsz, seq, w_in, w_gk2, b_gk, gla_norm_g, cmp_pos, cmp_k_w1, cmp_k_w2, cmp_v_w1, cmp_v_w2):
    gq, gk, gv, gg, nq, kc, vc, ks, vs, kw, vw, misc = _proj_in(*h_args, _pack_w_in1(w_in), L1_SEGS, seq)
    o_gla = _gla(gq, gk, gv, gg, misc, w_gk2, b_gk, gla_norm_g, bsz, seq)
    k_cmp = _compress(kc, cmp_pos, cmp_k_w1, cmp_k_w2, bsz, seq, HI)
    v_cmp = _compress(vc, cmp_pos, cmp_v_w1, cmp_v_w2, bsz, seq, None)
    o_c, selb, qs = _nsa_select(nq, k_cmp, v_cmp, bsz, seq)
    if selb.shape[-1] < LANES:
        selb = jnp.pad(selb, ((0, 0), (0, 0), (0, 0), (0, LANES - selb.shape[-1])))
    slopes = _alibi_table(NSA_HEADS, NSA_GROUPS)
    width = NSA_HEADS * NSA_HEAD_DIM
    o_s = _flash(qs, ks, vs, selb, tq=512, tk=512, slopes=slopes).reshape(bsz * seq, width)
    o_w = _flash(qs, kw, vw, None, tq=NSA_WINDOW, tk=NSA_WINDOW, slopes=slopes, window=NSA_WINDOW)
    return o_gla, _nsa_combine(o_c, o_s, o_w.reshape(bsz * seq, width), misc)


def kernel(x, c, positions, l0_ada_w, l0_ada_b, l0_mix_pre_g, l0_mix_post_g, l0_w_in, l0_conv_w, l0_conv_b, l0_dt_bias, l0_a_log, l0_d_skip, l0_ssm_norm_g, l0_q_a_norm_g, l0_w_q_b, l0_kv_a_norm_g, l0_w_kv_b, l0_w_out, l0_ffn_pre_g, l0_ffn_post_g, l0_w_gate, l0_w_up, l0_w_down, l1_ada_w, l1_ada_b, l1_mix_pre_g, l1_mix_post_g, l1_w_in, l1_w_gk2, l1_b_gk, l1_gla_norm_g, l1_cmp_pos, l1_cmp_k_w1, l1_cmp_k_w2, l1_cmp_v_w1, l1_cmp_v_w2, l1_w_out, l1_ffn_pre_g, l1_ffn_post_g, l1_w_gate, l1_w_up, l1_w_down):
    bsz, seq, d = x.shape
    x2 = x.reshape(bsz * seq, d)

    def sublayers(x2, ada_w, ada_b, pre_m, post_m, mixer, w_out, pre_f, post_f, w_gate, w_up, w_down):
        shift_m, scale_m, gate_m, shift_f, scale_f, gate_f = _ada(c, ada_w, ada_b)
        a, b = mixer((x2, pre_m, scale_m, shift_m))
        ka = a.shape[1]
        x2 = _out_res(x2, a, b, w_out[:ka].astype(BF16), w_out[ka:].astype(BF16), post_m, gate_m, seq)
        return _ffn(x2, pre_f, scale_f, shift_f, w_gate.astype(BF16), w_up.astype(BF16), w_down.astype(BF16),
                    post_f, gate_f, seq)

    x2 = sublayers(
        x2, l0_ada_w, l0_ada_b, l0_mix_pre_g, l0_mix_post_g,
        lambda h: _mixer0_parts(h, positions, bsz, seq, l0_w_in, l0_conv_w, l0_conv_b, l0_dt_bias, l0_a_log,
                                l0_d_skip, l0_ssm_norm_g, l0_q_a_norm_g, l0_w_q_b, l0_kv_a_norm_g, l0_w_kv_b),
        l0_w_out, l0_ffn_pre_g, l0_ffn_post_g, l0_w_gate, l0_w_up, l0_w_down)
    x2 = sublayers(
        x2, l1_ada_w, l1_ada_b, l1_mix_pre_g, l1_mix_post_g,
        lambda h: _mixer1_parts(h, bsz, seq, l1_w_in, l1_w_gk2, l1_b_gk, l1_gla_norm_g, l1_cmp_pos,
                                l1_cmp_k_w1, l1_cmp_k_w2, l1_cmp_v_w1, l1_cmp_v_w2),
        l1_w_out, l1_ffn_pre_g, l1_ffn_post_g, l1_w_gate, l1_w_up, l1_w_down)
    return x2.reshape(bsz, seq, d)
```

```python
import functools

import numpy as np
import jax
import jax.numpy as jnp
from jax import lax
from jax.experimental import pallas as pl
from jax.experimental.pallas import tpu as pltpu

F32, BF16 = jnp.float32, jnp.bfloat16
HI = lax.Precision.HIGHEST
LANES = 128
VMEM_LIMIT = 48 * 1024 * 1024

NORM_EPS = 1e-6
NEG_INF = -1e30
LOG2E = 1.4426950408889634
MASK_DIST = 1e33
N_MOD = 6

SSD_HEADS, SSD_HEAD_DIM, SSD_STATE, SSD_GROUPS, SSD_CONV = 16, 64, 128, 2, 4
SSD_INNER = SSD_HEADS * SSD_HEAD_DIM
SSD_CONV_CH = SSD_INNER + 2 * SSD_GROUPS * SSD_STATE
SSD_CHUNK = 128

MLA_HEADS, MLA_Q_RANK, MLA_KV_RANK, MLA_NOPE, MLA_ROPE, MLA_V = 8, 384, 256, 64, 32, 64
ROPE_THETA = 10000.0

GLA_HEADS, GLA_DK, GLA_DV, GLA_GATE_RANK, GLA_GATE_NORM = 4, 128, 256, 16, 16.0
GLA_CHUNK = 128
GLA_SUB = 16

NSA_HEADS, NSA_GROUPS, NSA_HEAD_DIM = 8, 2, 64
NSA_HG = NSA_HEADS // NSA_GROUPS
NSA_CMP_BLOCK, NSA_CMP_STRIDE, NSA_CMP_HIDDEN = 32, 16, 256
NSA_SEL_BLOCK, NSA_N_SEL, NSA_WINDOW, NSA_FORCE = 64, 16, 512, 1e4
_SEL_SHIFT = NSA_SEL_BLOCK.bit_length() - 1
assert 1 << _SEL_SHIFT == NSA_SEL_BLOCK


def _cparams(sem):
    return pltpu.CompilerParams(dimension_semantics=sem, vmem_limit_bytes=VMEM_LIMIT)


def _dot(a, b, prec=None):
    return jnp.dot(a, b, preferred_element_type=F32, precision=prec)


def _dot_nt(a, b, prec=None):
    return lax.dot_general(a, b, (((1,), (1,)), ((), ())), preferred_element_type=F32, precision=prec)


def _split(x, n):
    parts = []
    for _ in range(n):
        p = x.astype(BF16)
        parts.append(p)
        x = x - p.astype(F32)
    return parts


def _dot_01_left(sel, x, n):
    return sum(_dot(sel, p) for p in _split(x, n))


def _dot_01_right(x, sel, n):
    return sum(_dot(p, sel) for p in _split(x, n))


def _dot3(a, b, nt=False):
    f = _dot_nt if nt else _dot
    (ah, al), (bh, bl) = _split(a, 2), _split(b, 2)
    return f(ah, bh) + (f(ah, bl) + f(al, bh))


def _silu(x):
    return x * (1.0 / (1.0 + jnp.exp(-x)))


def _softplus(x):
    return jnp.maximum(x, 0.0) + jnp.log1p(jnp.exp(-jnp.abs(x)))


def _rms(x, g):
    return x * lax.rsqrt(jnp.mean(x * x, axis=-1, keepdims=True) + NORM_EPS) * g


def _tril(n):
    return lax.broadcasted_iota(jnp.int32, (n, n), 0) >= lax.broadcasted_iota(jnp.int32, (n, n), 1)


def _alibi_table(n, groups):
    s = 2.0 ** (-8.0 * np.arange(1, n + 1) / n)
    return [[float(v) for v in row] for row in s.reshape(groups, n // groups)]


def _by_group(g, table, h):
    val = table[0][h]
    for gi in range(1, len(table)):
        val = jnp.where(g == gi, table[gi][h], val)
    return val


def _ada_kernel(c_ref, w_ref, b_ref, o_ref):
    o_ref[...] = _dot(_silu(c_ref[...]), w_ref[...], HI) + b_ref[...]


def _ada(c, w, b):
    bsz, d = c.shape
    n = w.shape[1]
    rows = 8
    cp = jnp.zeros((rows, d), F32).at[:bsz].set(c)
    tn = 1024
    out = pl.pallas_call(
        _ada_kernel,
        grid=(n // tn,),
        in_specs=[pl.BlockSpec((rows, d), lambda j: (0, 0)),
                  pl.BlockSpec((d, tn), lambda j: (0, j)),
                  pl.BlockSpec((1, tn), lambda j: (0, j))],
        out_specs=pl.BlockSpec((rows, tn), lambda j: (0, j)),
        out_shape=jax.ShapeDtypeStruct((rows, n), F32),
        compiler_params=_cparams(("arbitrary",)),
        name="ada",
    )(cp, w, b.reshape(1, n))
    return [m.reshape(bsz, 1, d) for m in jnp.split(out[:bsz], N_MOD, axis=-1)]


def _proj_in_kernel(x_ref, g_ref, sc_ref, sh_ref, w_ref, *refs, segs):
    feat_ref, o_refs = (refs[0], refs[1:]) if any(s[4] for s in segs) else (None, refs)
    h = (_rms(x_ref[...], g_ref[...]) * (1.0 + sc_ref[0]) + sh_ref[0]).astype(BF16)
    for (a, b, heads, _, aug), o_ref in zip(segs, o_refs):
        res = _dot(h, w_ref[:, a:b])
        if heads:
            for g in range(heads):
                dh = (b - a) // heads
                piece = res[:, g * dh:(g + 1) * dh]
                if aug:
                    piece = jnp.concatenate([piece, feat_ref[...].astype(F32)], axis=-1)
                o_ref[0, g] = piece.astype(o_ref.dtype)
        else:
            o_ref[...] = res.astype(o_ref.dtype)


def _key_features(tm):
    assert tm % AUG_TK == 0
    off = np.arange(tm) % AUG_TK
    f = np.zeros((tm, LANES - NSA_HEAD_DIM), np.float32)
    f[:, 0] = f[:, 2] = off >> 4
    f[:, 1] = f[:, 3] = off & 15
    f[np.arange(tm), AUG_SEL - AUG_POS + off // NSA_SEL_BLOCK] = 1.0
    return jnp.asarray(f, BF16)


def _proj_in(x2, g, scale, shift, w, segs, seq, tm=512):
    m, d = x2.shape
    n = w.shape[1]
    tpb = seq // tm
    has_feat = any(s[4] for s in segs)
    width = lambda a, b, hd, aug: LANES if aug else (b - a) // hd
    return pl.pallas_call(
        functools.partial(_proj_in_kernel, segs=segs),
        grid=(m // tm,),
        in_specs=[pl.BlockSpec((tm, d), lambda i: (i, 0)),
                  pl.BlockSpec((1, d), lambda i: (0, 0)),
                  pl.BlockSpec((1, 1, d), lambda i: (i // tpb, 0, 0)),
                  pl.BlockSpec((1, 1, d), lambda i: (i // tpb, 0, 0)),
                  pl.BlockSpec((d, n), lambda i: (0, 0))]
                 + ([pl.BlockSpec((tm, LANES - NSA_HEAD_DIM), lambda i: (0, 0))] if has_feat else []),
        out_specs=[pl.BlockSpec((1, hd, tm, width(a, b, hd, aug)), lambda i: (i // tpb, 0, i % tpb, 0)) if hd else
                   pl.BlockSpec((tm, b - a), lambda i: (i, 0)) for a, b, hd, _, aug in segs],
        out_shape=[jax.ShapeDtypeStruct((m // seq, hd, seq, width(a, b, hd, aug)), dt) if hd else
                   jax.ShapeDtypeStruct((m, b - a), dt) for a, b, hd, dt, aug in segs],
        compiler_params=_cparams(("arbitrary",)),
        name="proj_in",
    )(x2, g.reshape(1, d), scale, shift, w, *([_key_features(tm)] if has_feat else []))


def _out_res_kernel(x_ref, a_ref, b_ref, wa_ref, wb_ref, g_ref, gate_ref, o_ref):
    y = _dot(a_ref[...], wa_ref[...]) + _dot(b_ref[...], wb_ref[...])
    o_ref[...] = x_ref[...] + gate_ref[0] * _rms(y, g_ref[...])


def _out_res(x2, a, b, wa, wb, g, gate, seq, tm=512):
    m, d = x2.shape
    ka, kb = a.shape[1], b.shape[1]
    tpb = seq // tm
    return pl.pallas_call(
        _out_res_kernel,
        grid=(m // tm,),
        in_specs=[pl.BlockSpec((tm, d), lambda i: (i, 0)),
                  pl.BlockSpec((tm, ka), lambda i: (i, 0)),
                  pl.BlockSpec((tm, kb), lambda i: (i, 0)),
                  pl.BlockSpec((ka, d), lambda i: (0, 0)),
                  pl.BlockSpec((kb, d), lambda i: (0, 0)),
                  pl.BlockSpec((1, d), lambda i: (0, 0)),
                  pl.BlockSpec((1, 1, d), lambda i: (i // tpb, 0, 0))],
        out_specs=pl.BlockSpec((tm, d), lambda i: (i, 0)),
        out_shape=jax.ShapeDtypeStruct((m, d), F32),
        compiler_params=_cparams(("arbitrary",)),
        name="out_res",
    )(x2, a, b, wa, wb, g.reshape(1, d), gate)


def _ffn_kernel(x_ref, gpre_ref, sc_ref, sh_ref, wg_ref, wu_ref, wd_ref, gpost_ref, gate_ref, o_ref,
                h_scr, acc_scr):
    j = pl.program_id(1)

    @pl.when(j == 0)
    def _():
        h_scr[...] = (_rms(x_ref[...], gpre_ref[...]) * (1.0 + sc_ref[0]) + sh_ref[0]).astype(BF16)
        acc_scr[...] = jnp.zeros_like(acc_scr)

    h = h_scr[...]
    act = (_silu(_dot(h, wg_ref[...])) * _dot(h, wu_ref[...])).astype(BF16)
    acc_scr[...] += _dot(act, wd_ref[...])

    @pl.when(j == pl.num_programs(1) - 1)
    def _():
        o_ref[...] = x_ref[...] + gate_ref[0] * _rms(acc_scr[...], gpost_ref[...])


def _ffn(x2, gpre, scale, shift, wg, wu, wd, gpost, gate, seq, tm=512, th=1408):
    m, d = x2.shape
    hid = wg.shape[1]
    tpb = seq // tm
    return pl.pallas_call(
        _ffn_kernel,
        grid=(m // tm, hid // th),
        in_specs=[pl.BlockSpec((tm, d), lambda i, j: (i, 0)),
                  pl.BlockSpec((1, d), lambda i, j: (0, 0)),
                  pl.BlockSpec((1, 1, d), lambda i, j: (i // tpb, 0, 0)),
                  pl.BlockSpec((1, 1, d), lambda i, j: (i // tpb, 0, 0)),
                  pl.BlockSpec((d, th), lambda i, j: (0, j)),
                  pl.BlockSpec((d, th), lambda i, j: (0, j)),
                  pl.BlockSpec((th, d), lambda i, j: (j, 0)),
                  pl.BlockSpec((1, d), lambda i, j: (0, 0)),
                  pl.BlockSpec((1, 1, d), lambda i, j: (i // tpb, 0, 0))],
        out_specs=pl.BlockSpec((tm, d), lambda i, j: (i, 0)),
        out_shape=jax.ShapeDtypeStruct((m, d), F32),
        scratch_shapes=[pltpu.VMEM((tm, d), BF16), pltpu.VMEM((tm, d), F32)],
        compiler_params=_cparams(("arbitrary", "arbitrary")),
        name="ffn",
    )(x2, gpre.reshape(1, d), scale, shift, wg, wu, wd, gpost.reshape(1, d), gate)


def _ssd_kernel(z_ref, xbc_ref, misc_ref, cw_ref, cb_ref, dtb_ref, alog_ref, dsk_ref, ng_ref, e_ref, o_ref,
                ext_scr, st_scr):
    L = SSD_CHUNK
    gsz = SSD_INNER // SSD_GROUPS
    hpg = SSD_HEADS // SSD_GROUPS
    pad = 8

    @pl.when(pl.program_id(1) == 0)
    def _():
        ext_scr[0:pad, :] = jnp.zeros((pad, SSD_CONV_CH), F32)
        st_scr[...] = jnp.zeros_like(st_scr)

    xt = xbc_ref[...]
    ext_scr[pad:pad + L, :] = xt
    acc = cb_ref[...] + cw_ref[0:1, :] * ext_scr[pad - 3:pad - 3 + L, :]
    for k in range(1, SSD_CONV):
        acc = acc + cw_ref[k:k + 1, :] * ext_scr[pad - 3 + k:pad - 3 + k + L, :]
    ext_scr[0:pad, :] = xt[L - pad:L, :]
    xbc = _silu(acc)
    xs = xbc[:, :SSD_INNER]

    e = e_ref[...]
    dt = _softplus(misc_ref[...] + dtb_ref[...])
    adt = dt * (-jnp.exp(alog_ref[...]))
    tril = _tril(L)
    a_cs = _dot_01_left(jnp.where(tril, 1.0, 0.0).astype(BF16), adt, 3)
    a_cs_t = a_cs.T
    ea = jnp.exp(a_cs)
    ea_e = _dot_01_right(ea, e, 2)
    dec_e = _dot_01_right(jnp.exp(a_cs[L - 1:L, :] - a_cs), e, 2)
    xd = xs * _dot_01_right(dt, e, 2)

    ys = []
    for g in range(SSD_GROUPS):
        bg = xbc[:, SSD_INNER + g * SSD_STATE:SSD_INNER + (g + 1) * SSD_STATE]
        cg = xbc[:, SSD_INNER + (SSD_GROUPS + g) * SSD_STATE:SSD_INNER + (SSD_GROUPS + g + 1) * SSD_STATE]
        bg16, cg16 = bg.astype(BF16), cg.astype(BF16)
        gmat = _dot_nt(cg16, bg16)
        cols = slice(g * gsz, (g + 1) * gsz)
        xdg = xd[:, cols]
        st = st_scr[g]
        y_off = _dot(cg16, st.astype(BF16)) * ea_e[:, cols]
        st_scr[g] = st * ea_e[L - 1:L, cols] + _dot(bg.T.astype(BF16), (xdg * dec_e[:, cols]).astype(BF16))
        yd = []
        for h in range(hpg):
            hh = g * hpg + h
            seg = a_cs[:, hh:hh + 1] - a_cs_t[hh:hh + 1, :]
            lmat = jnp.where(tril, jnp.exp(seg), 0.0)
            yd.append(_dot((gmat * lmat).astype(BF16),
                           xdg[:, h * SSD_HEAD_DIM:(h + 1) * SSD_HEAD_DIM].astype(BF16)))
        y = jnp.concatenate(yd, axis=-1) + y_off + dsk_ref[:, cols] * xs[:, cols]
        y = y * _silu(z_ref[:, cols].astype(F32))
        ys.append(_rms(y, ng_ref[:, cols]))
    o_ref[...] = jnp.concatenate(ys, axis=-1).astype(o_ref.dtype)


def _ssd(z, xbc, misc, conv_w, conv_b, dt_bias, a_log, d_skip, norm_g, bsz, seq):
    L = SSD_CHUNK
    nc = seq // L
    m = bsz * seq
    e = np.zeros((LANES, SSD_INNER), np.float32)
    for h in range(SSD_HEADS):
        e[h, h * SSD_HEAD_DIM:(h + 1) * SSD_HEAD_DIM] = 1.0
    pad128 = lambda v: jnp.zeros((1, LANES), F32).at[0, :v.shape[0]].set(v)
    row = lambda i, c: (i * nc + c, 0)
    const = lambda i, c: (0, 0)
    return pl.pallas_call(
        _ssd_kernel,
        grid=(bsz, nc),
        in_specs=[pl.BlockSpec((L, SSD_INNER), row),
                  pl.BlockSpec((L, SSD_CONV_CH), row),
                  pl.BlockSpec((L, LANES), row),
                  pl.BlockSpec((SSD_CONV, SSD_CONV_CH), const),
                  pl.BlockSpec((1, SSD_CONV_CH), const),
                  pl.BlockSpec((1, LANES), const),
                  pl.BlockSpec((1, LANES), const),
                  pl.BlockSpec((1, SSD_INNER), const),
                  pl.BlockSpec((1, SSD_INNER), const),
                  pl.BlockSpec((LANES, SSD_INNER), const)],
        out_specs=pl.BlockSpec((L, SSD_INNER), row),
        out_shape=jax.ShapeDtypeStruct((m, SSD_INNER), BF16),
        scratch_shapes=[pltpu.VMEM((L + 8, SSD_CONV_CH), F32),
                        pltpu.VMEM((SSD_GROUPS, SSD_STATE, SSD_INNER // SSD_GROUPS), F32)],
        compiler_params=_cparams(("arbitrary", "arbitrary")),
        name="ssd",
    )(z, xbc, misc, conv_w, conv_b.reshape(1, -1), pad128(dt_bias), pad128(a_log),
      jnp.repeat(d_skip, SSD_HEAD_DIM).reshape(1, -1), norm_g.reshape(1, -1), jnp.asarray(e, BF16))


MLA_D = LANES
MLA_HEADS_PER_STEP = 8
_R1 = MLA_NOPE
_R2 = MLA_NOPE + MLA_ROPE // 2


def _mla_prep_kernel(qa_ref, kva_ref, misc_ref, pos_ref, gq_ref, gkv_ref, wq_ref, wqs_ref, wk_ref, wv_ref,
                     pk_ref, pks_ref, invf_ref, sgn_ref, q_ref, k_ref, v_ref):
    ang = pos_ref[...].astype(F32) * invf_ref[...]
    cos = jnp.cos(ang)
    sin = jnp.sin(ang) * sgn_ref[...]
    nq = _rms(qa_ref[...].astype(F32), gq_ref[...]).astype(BF16)
    nkv = _rms(kva_ref[...].astype(F32), gkv_ref[...]).astype(BF16)
    misc = misc_ref[...]
    k_rot = _dot_01_right(misc, pk_ref[...], 3) * cos + _dot_01_right(misc, pks_ref[...], 3) * sin
    for h in range(MLA_HEADS):
        qh = _dot(nq, wq_ref[h]) * cos + _dot(nq, wqs_ref[h]) * sin
        q_ref[0, h] = (qh * ((MLA_NOPE + MLA_ROPE) ** -0.5 * LOG2E)).astype(q_ref.dtype)
        k_ref[0, h] = (_dot(nkv, wk_ref[h]) + k_rot).astype(k_ref.dtype)
        v_ref[0, h] = _dot(nkv, wv_ref[h]).astype(v_ref.dtype)


def _mla_prep(q_a, kv_a, misc, positions, gq, gkv, w_q_b, w_kv_b, bsz, seq, tm=512):
    m = bsz * seq
    half = MLA_ROPE // 2
    dq = MLA_NOPE + MLA_ROPE
    wq3 = w_q_b.reshape(MLA_Q_RANK, MLA_HEADS, dq).transpose(1, 0, 2)
    zq = jnp.zeros((MLA_HEADS, MLA_Q_RANK, MLA_D - dq), F32)
    wq = jnp.concatenate([wq3, zq], axis=-1).astype(BF16)
    wqs = jnp.concatenate([jnp.zeros((MLA_HEADS, MLA_Q_RANK, MLA_NOPE), F32), wq3[..., _R2:dq], wq3[..., _R1:_R2], zq],
                          axis=-1).astype(BF16)
    wkv3 = w_kv_b.reshape(MLA_KV_RANK, MLA_HEADS, MLA_NOPE + MLA_V).transpose(1, 0, 2)
    wk = jnp.concatenate([wkv3[..., :MLA_NOPE], jnp.zeros((MLA_HEADS, MLA_KV_RANK, MLA_D - MLA_NOPE), F32)],
                         axis=-1).astype(BF16)
    wv = wkv3[..., MLA_NOPE:].astype(BF16)
    pk = np.zeros((LANES, MLA_D), np.float32)
    pks = np.zeros((LANES, MLA_D), np.float32)
    invf = np.zeros((1, MLA_D), np.float32)
    sgn = np.zeros((1, MLA_D), np.float32)
    inv = (ROPE_THETA ** (-np.arange(0, MLA_ROPE, 2) / MLA_ROPE)).astype(np.float32)
    for i in range(half):
        pk[SSD_HEADS + i, _R1 + i] = 1.0
        pk[SSD_HEADS + half + i, _R2 + i] = 1.0
        pks[SSD_HEADS + half + i, _R1 + i] = 1.0
        pks[SSD_HEADS + i, _R2 + i] = 1.0
        invf[0, _R1 + i] = invf[0, _R2 + i] = inv[i]
        sgn[0, _R1 + i], sgn[0, _R2 + i] = -1.0, 1.0
    tpb = seq // tm
    c2 = lambda i: (0, 0)
    c3 = lambda i: (0, 0, 0)
    hm = lambda i: (i // tpb, 0, i % tpb, 0)
    return pl.pallas_call(
        _mla_prep_kernel,
        grid=(m // tm,),
        in_specs=[pl.BlockSpec((tm, MLA_Q_RANK), lambda i: (i, 0)),
                  pl.BlockSpec((tm, MLA_KV_RANK), lambda i: (i, 0)),
                  pl.BlockSpec((tm, LANES), lambda i: (i, 0)),
                  pl.BlockSpec((tm, 1), lambda i: (i, 0)),
                  pl.BlockSpec((1, MLA_Q_RANK), c2),
                  pl.BlockSpec((1, MLA_KV_RANK), c2),
                  pl.BlockSpec((MLA_HEADS, MLA_Q_RANK, MLA_D), c3),
                  pl.BlockSpec((MLA_HEADS, MLA_Q_RANK, MLA_D), c3),
                  pl.BlockSpec((MLA_HEADS, MLA_KV_RANK, MLA_D), c3),
                  pl.BlockSpec((MLA_HEADS, MLA_KV_RANK, MLA_V), c3),
                  pl.BlockSpec((LANES, MLA_D), c2),
                  pl.BlockSpec((LANES, MLA_D), c2),
                  pl.BlockSpec((1, MLA_D), c2),
                  pl.BlockSpec((1, MLA_D), c2)],
        out_specs=[pl.BlockSpec((1, MLA_HEADS, tm, MLA_D), hm),
                   pl.BlockSpec((1, MLA_HEADS, tm, MLA_D), hm),
                   pl.BlockSpec((1, MLA_HEADS, tm, MLA_V), hm)],
        out_shape=[jax.ShapeDtypeStruct((bsz, MLA_HEADS, seq, MLA_D), BF16),
                   jax.ShapeDtypeStruct((bsz, MLA_HEADS, seq, MLA_D), BF16),
                   jax.ShapeDtypeStruct((bsz, MLA_HEADS, seq, MLA_V), BF16)],
        compiler_params=_cparams(("arbitrary",)),
        name="mla_prep",
    )(q_a, kv_a, misc, positions.reshape(m, 1), gq.reshape(1, -1), gkv.reshape(1, -1), wq, wqs, wk, wv,
      jnp.asarray(pk, BF16), jnp.asarray(pks, BF16), jnp.asarray(invf), jnp.asarray(sgn))


_FIRST, _LAST, _MASKED = 1, 2, 4
AUG_TK = 512
AUG_POS = NSA_HEAD_DIM
AUG_SEL = AUG_POS + 4
AUG_NSEL = AUG_TK // NSA_SEL_BLOCK
SOFTMAX_ROWS = 64


def _flash_schedule(seq, tq, tk, window):
    qi_l, kb_l, fl_l = [], [], []
    for qi in range(seq // tq):
        q0, q1 = qi * tq, qi * tq + tq - 1
        first = q0 // tk
        lo = 0 if window is None else max(0, q0 - window + 1) // tk
        blocks = [first] + [b for b in range(lo, q1 // tk + 1) if b != first]
        for n, kb in enumerate(blocks):
            k0, k1 = kb * tk, kb * tk + tk - 1
            masked = k1 > q0 or (window is not None and q1 - k0 >= window)
            qi_l.append(qi)
            kb_l.append(kb)
            fl_l.append((_FIRST if n == 0 else 0) | (_LAST if n == len(blocks) - 1 else 0) | (_MASKED if masked else 0))
    return [jnp.asarray(np.asarray(t, np.int32)) for t in (qi_l, kb_l, fl_l)]


def _flash_kernel(qi_ref, kb_ref, fl_ref, *refs, hg, tq, tk, dv, slopes, window, has_sel, kv_per_head):
    if has_sel:
        q_ref, k_ref, v_ref, selb_ref, o_ref, m_scr, l_scr, alpha_scr, acc_scr, s_scr, p_scr, bias_scr = refs
    else:
        q_ref, k_ref, v_ref, o_ref, m_scr, l_scr, alpha_scr, acc_scr, s_scr, p_scr, bias_scr = refs
    g = pl.program_id(1)
    st = pl.program_id(2)
    qi, kb, fl = qi_ref[st], kb_ref[st], fl_ref[st]
    masked = (fl & _MASKED) != 0
    rows = hg * tq
    rc = SOFTMAX_ROWS

    @pl.when((fl & _FIRST) != 0)
    def _():
        m_scr[...] = jnp.full_like(m_scr, NEG_INF)
        l_scr[...] = jnp.zeros_like(l_scr)
        acc_scr[...] = jnp.zeros_like(acc_scr)

    def mask_bias():
        dist = (qi * tq + lax.broadcasted_iota(jnp.int32, (tq, tk), 0)) - \
               (kb * tk + lax.broadcasted_iota(jnp.int32, (tq, tk), 1))
        ok = dist >= 0
        if window is not None:
            ok = ok & (dist < window)
        return jnp.where(ok, 0.0, NEG_INF)

    def step(with_mask):
        with_bias = with_mask
        if with_mask:
            bias_scr[...] = mask_bias()
        q_sel = None
        if has_sel:
            src = lax.broadcasted_iota(jnp.int32, (LANES, LANES), 0)
            dst = lax.broadcasted_iota(jnp.int32, (LANES, LANES), 1)
            place = jnp.where(src - kb * AUG_NSEL == dst - AUG_SEL,
                              jnp.where(dst >= AUG_SEL, jnp.where(dst < AUG_SEL + AUG_NSEL, 1.0, 0.0), 0.0), 0.0)
            q_sel = _dot(selb_ref[0, 0], place.astype(BF16)).astype(BF16)
        kv = (lambda ref, h: ref[0, 0, h]) if kv_per_head else (lambda ref, h: ref[0, 0])
        for h in range(hg):
            q = q_ref[0, 0, h] if q_sel is None else q_ref[0, 0, h] + q_sel
            s_scr[h * tq:(h + 1) * tq, :] = _dot_nt(q, kv(k_ref, h))
        shift = [0.0] * hg
        if slopes is not None:
            shift = [(_by_group(g, slopes, h) * LOG2E) * (kb * tk - qi * tq).astype(F32) for h in range(hg)]
        for h in range(hg):
            for c in range(tq // rc):
                rs = slice(h * tq + c * rc, h * tq + (c + 1) * rc)
                s = s_scr[rs, :]
                if with_bias:
                    s = s + bias_scr[c * rc:(c + 1) * rc, :]
                    s_scr[rs, :] = s
                m_prev = m_scr[rs, :]
                m_next = jnp.maximum(m_prev, jnp.max(s, axis=1, keepdims=True) + shift[h])
                alpha_scr[rs, :] = jnp.exp2(m_prev - m_next)
                m_scr[rs, :] = m_next
        for h in range(hg):
            for c in range(tq // rc):
                rs = slice(h * tq + c * rc, h * tq + (c + 1) * rc)
                p = jnp.exp2(s_scr[rs, :] - jnp.tile(m_scr[rs, :] - shift[h], (1, tk // LANES)))
                alpha = alpha_scr[rs, :]
                l_scr[rs, :] = alpha * l_scr[rs, :] + jnp.sum(p, axis=1, keepdims=True)
                acc_scr[rs, :] = acc_scr[rs, :] * alpha[:, :dv]
                p_scr[rs, :] = p.astype(BF16)
        for h in range(hg):
            hs = slice(h * tq, (h + 1) * tq)
            acc_scr[hs, :] += _dot(p_scr[hs, :], kv(v_ref, h))

    pl.when(masked)(lambda: step(True))
    pl.when(jnp.logical_not(masked))(lambda: step(False))

    @pl.when((fl & _LAST) != 0)
    def _():
        out = acc_scr[...] / jnp.maximum(l_scr[:, :dv], 1e-30)
        o_ref[0] = jnp.concatenate([out[h * tq:(h + 1) * tq] for h in range(hg)], axis=-1).astype(o_ref.dtype)


def _flash(q, k, v, selb, *, tq, tk, slopes=None, window=None, out_dtype=F32):
    bsz, ng, hg, seq, d = q.shape
    dv = v.shape[-1]
    kv_per_head = k.ndim == 5
    sched = _flash_schedule(seq, tq, tk, window)
    rows = hg * tq
    assert slopes is None or (tk == AUG_TK and d == LANES and k.shape[-1] == LANES)
    if kv_per_head:
        kv_spec = lambda w: pl.BlockSpec((1, 1, hg, tk, w), lambda b, g, s, qi, kb, fl: (b, g, 0, kb[s], 0))
    else:
        kv_spec = lambda w: pl.BlockSpec((1, 1, tk, w), lambda b, g, s, qi, kb, fl: (b, g, kb[s], 0))
    in_specs = [pl.BlockSpec((1, 1, hg, tq, d), lambda b, g, s, qi, kb, fl: (b, g, 0, qi[s], 0)),
                kv_spec(d), kv_spec(dv)]
    args = [q, k, v]
    if selb is not None:
        in_specs.append(pl.BlockSpec((1, 1, tq, LANES), lambda b, g, s, qi, kb, fl: (b, g, qi[s], 0)))
        args.append(selb)
    return pl.pallas_call(
        functools.partial(_flash_kernel, hg=hg, tq=tq, tk=tk, dv=dv, slopes=slopes, window=window,
                          has_sel=selb is not None, kv_per_head=kv_per_head),
        grid_spec=pltpu.PrefetchScalarGridSpec(
            num_scalar_prefetch=3,
            grid=(bsz, ng, int(sched[0].shape[0])),
            in_specs=in_specs,
            out_specs=pl.BlockSpec((1, tq, hg * dv), lambda b, g, s, qi, kb, fl: (b, qi[s], g)),
            scratch_shapes=[pltpu.VMEM((rows, LANES), F32), pltpu.VMEM((rows, LANES), F32),
                            pltpu.VMEM((rows, LANES), F32),
                            pltpu.VMEM((rows, dv), F32), pltpu.VMEM((rows, tk), F32),
                            pltpu.VMEM((rows, tk), BF16), pltpu.VMEM((tq, tk), F32)]),
        out_shape=jax.ShapeDtypeStruct((bsz, seq, ng * hg * dv), out_dtype),
        compiler_params=_cparams(("arbitrary", "arbitrary", "arbitrary")),
        name="flash",
    )(*sched, *args)


def _gla_kernel(q_ref, k_ref, v_ref, gg_ref, misc_ref, w2_ref, bgk_ref, ng_ref, o_ref, st_scr):
    L = GLA_CHUNK

    @pl.when(pl.program_id(1) == 0)
    def _():
        st_scr[...] = jnp.zeros_like(st_scr)

    zg = _dot3(misc_ref[...], w2_ref[...]) + bgk_ref[...]
    log_a = -_softplus(-zg) * (1.0 / GLA_GATE_NORM)
    tril = _tril(L)
    bc_all = _dot_01_left(jnp.where(tril, 1.0, 0.0).astype(BF16), log_a, 3)
    row = lax.broadcasted_iota(jnp.int32, (L, GLA_DK), 0)
    for h in range(GLA_HEADS):
        kc, vc = slice(h * GLA_DK, (h + 1) * GLA_DK), slice(h * GLA_DV, (h + 1) * GLA_DV)
        bc = bc_all[:, kc]
        q = q_ref[:, kc] * (GLA_DK ** -0.5)
        k = k_ref[:, kc]
        v = v_ref[:, vc].astype(F32)
        st = st_scr[h]
        o = _dot_nt((q * jnp.exp(bc)).astype(BF16), st.astype(BF16))
        att = []
        for i in range(L // GLA_SUB):
            r0 = i * GLA_SUB
            ref = bc[r0:r0 + 1, :]
            qi = q[r0:r0 + GLA_SUB] * jnp.exp(bc[r0:r0 + GLA_SUB] - ref)
            ki = k * jnp.exp(jnp.where(row < r0 + GLA_SUB, ref - bc, 0.0))
            att.append(_dot_nt(qi.astype(BF16), ki.astype(BF16)))
        att = jnp.where(tril, jnp.concatenate(att, axis=0), 0.0)
        o = o + _dot(att.astype(BF16), v.astype(BF16))
        b_last = bc[L - 1:L, :]
        kd = (k * jnp.exp(b_last - bc)).astype(BF16)
        st_scr[h] = st * jnp.exp(b_last) + _dot(v.T.astype(BF16), kd)
        o_ref[:, vc] = (_rms(o, ng_ref[...]) * _silu(gg_ref[:, vc].astype(F32))).astype(o_ref.dtype)


def _gla(gq, gk, gv, gg, misc, w_gk2, b_gk, norm_g, bsz, seq):
    L = GLA_CHUNK
    nc = seq // L
    m = bsz * seq
    nk, nv = GLA_HEADS * GLA_DK, GLA_HEADS * GLA_DV
    w2 = jnp.zeros((LANES, nk), F32).at[:GLA_GATE_RANK].set(w_gk2)
    row = lambda b, c: (b * nc + c, 0)
    const = lambda b, c: (0, 0)
    return pl.pallas_call(
        _gla_kernel,
        grid=(bsz, nc),
        in_specs=[pl.BlockSpec((L, nk), row),
                  pl.BlockSpec((L, nk), row),
                  pl.BlockSpec((L, nv), row),
                  pl.BlockSpec((L, nv), row),
                  pl.BlockSpec((L, LANES), row),
                  pl.BlockSpec((LANES, nk), const),
                  pl.BlockSpec((1, nk), const),
                  pl.BlockSpec((1, GLA_DV), const)],
        out_specs=pl.BlockSpec((L, nv), row),
        out_shape=jax.ShapeDtypeStruct((m, nv), BF16),
        scratch_shapes=[pltpu.VMEM((GLA_HEADS, GLA_DV, GLA_DK), F32)],
        compiler_params=_cparams(("arbitrary", "arbitrary")),
        name="gla",
    )(gq, gk, gv, gg, misc, w2, b_gk.reshape(1, -1), norm_g.reshape(1, -1))


def _cmp_kernel(z_ref, pa_ref, pb_ref, wa_ref, wb_ref, w2_ref, o_ref, *, prec):
    z = z_ref[0]
    if prec is None:
        cast = lambda t: t.astype(BF16)
    else:
        cast = lambda t: t
    first = _dot(cast(z + pa_ref[...]), cast(wa_ref[...]), prec)
    second = _dot(cast(z + pb_ref[...]), cast(wb_ref[...]), prec)
    n = first.shape[0]
    hid = _silu(first + pltpu.roll(second, n - 1, 0))
    for g in range(NSA_GROUPS):
        o_ref[0, g] = _dot(cast(hid[:, g * NSA_CMP_HIDDEN:(g + 1) * NSA_CMP_HIDDEN]), cast(w2_ref[...]), prec)


def _compress(z, cmp_pos, w1, w2, bsz, seq, prec):
    nb = seq // NSA_CMP_STRIDE
    width = NSA_CMP_STRIDE * NSA_GROUPS * NSA_HEAD_DIM
    zr = z.reshape(bsz, nb, width)
    per = NSA_CMP_BLOCK // NSA_CMP_STRIDE
    eye = jnp.eye(NSA_GROUPS, dtype=F32)
    wbig = jnp.einsum("ldj,gh->lgdhj", w1.reshape(NSA_CMP_BLOCK, NSA_HEAD_DIM, NSA_CMP_HIDDEN), eye)
    wbig = wbig.reshape(per, width, NSA_GROUPS * NSA_CMP_HIDDEN)
    posb = jnp.broadcast_to(cmp_pos[:, None, :], (NSA_CMP_BLOCK, NSA_GROUPS, NSA_HEAD_DIM)).reshape(per, 1, width)
    assert per == 2
    c2 = lambda b: (0, 0)
    return pl.pallas_call(
        functools.partial(_cmp_kernel, prec=prec),
        grid=(bsz,),
        in_specs=[pl.BlockSpec((1, nb, width), lambda b: (b, 0, 0)),
                  pl.BlockSpec((1, width), c2), pl.BlockSpec((1, width), c2),
                  pl.BlockSpec((width, NSA_GROUPS * NSA_CMP_HIDDEN), c2),
                  pl.BlockSpec((width, NSA_GROUPS * NSA_CMP_HIDDEN), c2),
                  pl.BlockSpec((NSA_CMP_HIDDEN, NSA_HEAD_DIM), c2)],
        out_specs=pl.BlockSpec((1, NSA_GROUPS, nb, NSA_HEAD_DIM), lambda b: (b, 0, 0, 0)),
        out_shape=jax.ShapeDtypeStruct((bsz, NSA_GROUPS, nb, NSA_HEAD_DIM), F32),
        compiler_params=_cparams(("arbitrary",)),
        name="nsa_compress",
    )(zr, posb[0], posb[1], wbig[0], wbig[1], w2)


def _nsa_select_kernel(q_ref, kc_ref, vc_ref, ovt_ref, oc_ref, selb_ref, qs_ref, *, tq, n_sel, slopes, scale):
    g = pl.program_id(1)
    qi = pl.program_id(2)
    ncmp = kc_ref.shape[2]
    nslc = ovt_ref.shape[0]
    qpos_c = qi * tq + lax.broadcasted_iota(jnp.int32, (ncmp, tq), 1)
    cmp_end = lax.broadcasted_iota(jnp.int32, (ncmp, tq), 0) * NSA_CMP_STRIDE + (NSA_CMP_BLOCK - 1)
    dist = (qpos_c - cmp_end).astype(F32)
    dist = jnp.where(dist >= 0.0, dist, MASK_DIST)
    kc_hi, kc_lo = _split(kc_ref[0, 0], 2)
    vc = vc_ref[0, 0].astype(BF16)
    dh = NSA_HEAD_DIM
    o_c = []
    psum = jnp.zeros((ncmp, tq), F32)
    for h in range(NSA_HG):
        q = q_ref[:, h * dh:(h + 1) * dh] * (scale * LOG2E)
        c = jnp.full((1, LANES - dh), _by_group(g, slopes, h) * LOG2E, F32)
        c_hi = c.astype(BF16).astype(F32)
        c_lo = c - c_hi
        lane = lax.broadcasted_iota(jnp.int32, (1, LANES - dh), 1)
        feat = jnp.where(lane == 0, 16.0 * c_hi, jnp.where(lane == 1, c_hi, jnp.where(
            lane == 2, 16.0 * c_lo, jnp.where(lane == 3, c_lo, 0.0))))
        qs_ref[0, 0, h] = jnp.concatenate([q, jnp.broadcast_to(feat, (tq, LANES - dh))], axis=-1).astype(qs_ref.dtype)
        q_hi, q_lo = _split(q, 2)
        qk = _dot_nt(kc_hi, q_hi) + (_dot_nt(kc_hi, q_lo) + _dot_nt(kc_lo, q_hi))
        s = qk - (_by_group(g, slopes, h) * LOG2E) * dist
        mx = jnp.max(s, axis=0, keepdims=True)
        ex = jnp.exp2(s - mx)
        den = jnp.maximum(jnp.sum(ex, axis=0, keepdims=True), 1e-30)
        p = ex * jnp.where(mx > NEG_INF, 1.0 / den, 0.0)
        o_c.append(lax.dot_general(p.astype(BF16), vc, (((0,), (0,)), ((), ())), preferred_element_type=F32))
        psum = psum + p
    oc_ref[...] = jnp.concatenate(o_c, axis=-1)
    imp = _dot_01_left(ovt_ref[...], psum, 2)
    blk = lax.broadcasted_iota(jnp.int32, (nslc, tq), 0)
    qpos = qi * tq + lax.broadcasted_iota(jnp.int32, (nslc, tq), 1)
    forced = (blk == 0) | (blk == (qpos >> _SEL_SHIFT))
    avail = blk * NSA_SEL_BLOCK <= qpos
    imp = jnp.where(forced, NSA_FORCE, jnp.where(avail, imp, -1.0))
    blkf = blk.astype(F32)
    selb = jnp.full((nslc, tq), NEG_INF, F32)
    for _ in range(n_sel):
        mx = jnp.max(imp, axis=0, keepdims=True)
        first = jnp.min(jnp.where(imp == mx, blkf, float(nslc)), axis=0, keepdims=True)
        one = blkf == first
        selb = jnp.where(one, 0.0, selb)
        imp = jnp.where(one, -2.0, imp)
    selb_ref[0, 0] = selb.T.astype(selb_ref.dtype)


def _nsa_select(nq, k_cmp, v_cmp, bsz, seq, tq=512):
    ng, hg, dh = NSA_GROUPS, NSA_HG, NSA_HEAD_DIM
    ncmp = k_cmp.shape[2]
    nslc = seq // NSA_SEL_BLOCK
    n_sel = min(NSA_N_SEL, nslc)
    nt = seq // tq
    c_start = np.arange(ncmp) * NSA_CMP_STRIDE
    s_start = np.arange(nslc) * NSA_SEL_BLOCK
    ovt = ((c_start[None, :] < s_start[:, None] + NSA_SEL_BLOCK)
           & (c_start[None, :] + NSA_CMP_BLOCK > s_start[:, None])).astype(np.float32)
    ovt[:, (seq - NSA_CMP_BLOCK) // NSA_CMP_STRIDE + 1:] = 0.0
    return pl.pallas_call(
        functools.partial(_nsa_select_kernel, tq=tq, n_sel=n_sel, slopes=_alibi_table(NSA_HEADS, NSA_GROUPS),
                          scale=NSA_HEAD_DIM ** -0.5),
        grid=(bsz, ng, nt),
        in_specs=[pl.BlockSpec((tq, hg * dh), lambda b, g, i: (b * nt + i, g)),
                  pl.BlockSpec((1, 1, ncmp, dh), lambda b, g, i: (b, g, 0, 0)),
                  pl.BlockSpec((1, 1, ncmp, dh), lambda b, g, i: (b, g, 0, 0)),
                  pl.BlockSpec((nslc, ncmp), lambda b, g, i: (0, 0))],
        out_specs=[pl.BlockSpec((tq, hg * dh), lambda b, g, i: (b * nt + i, g)),
                   pl.BlockSpec((1, 1, tq, nslc), lambda b, g, i: (b, g, i, 0)),
                   pl.BlockSpec((1, 1, hg, tq, LANES), lambda b, g, i: (b, g, 0, i, 0))],
        out_shape=[jax.ShapeDtypeStruct((bsz * seq, ng * hg * dh), F32),
                   jax.ShapeDtypeStruct((bsz, ng, seq, nslc), BF16),
                   jax.ShapeDtypeStruct((bsz, ng, hg, seq, LANES), BF16)],
        compiler_params=_cparams(("arbitrary", "arbitrary", "arbitrary")),
        name="nsa_select",
    )(nq, k_cmp, v_cmp, jnp.asarray(ovt, BF16))


def _nsa_combine_kernel(oc_ref, os_ref, ow_ref, misc_ref, e_ref, o_ref):
    gates = 1.0 / (1.0 + jnp.exp(-misc_ref[...]))
    o = (_dot_01_right(gates, e_ref[0], 2) * oc_ref[...] + _dot_01_right(gates, e_ref[1], 2) * os_ref[...]
         + _dot_01_right(gates, e_ref[2], 2) * ow_ref[...])
    o_ref[...] = o.astype(o_ref.dtype)


def _nsa_combine(o_c, o_s, o_w, misc, tq=512):
    m, width = o_c.shape
    e = np.zeros((3, LANES, width), np.float32)
    for h in range(NSA_HEADS):
        for r in range(3):
            e[r, GLA_GATE_RANK + 3 * h + r, h * NSA_HEAD_DIM:(h + 1) * NSA_HEAD_DIM] = 1.0
    spec = pl.BlockSpec((tq, width), lambda i: (i, 0))
    return pl.pallas_call(
        _nsa_combine_kernel,
        grid=(m // tq,),
        in_specs=[spec, spec, spec, pl.BlockSpec((tq, LANES), lambda i: (i, 0)),
                  pl.BlockSpec((3, LANES, width), lambda i: (0, 0, 0))],
        out_specs=spec,
        out_shape=jax.ShapeDtypeStruct((m, width), BF16),
        compiler_params=_cparams(("arbitrary",)),
        name="nsa_combine",
    )(o_c, o_s, o_w, misc, jnp.asarray(e, BF16))


L0_SEGS = ((0, 1024, 0, BF16, False), (1024, 2560, 0, F32, False), (2560, 2688, 0, F32, False),
           (2688, 3072, 0, BF16, False), (3072, 3328, 0, BF16, False))
_G = NSA_GROUPS
L1_SEGS = ((0, 512, 0, F32, False), (512, 1024, 0, F32, False), (1024, 2048, 0, BF16, False),
           (2048, 3072, 0, BF16, False), (3072, 3584, 0, F32, False), (3584, 3712, 0, F32, False),
           (3712, 3840, 0, F32, False), (3840, 3968, _G, BF16, True), (3968, 4096, _G, BF16, False),
           (4096, 4224, _G, BF16, True), (4224, 4352, _G, BF16, False), (4352, 4480, 0, F32, False))


def _pack_w_in0(w):
    d = w.shape[0]
    a = SSD_INNER + SSD_CONV_CH
    dt = w[:, a:a + SSD_HEADS]
    qa = w[:, a + SSD_HEADS:a + SSD_HEADS + MLA_Q_RANK]
    kva = w[:, a + SSD_HEADS + MLA_Q_RANK:a + SSD_HEADS + MLA_Q_RANK + MLA_KV_RANK]
    kpe = w[:, a + SSD_HEADS + MLA_Q_RANK + MLA_KV_RANK:]
    pad = jnp.zeros((d, LANES - SSD_HEADS - MLA_ROPE), F32)
    return jnp.concatenate([w[:, :a], dt, kpe, pad, qa, kva], axis=1).astype(BF16)


def _pack_w_in1(w):
    d = w.shape[0]
    qk = 2 * GLA_HEADS * GLA_DK
    vv = GLA_HEADS * GLA_DV
    o = qk + vv
    glr = w[:, o:o + GLA_GATE_RANK]
    gg = w[:, o + GLA_GATE_RANK:o + GLA_GATE_RANK + vv]
    o2 = o + GLA_GATE_RANK + vv
    nsa = w[:, o2:o2 + NSA_HEADS * NSA_HEAD_DIM + 6 * NSA_GROUPS * NSA_HEAD_DIM]
    ngate = w[:, o2 + NSA_HEADS * NSA_HEAD_DIM + 6 * NSA_GROUPS * NSA_HEAD_DIM:]
    pad = jnp.zeros((d, LANES - GLA_GATE_RANK - 3 * NSA_HEADS), F32)
    return jnp.concatenate([w[:, :o], gg, nsa, glr, ngate, pad], axis=1).astype(BF16)


def _mixer0_parts(h_args, positions, bsz, seq, w_in, conv_w, conv_b, dt_bias, a_log, d_skip, ssm_norm_g,
                  q_a_norm_g, w_q_b, kv_a_norm_g, w_kv_b):
    z, xbc, misc, q_a, kv_a = _proj_in(*h_args, _pack_w_in0(w_in), L0_SEGS, seq)
    y_ssd = _ssd(z, xbc, misc, conv_w, conv_b, dt_bias, a_log, d_skip, ssm_norm_g, bsz, seq)
    q, k, v = _mla_prep(q_a, kv_a, misc, positions, q_a_norm_g, kv_a_norm_g, w_q_b, w_kv_b, bsz, seq)
    pair = lambda t: t.reshape(bsz, MLA_HEADS // MLA_HEADS_PER_STEP, MLA_HEADS_PER_STEP, seq, t.shape[-1])
    o = _flash(pair(q), pair(k), pair(v), None, tq=512, tk=512, out_dtype=BF16)
    return y_ssd, o.reshape(bsz * seq, MLA_HEADS * MLA_V)


def _mixer1_parts(h_args, bsz, seq, w_in, w_gk2, b_gk, gla_norm_g, cmp_pos, cmp_k_w1, cmp_k_w2, cmp_v_w1, cmp_v_w2):
    gq, gk, gv, gg, nq, kc, vc, ks, vs, kw, vw, misc = _proj_in(*h_args, _pack_w_in1(w_in), L1_SEGS, seq)
    o_gla = _gla(gq, gk, gv, gg, misc, w_gk2, b_gk, gla_norm_g, bsz, seq)
    k_cmp = _compress(kc, cmp_pos, cmp_k_w1, cmp_k_w2, bsz, seq, HI)
    v_cmp = _compress(vc, cmp_pos, cmp_v_w1, cmp_v_w2, bsz, seq, None)
    o_c, selb, qs = _nsa_select(nq, k_cmp, v_cmp, bsz, seq)
    if selb.shape[-1] < LANES:
        selb = jnp.pad(selb, ((0, 0), (0, 0), (0, 0), (0, LANES - selb.shape[-1])))
    slopes = _alibi_table(NSA_HEADS, NSA_GROUPS)
    width = NSA_HEADS * NSA_HEAD_DIM
    o_s = _flash(qs, ks, vs, selb, tq=512, tk=512, slopes=slopes).reshape(bsz * seq, width)
    o_w = _flash(qs, kw, vw, None, tq=NSA_WINDOW, tk=NSA_WINDOW, slopes=slopes, window=NSA_WINDOW)
    return o_gla, _nsa_combine(o_c, o_s, o_w.reshape(bsz * seq, width), misc)


def kernel(x, c, positions, l0_ada_w, l0_ada_b, l0_mix_pre_g, l0_mix_post_g, l0_w_in, l0_conv_w, l0_conv_b, l0_dt_bias, l0_a_log, l0_d_skip, l0_ssm_norm_g, l0_q_a_norm_g, l0_w_q_b, l0_kv_a_norm_g, l0_w_kv_b, l0_w_out, l0_ffn_pre_g, l0_ffn_post_g, l0_w_gate, l0_w_up, l0_w_down, l1_ada_w, l1_ada_b, l1_mix_pre_g, l1_mix_post_g, l1_w_in, l1_w_gk2, l1_b_gk, l1_gla_norm_g, l1_cmp_pos, l1_cmp_k_w1, l1_cmp_k_w2, l1_cmp_v_w1, l1_cmp_v_w2, l1_w_out, l1_ffn_pre_g, l1_ffn_post_g, l1_w_gate, l1_w_up, l1_w_down):
    bsz, seq, d = x.shape
    x2 = x.reshape(bsz * seq, d)

    def sublayers(x2, ada_w, ada_b, pre_m, post_m, mixer, w_out, pre_f, post_f, w_gate, w_up, w_down):
        shift_m, scale_m, gate_m, shift_f, scale_f, gate_f = _ada(c, ada_w, ada_b)
        a, b = mixer((x2, pre_m, scale_m, shift_m))
        ka = a.shape[1]
        x2 = _out_res(x2, a, b, w_out[:ka].astype(BF16), w_out[ka:].astype(BF16), post_m, gate_m, seq)
        return _ffn(x2, pre_f, scale_f, shift_f, w_gate.astype(BF16), w_up.astype(BF16), w_down.astype(BF16),
                    post_f, gate_f, seq)

    x2 = sublayers(
        x2, l0_ada_w, l0_ada_b, l0_mix_pre_g, l0_mix_post_g,
        lambda h: _mixer0_parts(h, positions, bsz, seq, l0_w_in, l0_conv_w, l0_conv_b, l0_dt_bias, l0_a_log,
                                l0_d_skip, l0_ssm_norm_g, l0_q_a_norm_g, l0_w_q_b, l0_kv_a_norm_g, l0_w_kv_b),
        l0_w_out, l0_ffn_pre_g, l0_ffn_post_g, l0_w_gate, l0_w_up, l0_w_down)
    x2 = sublayers(
        x2, l1_ada_w, l1_ada_b, l1_mix_pre_g, l1_mix_post_g,
        lambda h: _mixer1_parts(h, bsz, seq, l1_w_in, l1_w_gk2, l1_b_gk, l1_gla_norm_g, l1_cmp_pos,
                                l1_cmp_k_w1, l1_cmp_k_w2, l1_cmp_v_w1, l1_cmp_v_w2),
        l1_w_out, l1_ffn_pre_g, l1_ffn_post_g, l1_w_gate, l1_w_up, l1_w_down)
    return x2.reshape(bsz, seq, d)
```

```python
import functools

import numpy as np
import jax
import jax.numpy as jnp
from jax import lax
from jax.experimental import pallas as pl
from jax.experimental.pallas import tpu as pltpu

F32, BF16 = jnp.float32, jnp.bfloat16
HI = lax.Precision.HIGHEST
LANES = 128
VMEM_LIMIT = 48 * 1024 * 1024

NORM_EPS = 1e-6
NEG_INF = -1e30
LOG2E = 1.4426950408889634
MASK_DIST = 1e33
N_MOD = 6

SSD_HEADS, SSD_HEAD_DIM, SSD_STATE, SSD_GROUPS, SSD_CONV = 16, 64, 128, 2, 4
SSD_INNER = SSD_HEADS * SSD_HEAD_DIM
SSD_CONV_CH = SSD_INNER + 2 * SSD_GROUPS * SSD_STATE
SSD_CHUNK = 128

MLA_HEADS, MLA_Q_RANK, MLA_KV_RANK, MLA_NOPE, MLA_ROPE, MLA_V = 8, 384, 256, 64, 32, 64
ROPE_THETA = 10000.0

GLA_HEADS, GLA_DK, GLA_DV, GLA_GATE_RANK, GLA_GATE_NORM = 4, 128, 256, 16, 16.0
GLA_CHUNK = 128
GLA_SUB = 16

NSA_HEADS, NSA_GROUPS, NSA_HEAD_DIM = 8, 2, 64
NSA_HG = NSA_HEADS // NSA_GROUPS
NSA_CMP_BLOCK, NSA_CMP_STRIDE, NSA_CMP_HIDDEN = 32, 16, 256
NSA_SEL_BLOCK, NSA_N_SEL, NSA_WINDOW, NSA_FORCE = 64, 16, 512, 1e4
_SEL_SHIFT = NSA_SEL_BLOCK.bit_length() - 1
assert 1 << _SEL_SHIFT == NSA_SEL_BLOCK


def _cparams(sem):
    return pltpu.CompilerParams(dimension_semantics=sem, vmem_limit_bytes=VMEM_LIMIT)


def _dot(a, b, prec=None):
    return jnp.dot(a, b, preferred_element_type=F32, precision=prec)


def _dot_nt(a, b, prec=None):
    return lax.dot_general(a, b, (((1,), (1,)), ((), ())), preferred_element_type=F32, precision=prec)


def _split(x, n):
    parts = []
    for _ in range(n):
        p = x.astype(BF16)
        parts.append(p)
        x = x - p.astype(F32)
    return parts


def _dot_01_left(sel, x, n):
    return sum(_dot(sel, p) for p in _split(x, n))


def _dot_01_right(x, sel, n):
    return sum(_dot(p, sel) for p in _split(x, n))


def _dot3(a, b, nt=False):
    f = _dot_nt if nt else _dot
    (ah, al), (bh, bl) = _split(a, 2), _split(b, 2)
    return f(ah, bh) + (f(ah, bl) + f(al, bh))


def _silu(x):
    return x * (1.0 / (1.0 + jnp.exp(-x)))


def _softplus(x):
    return jnp.maximum(x, 0.0) + jnp.log1p(jnp.exp(-jnp.abs(x)))


def _rms(x, g):
    return x * lax.rsqrt(jnp.mean(x * x, axis=-1, keepdims=True) + NORM_EPS) * g


def _tril(n):
    return lax.broadcasted_iota(jnp.int32, (n, n), 0) >= lax.broadcasted_iota(jnp.int32, (n, n), 1)


def _alibi_table(n, groups):
    s = 2.0 ** (-8.0 * np.arange(1, n + 1) / n)
    return [[float(v) for v in row] for row in s.reshape(groups, n // groups)]


def _by_group(g, table, h):
    val = table[0][h]
    for gi in range(1, len(table)):
        val = jnp.where(g == gi, table[gi][h], val)
    return val


def _ada_kernel(c_ref, w_ref, b_ref, o_ref):
    o_ref[...] = _dot(_silu(c_ref[...]), w_ref[...], HI) + b_ref[...]


def _ada(c, w, b):
    bsz, d = c.shape
    n = w.shape[1]
    rows = 8
    cp = jnp.zeros((rows, d), F32).at[:bsz].set(c)
    tn = 1024
    out = pl.pallas_call(
        _ada_kernel,
        grid=(n // tn,),
        in_specs=[pl.BlockSpec((rows, d), lambda j: (0, 0)),
                  pl.BlockSpec((d, tn), lambda j: (0, j)),
                  pl.BlockSpec((1, tn), lambda j: (0, j))],
        out_specs=pl.BlockSpec((rows, tn), lambda j: (0, j)),
        out_shape=jax.ShapeDtypeStruct((rows, n), F32),
        compiler_params=_cparams(("arbitrary",)),
        name="ada",
    )(cp, w, b.reshape(1, n))
    return [m.reshape(bsz, 1, d) for m in jnp.split(out[:bsz], N_MOD, axis=-1)]


def _proj_in_kernel(x_ref, g_ref, sc_ref, sh_ref, w_ref, *refs, segs):
    feat_ref, o_refs = (refs[0], refs[1:]) if any(s[4] for s in segs) else (None, refs)
    h = (_rms(x_ref[...], g_ref[...]) * (1.0 + sc_ref[0]) + sh_ref[0]).astype(BF16)
    for (a, b, heads, _, aug), o_ref in zip(segs, o_refs):
        res = _dot(h, w_ref[:, a:b])
        if heads:
            for g in range(heads):
                dh = (b - a) // heads
                piece = res[:, g * dh:(g + 1) * dh]
                if aug:
                    piece = jnp.concatenate([piece, feat_ref[aug - 1].astype(F32)], axis=-1)
                o_ref[0, g] = piece.astype(o_ref.dtype)
        else:
            o_ref[...] = res.astype(o_ref.dtype)


def _key_features(tm):
    assert tm % AUG_TK == 0
    off = np.arange(tm) % AUG_TK
    f = np.zeros((tm, LANES - NSA_HEAD_DIM), np.float32)
    f[:, 0] = f[:, 2] = off >> 4
    f[:, 1] = f[:, 3] = off & 15
    f[np.arange(tm), AUG_SEL - AUG_POS + off // NSA_SEL_BLOCK] = 1.0
    ones = np.zeros_like(f)
    ones[:, 0] = 1.0
    return jnp.asarray(np.stack([f, ones]), BF16)


def _proj_in(x2, g, scale, shift, w, segs, seq, tm=512):
    m, d = x2.shape
    n = w.shape[1]
    tpb = seq // tm
    has_feat = any(s[4] for s in segs)
    width = lambda a, b, hd, aug: LANES if aug else (b - a) // hd
    return pl.pallas_call(
        functools.partial(_proj_in_kernel, segs=segs),
        grid=(m // tm,),
        in_specs=[pl.BlockSpec((tm, d), lambda i: (i, 0)),
                  pl.BlockSpec((1, d), lambda i: (0, 0)),
                  pl.BlockSpec((1, 1, d), lambda i: (i // tpb, 0, 0)),
                  pl.BlockSpec((1, 1, d), lambda i: (i // tpb, 0, 0)),
                  pl.BlockSpec((d, n), lambda i: (0, 0))]
                 + ([pl.BlockSpec((2, tm, LANES - NSA_HEAD_DIM), lambda i: (0, 0, 0))] if has_feat else []),
        out_specs=[pl.BlockSpec((1, hd, tm, width(a, b, hd, aug)), lambda i: (i // tpb, 0, i % tpb, 0)) if hd else
                   pl.BlockSpec((tm, b - a), lambda i: (i, 0)) for a, b, hd, _, aug in segs],
        out_shape=[jax.ShapeDtypeStruct((m // seq, hd, seq, width(a, b, hd, aug)), dt) if hd else
                   jax.ShapeDtypeStruct((m, b - a), dt) for a, b, hd, dt, aug in segs],
        compiler_params=_cparams(("arbitrary",)),
        name="proj_in",
    )(x2, g.reshape(1, d), scale, shift, w, *([_key_features(tm)] if has_feat else []))


def _out_res_kernel(x_ref, a_ref, b_ref, wa_ref, wb_ref, g_ref, gate_ref, o_ref):
    y = _dot(a_ref[...], wa_ref[...]) + _dot(b_ref[...], wb_ref[...])
    o_ref[...] = x_ref[...] + gate_ref[0] * _rms(y, g_ref[...])


def _out_res(x2, a, b, wa, wb, g, gate, seq, tm=512):
    m, d = x2.shape
    ka, kb = a.shape[1], b.shape[1]
    tpb = seq // tm
    return pl.pallas_call(
        _out_res_kernel,
        grid=(m // tm,),
        in_specs=[pl.BlockSpec((tm, d), lambda i: (i, 0)),
                  pl.BlockSpec((tm, ka), lambda i: (i, 0)),
                  pl.BlockSpec((tm, kb), lambda i: (i, 0)),
                  pl.BlockSpec((ka, d), lambda i: (0, 0)),
                  pl.BlockSpec((kb, d), lambda i: (0, 0)),
                  pl.BlockSpec((1, d), lambda i: (0, 0)),
                  pl.BlockSpec((1, 1, d), lambda i: (i // tpb, 0, 0))],
        out_specs=pl.BlockSpec((tm, d), lambda i: (i, 0)),
        out_shape=jax.ShapeDtypeStruct((m, d), F32),
        compiler_params=_cparams(("arbitrary",)),
        name="out_res",
    )(x2, a, b, wa, wb, g.reshape(1, d), gate)


def _ffn_kernel(x_ref, gpre_ref, sc_ref, sh_ref, wg_ref, wu_ref, wd_ref, gpost_ref, gate_ref, o_ref,
                h_scr, acc_scr):
    j = pl.program_id(1)

    @pl.when(j == 0)
    def _():
        h_scr[...] = (_rms(x_ref[...], gpre_ref[...]) * (1.0 + sc_ref[0]) + sh_ref[0]).astype(BF16)
        acc_scr[...] = jnp.zeros_like(acc_scr)

    h = h_scr[...]
    act = (_silu(_dot(h, wg_ref[...])) * _dot(h, wu_ref[...])).astype(BF16)
    acc_scr[...] += _dot(act, wd_ref[...])

    @pl.when(j == pl.num_programs(1) - 1)
    def _():
        o_ref[...] = x_ref[...] + gate_ref[0] * _rms(acc_scr[...], gpost_ref[...])


def _ffn(x2, gpre, scale, shift, wg, wu, wd, gpost, gate, seq, tm=512, th=1408):
    m, d = x2.shape
    hid = wg.shape[1]
    tpb = seq // tm
    return pl.pallas_call(
        _ffn_kernel,
        grid=(m // tm, hid // th),
        in_specs=[pl.BlockSpec((tm, d), lambda i, j: (i, 0)),
                  pl.BlockSpec((1, d), lambda i, j: (0, 0)),
                  pl.BlockSpec((1, 1, d), lambda i, j: (i // tpb, 0, 0)),
                  pl.BlockSpec((1, 1, d), lambda i, j: (i // tpb, 0, 0)),
                  pl.BlockSpec((d, th), lambda i, j: (0, j)),
                  pl.BlockSpec((d, th), lambda i, j: (0, j)),
                  pl.BlockSpec((th, d), lambda i, j: (j, 0)),
                  pl.BlockSpec((1, d), lambda i, j: (0, 0)),
                  pl.BlockSpec((1, 1, d), lambda i, j: (i // tpb, 0, 0))],
        out_specs=pl.BlockSpec((tm, d), lambda i, j: (i, 0)),
        out_shape=jax.ShapeDtypeStruct((m, d), F32),
        scratch_shapes=[pltpu.VMEM((tm, d), BF16), pltpu.VMEM((tm, d), F32)],
        compiler_params=_cparams(("arbitrary", "arbitrary")),
        name="ffn",
    )(x2, gpre.reshape(1, d), scale, shift, wg, wu, wd, gpost.reshape(1, d), gate)


def _ssd_kernel(z_ref, xbc_ref, misc_ref, cw_ref, cb_ref, dtb_ref, alog_ref, dsk_ref, ng_ref, e_ref, o_ref,
                ext_scr, st_scr):
    L = SSD_CHUNK
    gsz = SSD_INNER // SSD_GROUPS
    hpg = SSD_HEADS // SSD_GROUPS
    pad = 8

    @pl.when(pl.program_id(1) == 0)
    def _():
        ext_scr[0:pad, :] = jnp.zeros((pad, SSD_CONV_CH), F32)
        st_scr[...] = jnp.zeros_like(st_scr)

    xt = xbc_ref[...]
    ext_scr[pad:pad + L, :] = xt
    acc = cb_ref[...] + cw_ref[0:1, :] * ext_scr[pad - 3:pad - 3 + L, :]
    for k in range(1, SSD_CONV):
        acc = acc + cw_ref[k:k + 1, :] * ext_scr[pad - 3 + k:pad - 3 + k + L, :]
    ext_scr[0:pad, :] = xt[L - pad:L, :]
    xbc = _silu(acc)
    xs = xbc[:, :SSD_INNER]

    e = e_ref[...]
    dt = _softplus(misc_ref[...] + dtb_ref[...])
    adt = dt * (-jnp.exp(alog_ref[...]))
    tril = _tril(L)
    a_cs = _dot_01_left(jnp.where(tril, 1.0, 0.0).astype(BF16), adt, 3)
    a_cs_t = a_cs.T
    ea = jnp.exp(a_cs)
    ea_e = _dot_01_right(ea, e, 2)
    dec_e = _dot_01_right(jnp.exp(a_cs[L - 1:L, :] - a_cs), e, 2)
    xd = xs * _dot_01_right(dt, e, 2)

    ys = []
    for g in range(SSD_GROUPS):
        bg = xbc[:, SSD_INNER + g * SSD_STATE:SSD_INNER + (g + 1) * SSD_STATE]
        cg = xbc[:, SSD_INNER + (SSD_GROUPS + g) * SSD_STATE:SSD_INNER + (SSD_GROUPS + g + 1) * SSD_STATE]
        bg16, cg16 = bg.astype(BF16), cg.astype(BF16)
        gmat = _dot_nt(cg16, bg16)
        cols = slice(g * gsz, (g + 1) * gsz)
        xdg = xd[:, cols]
        st = st_scr[g]
        y_off = _dot(cg16, st.astype(BF16)) * ea_e[:, cols]
        st_scr[g] = st * ea_e[L - 1:L, cols] + _dot(bg.T.astype(BF16), (xdg * dec_e[:, cols]).astype(BF16))
        yd = []
        for h in range(hpg):
            hh = g * hpg + h
            seg = a_cs[:, hh:hh + 1] - a_cs_t[hh:hh + 1, :]
            lmat = jnp.where(tril, jnp.exp(seg), 0.0)
            yd.append(_dot((gmat * lmat).astype(BF16),
                           xdg[:, h * SSD_HEAD_DIM:(h + 1) * SSD_HEAD_DIM].astype(BF16)))
        y = jnp.concatenate(yd, axis=-1) + y_off + dsk_ref[:, cols] * xs[:, cols]
        y = y * _silu(z_ref[:, cols].astype(F32))
        ys.append(_rms(y, ng_ref[:, cols]))
    o_ref[...] = jnp.concatenate(ys, axis=-1).astype(o_ref.dtype)


def _ssd(z, xbc, misc, conv_w, conv_b, dt_bias, a_log, d_skip, norm_g, bsz, seq):
    L = SSD_CHUNK
    nc = seq // L
    m = bsz * seq
    e = np.zeros((LANES, SSD_INNER), np.float32)
    for h in range(SSD_HEADS):
        e[h, h * SSD_HEAD_DIM:(h + 1) * SSD_HEAD_DIM] = 1.0
    pad128 = lambda v: jnp.zeros((1, LANES), F32).at[0, :v.shape[0]].set(v)
    row = lambda i, c: (i * nc + c, 0)
    const = lambda i, c: (0, 0)
    return pl.pallas_call(
        _ssd_kernel,
        grid=(bsz, nc),
        in_specs=[pl.BlockSpec((L, SSD_INNER), row),
                  pl.BlockSpec((L, SSD_CONV_CH), row),
                  pl.BlockSpec((L, LANES), row),
                  pl.BlockSpec((SSD_CONV, SSD_CONV_CH), const),
                  pl.BlockSpec((1, SSD_CONV_CH), const),
                  pl.BlockSpec((1, LANES), const),
                  pl.BlockSpec((1, LANES), const),
                  pl.BlockSpec((1, SSD_INNER), const),
                  pl.BlockSpec((1, SSD_INNER), const),
                  pl.BlockSpec((LANES, SSD_INNER), const)],
        out_specs=pl.BlockSpec((L, SSD_INNER), row),
        out_shape=jax.ShapeDtypeStruct((m, SSD_INNER), BF16),
        scratch_shapes=[pltpu.VMEM((L + 8, SSD_CONV_CH), F32),
                        pltpu.VMEM((SSD_GROUPS, SSD_STATE, SSD_INNER // SSD_GROUPS), F32)],
        compiler_params=_cparams(("arbitrary", "arbitrary")),
        name="ssd",
    )(z, xbc, misc, conv_w, conv_b.reshape(1, -1), pad128(dt_bias), pad128(a_log),
      jnp.repeat(d_skip, SSD_HEAD_DIM).reshape(1, -1), norm_g.reshape(1, -1), jnp.asarray(e, BF16))


MLA_D = LANES
MLA_HEADS_PER_STEP = 8
_R1 = MLA_NOPE
_R2 = MLA_NOPE + MLA_ROPE // 2


def _mla_prep_kernel(qa_ref, kva_ref, misc_ref, pos_ref, gq_ref, gkv_ref, wq_ref, wqs_ref, wk_ref, wv_ref,
                     pk_ref, pks_ref, invf_ref, sgn_ref, q_ref, k_ref, v_ref):
    ang = pos_ref[...].astype(F32) * invf_ref[...]
    cos = jnp.cos(ang)
    sin = jnp.sin(ang) * sgn_ref[...]
    nq = _rms(qa_ref[...].astype(F32), gq_ref[...]).astype(BF16)
    nkv = _rms(kva_ref[...].astype(F32), gkv_ref[...]).astype(BF16)
    misc = misc_ref[...]
    k_rot = _dot_01_right(misc, pk_ref[...], 3) * cos + _dot_01_right(misc, pks_ref[...], 3) * sin
    for h in range(MLA_HEADS):
        qh = _dot(nq, wq_ref[h]) * cos + _dot(nq, wqs_ref[h]) * sin
        q_ref[0, h] = (qh * ((MLA_NOPE + MLA_ROPE) ** -0.5 * LOG2E)).astype(q_ref.dtype)
        k_ref[0, h] = (_dot(nkv, wk_ref[h]) + k_rot).astype(k_ref.dtype)
        v_ref[0, h] = _dot(nkv, wv_ref[h]).astype(v_ref.dtype)


def _mla_prep(q_a, kv_a, misc, positions, gq, gkv, w_q_b, w_kv_b, bsz, seq, tm=512):
    m = bsz * seq
    half = MLA_ROPE // 2
    dq = MLA_NOPE + MLA_ROPE
    wq3 = w_q_b.reshape(MLA_Q_RANK, MLA_HEADS, dq).transpose(1, 0, 2)
    zq = jnp.zeros((MLA_HEADS, MLA_Q_RANK, MLA_D - dq), F32)
    wq = jnp.concatenate([wq3, zq], axis=-1).astype(BF16)
    wqs = jnp.concatenate([jnp.zeros((MLA_HEADS, MLA_Q_RANK, MLA_NOPE), F32), wq3[..., _R2:dq], wq3[..., _R1:_R2], zq],
                          axis=-1).astype(BF16)
    wkv3 = w_kv_b.reshape(MLA_KV_RANK, MLA_HEADS, MLA_NOPE + MLA_V).transpose(1, 0, 2)
    wk = jnp.concatenate([wkv3[..., :MLA_NOPE], jnp.zeros((MLA_HEADS, MLA_KV_RANK, MLA_D - MLA_NOPE), F32)],
                         axis=-1).astype(BF16)
    wv = wkv3[..., MLA_NOPE:].astype(BF16)
    pk = np.zeros((LANES, MLA_D), np.float32)
    pks = np.zeros((LANES, MLA_D), np.float32)
    invf = np.zeros((1, MLA_D), np.float32)
    sgn = np.zeros((1, MLA_D), np.float32)
    inv = (ROPE_THETA ** (-np.arange(0, MLA_ROPE, 2) / MLA_ROPE)).astype(np.float32)
    for i in range(half):
        pk[SSD_HEADS + i, _R1 + i] = 1.0
        pk[SSD_HEADS + half + i, _R2 + i] = 1.0
        pks[SSD_HEADS + half + i, _R1 + i] = 1.0
        pks[SSD_HEADS + i, _R2 + i] = 1.0
        invf[0, _R1 + i] = invf[0, _R2 + i] = inv[i]
        sgn[0, _R1 + i], sgn[0, _R2 + i] = -1.0, 1.0
    tpb = seq // tm
    c2 = lambda i: (0, 0)
    c3 = lambda i: (0, 0, 0)
    hm = lambda i: (i // tpb, 0, i % tpb, 0)
    return pl.pallas_call(
        _mla_prep_kernel,
        grid=(m // tm,),
        in_specs=[pl.BlockSpec((tm, MLA_Q_RANK), lambda i: (i, 0)),
                  pl.BlockSpec((tm, MLA_KV_RANK), lambda i: (i, 0)),
                  pl.BlockSpec((tm, LANES), lambda i: (i, 0)),
                  pl.BlockSpec((tm, 1), lambda i: (i, 0)),
                  pl.BlockSpec((1, MLA_Q_RANK), c2),
                  pl.BlockSpec((1, MLA_KV_RANK), c2),
                  pl.BlockSpec((MLA_HEADS, MLA_Q_RANK, MLA_D), c3),
                  pl.BlockSpec((MLA_HEADS, MLA_Q_RANK, MLA_D), c3),
                  pl.BlockSpec((MLA_HEADS, MLA_KV_RANK, MLA_D), c3),
                  pl.BlockSpec((MLA_HEADS, MLA_KV_RANK, MLA_V), c3),
                  pl.BlockSpec((LANES, MLA_D), c2),
                  pl.BlockSpec((LANES, MLA_D), c2),
                  pl.BlockSpec((1, MLA_D), c2),
                  pl.BlockSpec((1, MLA_D), c2)],
        out_specs=[pl.BlockSpec((1, MLA_HEADS, tm, MLA_D), hm),
                   pl.BlockSpec((1, MLA_HEADS, tm, MLA_D), hm),
                   pl.BlockSpec((1, MLA_HEADS, tm, MLA_V), hm)],
        out_shape=[jax.ShapeDtypeStruct((bsz, MLA_HEADS, seq, MLA_D), BF16),
                   jax.ShapeDtypeStruct((bsz, MLA_HEADS, seq, MLA_D), BF16),
                   jax.ShapeDtypeStruct((bsz, MLA_HEADS, seq, MLA_V), BF16)],
        compiler_params=_cparams(("arbitrary",)),
        name="mla_prep",
    )(q_a, kv_a, misc, positions.reshape(m, 1), gq.reshape(1, -1), gkv.reshape(1, -1), wq, wqs, wk, wv,
      jnp.asarray(pk, BF16), jnp.asarray(pks, BF16), jnp.asarray(invf), jnp.asarray(sgn))


_FIRST, _LAST, _MASKED = 1, 2, 4
AUG_TK = 512
AUG_POS = NSA_HEAD_DIM
AUG_SEL = AUG_POS + 4
AUG_NSEL = AUG_TK // NSA_SEL_BLOCK
SOFTMAX_ROWS = 64


def _flash_schedule(seq, tq, tk, window):
    qi_l, kb_l, fl_l = [], [], []
    for qi in range(seq // tq):
        q0, q1 = qi * tq, qi * tq + tq - 1
        first = q0 // tk
        lo = 0 if window is None else max(0, q0 - window + 1) // tk
        blocks = [first] + [b for b in range(lo, q1 // tk + 1) if b != first]
        for n, kb in enumerate(blocks):
            k0, k1 = kb * tk, kb * tk + tk - 1
            masked = k1 > q0 or (window is not None and q1 - k0 >= window)
            qi_l.append(qi)
            kb_l.append(kb)
            fl_l.append((_FIRST if n == 0 else 0) | (_LAST if n == len(blocks) - 1 else 0) | (_MASKED if masked else 0))
    return [jnp.asarray(np.asarray(t, np.int32)) for t in (qi_l, kb_l, fl_l)]


def _flash_kernel(qi_ref, kb_ref, fl_ref, *refs, hg, tq, tk, dvo, slopes, window, has_sel, kv_per_head):
    if has_sel:
        q_ref, k_ref, v_ref, selb_ref, o_ref, m_scr, l_scr, alpha_scr, acc_scr, s_scr, p_scr, bias_scr = refs
    else:
        q_ref, k_ref, v_ref, o_ref, m_scr, l_scr, alpha_scr, acc_scr, s_scr, p_scr, bias_scr = refs
    l_in_acc = acc_scr.shape[-1] > dvo
    g = pl.program_id(1)
    st = pl.program_id(2)
    qi, kb, fl = qi_ref[st], kb_ref[st], fl_ref[st]
    masked = (fl & _MASKED) != 0
    rows = hg * tq
    rc = SOFTMAX_ROWS

    @pl.when((fl & _FIRST) != 0)
    def _():
        m_scr[...] = jnp.full_like(m_scr, NEG_INF)
        l_scr[...] = jnp.zeros_like(l_scr)
        acc_scr[...] = jnp.zeros_like(acc_scr)

    def mask_bias():
        dist = (qi * tq + lax.broadcasted_iota(jnp.int32, (tq, tk), 0)) - \
               (kb * tk + lax.broadcasted_iota(jnp.int32, (tq, tk), 1))
        ok = dist >= 0
        if window is not None:
            ok = ok & (dist < window)
        return jnp.where(ok, 0.0, NEG_INF)

    def step(with_mask):
        with_bias = with_mask
        if with_mask:
            bias_scr[...] = mask_bias()
        q_sel = None
        if has_sel:
            src = lax.broadcasted_iota(jnp.int32, (LANES, LANES), 0)
            dst = lax.broadcasted_iota(jnp.int32, (LANES, LANES), 1)
            place = jnp.where(src - kb * AUG_NSEL == dst - AUG_SEL,
                              jnp.where(dst >= AUG_SEL, jnp.where(dst < AUG_SEL + AUG_NSEL, 1.0, 0.0), 0.0), 0.0)
            q_sel = _dot(selb_ref[0, 0], place.astype(BF16)).astype(BF16)
        kv = (lambda ref, h: ref[0, 0, h]) if kv_per_head else (lambda ref, h: ref[0, 0])
        for h in range(hg):
            q = q_ref[0, 0, h] if q_sel is None else q_ref[0, 0, h] + q_sel
            s_scr[h * tq:(h + 1) * tq, :] = _dot_nt(q, kv(k_ref, h))
        shift = [0.0] * hg
        if slopes is not None:
            shift = [(_by_group(g, slopes, h) * LOG2E) * (kb * tk - qi * tq).astype(F32) for h in range(hg)]
        for h in range(hg):
            for c in range(tq // rc):
                rs = slice(h * tq + c * rc, h * tq + (c + 1) * rc)
                s = s_scr[rs, :]
                if with_bias:
                    s = s + bias_scr[c * rc:(c + 1) * rc, :]
                    s_scr[rs, :] = s
                m_prev = m_scr[rs, :]
                m_next = jnp.maximum(m_prev, jnp.max(s, axis=1, keepdims=True) + shift[h])
                alpha_scr[rs, :] = jnp.exp2(m_prev - m_next)
                m_scr[rs, :] = m_next
        for h in range(hg):
            for c in range(tq // rc):
                rs = slice(h * tq + c * rc, h * tq + (c + 1) * rc)
                p = jnp.exp2(s_scr[rs, :] - jnp.tile(m_scr[rs, :] - shift[h], (1, tk // LANES)))
                alpha = alpha_scr[rs, :]
                if l_in_acc:
                    acc_scr[rs, :] = acc_scr[rs, :] * alpha
                else:
                    l_scr[rs, :] = alpha * l_scr[rs, :] + jnp.sum(p, axis=1, keepdims=True)
                    acc_scr[rs, :] = acc_scr[rs, :] * alpha[:, :dvo]
                p_scr[rs, :] = p.astype(BF16)
        for h in range(hg):
            hs = slice(h * tq, (h + 1) * tq)
            acc_scr[hs, :] += _dot(p_scr[hs, :], kv(v_ref, h))

    pl.when(masked)(lambda: step(True))
    pl.when(jnp.logical_not(masked))(lambda: step(False))

    @pl.when((fl & _LAST) != 0)
    def _():
        den = acc_scr[:, dvo:dvo + 1] if l_in_acc else l_scr[:, :dvo]
        out = acc_scr[:, :dvo] / jnp.maximum(den, 1e-30)
        o_ref[0] = jnp.concatenate([out[h * tq:(h + 1) * tq] for h in range(hg)], axis=-1).astype(o_ref.dtype)


def _flash(q, k, v, selb, *, tq, tk, slopes=None, window=None, out_dtype=F32):
    bsz, ng, hg, seq, d = q.shape
    dv = v.shape[-1]
    dvo = dv if slopes is None else dv - LANES // 2
    assert dvo == LANES // 2
    kv_per_head = k.ndim == 5
    sched = _flash_schedule(seq, tq, tk, window)
    rows = hg * tq
    assert slopes is None or (tk == AUG_TK and d == LANES and k.shape[-1] == LANES)
    if kv_per_head:
        kv_spec = lambda w: pl.BlockSpec((1, 1, hg, tk, w), lambda b, g, s, qi, kb, fl: (b, g, 0, kb[s], 0))
    else:
        kv_spec = lambda w: pl.BlockSpec((1, 1, tk, w), lambda b, g, s, qi, kb, fl: (b, g, kb[s], 0))
    in_specs = [pl.BlockSpec((1, 1, hg, tq, d), lambda b, g, s, qi, kb, fl: (b, g, 0, qi[s], 0)),
                kv_spec(d), kv_spec(dv)]
    args = [q, k, v]
    if selb is not None:
        in_specs.append(pl.BlockSpec((1, 1, tq, LANES), lambda b, g, s, qi, kb, fl: (b, g, qi[s], 0)))
        args.append(selb)
    return pl.pallas_call(
        functools.partial(_flash_kernel, hg=hg, tq=tq, tk=tk, dvo=dvo, slopes=slopes, window=window,
                          has_sel=selb is not None, kv_per_head=kv_per_head),
        grid_spec=pltpu.PrefetchScalarGridSpec(
            num_scalar_prefetch=3,
            grid=(bsz, ng, int(sched[0].shape[0])),
            in_specs=in_specs,
            out_specs=pl.BlockSpec((1, tq, hg * dvo), lambda b, g, s, qi, kb, fl: (b, qi[s], g)),
            scratch_shapes=[pltpu.VMEM((rows, LANES), F32), pltpu.VMEM((rows, LANES), F32),
                            pltpu.VMEM((rows, LANES), F32),
                            pltpu.VMEM((rows, dv), F32), pltpu.VMEM((rows, tk), F32),
                            pltpu.VMEM((rows, tk), BF16), pltpu.VMEM((tq, tk), F32)]),
        out_shape=jax.ShapeDtypeStruct((bsz, seq, ng * hg * dvo), out_dtype),
        compiler_params=_cparams(("arbitrary", "arbitrary", "arbitrary")),
        name="flash",
    )(*sched, *args)


def _gla_kernel(q_ref, k_ref, v_ref, gg_ref, misc_ref, w2_ref, bgk_ref, ng_ref, o_ref, st_scr):
    L = GLA_CHUNK

    @pl.when(pl.program_id(1) == 0)
    def _():
        st_scr[...] = jnp.zeros_like(st_scr)

    zg = _dot3(misc_ref[...], w2_ref[...]) + bgk_ref[...]
    log_a = -_softplus(-zg) * (1.0 / GLA_GATE_NORM)
    tril = _tril(L)
    bc_all = _dot_01_left(jnp.where(tril, 1.0, 0.0).astype(BF16), log_a, 3)
    row = lax.broadcasted_iota(jnp.int32, (L, GLA_DK), 0)
    for h in range(GLA_HEADS):
        kc, vc = slice(h * GLA_DK, (h + 1) * GLA_DK), slice(h * GLA_DV, (h + 1) * GLA_DV)
        bc = bc_all[:, kc]
        q = q_ref[:, kc] * (GLA_DK ** -0.5)
        k = k_ref[:, kc]
        v = v_ref[:, vc].astype(F32)
        st = st_scr[h]
        o = _dot_nt((q * jnp.exp(bc)).astype(BF16), st.astype(BF16))
        att = []
        for i in range(L // GLA_SUB):
            r0 = i * GLA_SUB
            ref = bc[r0:r0 + 1, :]
            qi = q[r0:r0 + GLA_SUB] * jnp.exp(bc[r0:r0 + GLA_SUB] - ref)
            ki = k * jnp.exp(jnp.where(row < r0 + GLA_SUB, ref - bc, 0.0))
            att.append(_dot_nt(qi.astype(BF16), ki.astype(BF16)))
        att = jnp.where(tril, jnp.concatenate(att, axis=0), 0.0)
        o = o + _dot(att.astype(BF16), v.astype(BF16))
        b_last = bc[L - 1:L, :]
        kd = (k * jnp.exp(b_last - bc)).astype(BF16)
        st_scr[h] = st * jnp.exp(b_last) + _dot(v.T.astype(BF16), kd)
        o_ref[:, vc] = (_rms(o, ng_ref[...]) * _silu(gg_ref[:, vc].astype(F32))).astype(o_ref.dtype)


def _gla(gq, gk, gv, gg, misc, w_gk2, b_gk, norm_g, bsz, seq):
    L = GLA_CHUNK
    nc = seq // L
    m = bsz * seq
    nk, nv = GLA_HEADS * GLA_DK, GLA_HEADS * GLA_DV
    w2 = jnp.zeros((LANES, nk), F32).at[:GLA_GATE_RANK].set(w_gk2)
    row = lambda b, c: (b * nc + c, 0)
    const = lambda b, c: (0, 0)
    return pl.pallas_call(
        _gla_kernel,
        grid=(bsz, nc),
        in_specs=[pl.BlockSpec((L, nk), row),
                  pl.BlockSpec((L, nk), row),
                  pl.BlockSpec((L, nv), row),
                  pl.BlockSpec((L, nv), row),
                  pl.BlockSpec((L, LANES), row),
                  pl.BlockSpec((LANES, nk), const),
                  pl.BlockSpec((1, nk), const),
                  pl.BlockSpec((1, GLA_DV), const)],
        out_specs=pl.BlockSpec((L, nv), row),
        out_shape=jax.ShapeDtypeStruct((m, nv), BF16),
        scratch_shapes=[pltpu.VMEM((GLA_HEADS, GLA_DV, GLA_DK), F32)],
        compiler_params=_cparams(("arbitrary", "arbitrary")),
        name="gla",
    )(gq, gk, gv, gg, misc, w2, b_gk.reshape(1, -1), norm_g.reshape(1, -1))


def _cmp_kernel(z_ref, pa_ref, pb_ref, wa_ref, wb_ref, w2_ref, o_ref, *, prec):
    z = z_ref[0]
    if prec is None:
        cast = lambda t: t.astype(BF16)
    else:
        cast = lambda t: t
    first = _dot(cast(z + pa_ref[...]), cast(wa_ref[...]), prec)
    second = _dot(cast(z + pb_ref[...]), cast(wb_ref[...]), prec)
    n = first.shape[0]
    hid = _silu(first + pltpu.roll(second, n - 1, 0))
    for g in range(NSA_GROUPS):
        o_ref[0, g] = _dot(cast(hid[:, g * NSA_CMP_HIDDEN:(g + 1) * NSA_CMP_HIDDEN]), cast(w2_ref[...]), prec)


def _compress(z, cmp_pos, w1, w2, bsz, seq, prec):
    nb = seq // NSA_CMP_STRIDE
    width = NSA_CMP_STRIDE * NSA_GROUPS * NSA_HEAD_DIM
    zr = z.reshape(bsz, nb, width)
    per = NSA_CMP_BLOCK // NSA_CMP_STRIDE
    eye = jnp.eye(NSA_GROUPS, dtype=F32)
    wbig = jnp.einsum("ldj,gh->lgdhj", w1.reshape(NSA_CMP_BLOCK, NSA_HEAD_DIM, NSA_CMP_HIDDEN), eye)
    wbig = wbig.reshape(per, width, NSA_GROUPS * NSA_CMP_HIDDEN)
    posb = jnp.broadcast_to(cmp_pos[:, None, :], (NSA_CMP_BLOCK, NSA_GROUPS, NSA_HEAD_DIM)).reshape(per, 1, width)
    assert per == 2
    c2 = lambda b: (0, 0)
    return pl.pallas_call(
        functools.partial(_cmp_kernel, prec=prec),
        grid=(bsz,),
        in_specs=[pl.BlockSpec((1, nb, width), lambda b: (b, 0, 0)),
                  pl.BlockSpec((1, width), c2), pl.BlockSpec((1, width), c2),
                  pl.BlockSpec((width, NSA_GROUPS * NSA_CMP_HIDDEN), c2),
                  pl.BlockSpec((width, NSA_GROUPS * NSA_CMP_HIDDEN), c2),
                  pl.BlockSpec((NSA_CMP_HIDDEN, NSA_HEAD_DIM), c2)],
        out_specs=pl.BlockSpec((1, NSA_GROUPS, nb, NSA_HEAD_DIM), lambda b: (b, 0, 0, 0)),
        out_shape=jax.ShapeDtypeStruct((bsz, NSA_GROUPS, nb, NSA_HEAD_DIM), F32),
        compiler_params=_cparams(("arbitrary",)),
        name="nsa_compress",
    )(zr, posb[0], posb[1], wbig[0], wbig[1], w2)


def _nsa_select_kernel(q_ref, kc_ref, vc_ref, ovt_ref, oc_ref, selb_ref, qs_ref, *, tq, n_sel, slopes, scale):
    g = pl.program_id(1)
    qi = pl.program_id(2)
    ncmp = kc_ref.shape[2]
    nslc = ovt_ref.shape[0]
    qpos_c = qi * tq + lax.broadcasted_iota(jnp.int32, (ncmp, tq), 1)
    cmp_end = lax.broadcasted_iota(jnp.int32, (ncmp, tq), 0) * NSA_CMP_STRIDE + (NSA_CMP_BLOCK - 1)
    dist = (qpos_c - cmp_end).astype(F32)
    dist = jnp.where(dist >= 0.0, dist, MASK_DIST)
    kc_hi, kc_lo = _split(kc_ref[0, 0], 2)
    vc = vc_ref[0, 0].astype(BF16)
    dh = NSA_HEAD_DIM
    o_c = []
    psum = jnp.zeros((ncmp, tq), F32)
    for h in range(NSA_HG):
        q = q_ref[:, h * dh:(h + 1) * dh] * (scale * LOG2E)
        c = jnp.full((1, LANES - dh), _by_group(g, slopes, h) * LOG2E, F32)
        c_hi = c.astype(BF16).astype(F32)
        c_lo = c - c_hi
        lane = lax.broadcasted_iota(jnp.int32, (1, LANES - dh), 1)
        feat = jnp.where(lane == 0, 16.0 * c_hi, jnp.where(lane == 1, c_hi, jnp.where(
            lane == 2, 16.0 * c_lo, jnp.where(lane == 3, c_lo, 0.0))))
        qs_ref[0, 0, h] = jnp.concatenate([q, jnp.broadcast_to(feat, (tq, LANES - dh))], axis=-1).astype(qs_ref.dtype)
        q_hi, q_lo = _split(q, 2)
        qk = _dot_nt(kc_hi, q_hi) + (_dot_nt(kc_hi, q_lo) + _dot_nt(kc_lo, q_hi))
        s = qk - (_by_group(g, slopes, h) * LOG2E) * dist
        mx = jnp.max(s, axis=0, keepdims=True)
        ex = jnp.exp2(s - mx)
        den = jnp.maximum(jnp.sum(ex, axis=0, keepdims=True), 1e-30)
        p = ex * jnp.where(mx > NEG_INF, 1.0 / den, 0.0)
        o_c.append(lax.dot_general(p.astype(BF16), vc, (((0,), (0,)), ((), ())), preferred_element_type=F32))
        psum = psum + p
    oc_ref[...] = jnp.concatenate(o_c, axis=-1)
    imp = _dot_01_left(ovt_ref[...], psum, 2)
    blk = lax.broadcasted_iota(jnp.int32, (nslc, tq), 0)
    qpos = qi * tq + lax.broadcasted_iota(jnp.int32, (nslc, tq), 1)
    forced = (blk == 0) | (blk == (qpos >> _SEL_SHIFT))
    avail = blk * NSA_SEL_BLOCK <= qpos
    imp = jnp.where(forced, NSA_FORCE, jnp.where(avail, imp, -1.0))
    blkf = blk.astype(F32)
    selb = jnp.full((nslc, tq), NEG_INF, F32)
    for _ in range(n_sel):
        mx = jnp.max(imp, axis=0, keepdims=True)
        first = jnp.min(jnp.where(imp == mx, blkf, float(nslc)), axis=0, keepdims=True)
        one = blkf == first
        selb = jnp.where(one, 0.0, selb)
        imp = jnp.where(one, -2.0, imp)
    selb_ref[0, 0] = selb.T.astype(selb_ref.dtype)


def _nsa_select(nq, k_cmp, v_cmp, bsz, seq, tq=512):
    ng, hg, dh = NSA_GROUPS, NSA_HG, NSA_HEAD_DIM
    ncmp = k_cmp.shape[2]
    nslc = seq // NSA_SEL_BLOCK
    n_sel = min(NSA_N_SEL, nslc)
    nt = seq // tq
    c_start = np.arange(ncmp) * NSA_CMP_STRIDE
    s_start = np.arange(nslc) * NSA_SEL_BLOCK
    ovt = ((c_start[None, :] < s_start[:, None] + NSA_SEL_BLOCK)
           & (c_start[None, :] + NSA_CMP_BLOCK > s_start[:, None])).astype(np.float32)
    ovt[:, (seq - NSA_CMP_BLOCK) // NSA_CMP_STRIDE + 1:] = 0.0
    return pl.pallas_call(
        functools.partial(_nsa_select_kernel, tq=tq, n_sel=n_sel, slopes=_alibi_table(NSA_HEADS, NSA_GROUPS),
                          scale=NSA_HEAD_DIM ** -0.5),
        grid=(bsz, ng, nt),
        in_specs=[pl.BlockSpec((tq, hg * dh), lambda b, g, i: (b * nt + i, g)),
                  pl.BlockSpec((1, 1, ncmp, dh), lambda b, g, i: (b, g, 0, 0)),
                  pl.BlockSpec((1, 1, ncmp, dh), lambda b, g, i: (b, g, 0, 0)),
                  pl.BlockSpec((nslc, ncmp), lambda b, g, i: (0, 0))],
        out_specs=[pl.BlockSpec((tq, hg * dh), lambda b, g, i: (b * nt + i, g)),
                   pl.BlockSpec((1, 1, tq, nslc), lambda b, g, i: (b, g, i, 0)),
                   pl.BlockSpec((1, 1, hg, tq, LANES), lambda b, g, i: (b, g, 0, i, 0))],
        out_shape=[jax.ShapeDtypeStruct((bsz * seq, ng * hg * dh), F32),
                   jax.ShapeDtypeStruct((bsz, ng, seq, nslc), BF16),
                   jax.ShapeDtypeStruct((bsz, ng, hg, seq, LANES), BF16)],
        compiler_params=_cparams(("arbitrary", "arbitrary", "arbitrary")),
        name="nsa_select",
    )(nq, k_cmp, v_cmp, jnp.asarray(ovt, BF16))


def _nsa_combine_kernel(oc_ref, os_ref, ow_ref, misc_ref, e_ref, o_ref):
    gates = 1.0 / (1.0 + jnp.exp(-misc_ref[...]))
    o = (_dot_01_right(gates, e_ref[0], 2) * oc_ref[...] + _dot_01_right(gates, e_ref[1], 2) * os_ref[...]
         + _dot_01_right(gates, e_ref[2], 2) * ow_ref[...])
    o_ref[...] = o.astype(o_ref.dtype)


def _nsa_combine(o_c, o_s, o_w, misc, tq=512):
    m, width = o_c.shape
    e = np.zeros((3, LANES, width), np.float32)
    for h in range(NSA_HEADS):
        for r in range(3):
            e[r, GLA_GATE_RANK + 3 * h + r, h * NSA_HEAD_DIM:(h + 1) * NSA_HEAD_DIM] = 1.0
    spec = pl.BlockSpec((tq, width), lambda i: (i, 0))
    return pl.pallas_call(
        _nsa_combine_kernel,
        grid=(m // tq,),
        in_specs=[spec, spec, spec, pl.BlockSpec((tq, LANES), lambda i: (i, 0)),
                  pl.BlockSpec((3, LANES, width), lambda i: (0, 0, 0))],
        out_specs=spec,
        out_shape=jax.ShapeDtypeStruct((m, width), BF16),
        compiler_params=_cparams(("arbitrary",)),
        name="nsa_combine",
    )(o_c, o_s, o_w, misc, jnp.asarray(e, BF16))


L0_SEGS = ((0, 1024, 0, BF16, False), (1024, 2560, 0, F32, False), (2560, 2688, 0, F32, False),
           (2688, 3072, 0, BF16, False), (3072, 3328, 0, BF16, False))
_G = NSA_GROUPS
L1_SEGS = ((0, 512, 0, F32, False), (512, 1024, 0, F32, False), (1024, 2048, 0, BF16, False),
           (2048, 3072, 0, BF16, False), (3072, 3584, 0, F32, False), (3584, 3712, 0, F32, False),
           (3712, 3840, 0, F32, False), (3840, 3968, _G, BF16, 1), (3968, 4096, _G, BF16, 2),
           (4096, 4224, _G, BF16, 1), (4224, 4352, _G, BF16, 2), (4352, 4480, 0, F32, False))


def _pack_w_in0(w):
    d = w.shape[0]
    a = SSD_INNER + SSD_CONV_CH
    dt = w[:, a:a + SSD_HEADS]
    qa = w[:, a + SSD_HEADS:a + SSD_HEADS + MLA_Q_RANK]
    kva = w[:, a + SSD_HEADS + MLA_Q_RANK:a + SSD_HEADS + MLA_Q_RANK + MLA_KV_RANK]
    kpe = w[:, a + SSD_HEADS + MLA_Q_RANK + MLA_KV_RANK:]
    pad = jnp.zeros((d, LANES - SSD_HEADS - MLA_ROPE), F32)
    return jnp.concatenate([w[:, :a], dt, kpe, pad, qa, kva], axis=1).astype(BF16)


def _pack_w_in1(w):
    d = w.shape[0]
    qk = 2 * GLA_HEADS * GLA_DK
    vv = GLA_HEADS * GLA_DV
    o = qk + vv
    glr = w[:, o:o + GLA_GATE_RANK]
    gg = w[:, o + GLA_GATE_RANK:o + GLA_GATE_RANK + vv]
    o2 = o + GLA_GATE_RANK + vv
    nsa = w[:, o2:o2 + NSA_HEADS * NSA_HEAD_DIM + 6 * NSA_GROUPS * NSA_HEAD_DIM]
    ngate = w[:, o2 + NSA_HEADS * NSA_HEAD_DIM + 6 * NSA_GROUPS * NSA_HEAD_DIM:]
    pad = jnp.zeros((d, LANES - GLA_GATE_RANK - 3 * NSA_HEADS), F32)
    return jnp.concatenate([w[:, :o], gg, nsa, glr, ngate, pad], axis=1).astype(BF16)


def _mixer0_parts(h_args, positions, bsz, seq, w_in, conv_w, conv_b, dt_bias, a_log, d_skip, ssm_norm_g,
                  q_a_norm_g, w_q_b, kv_a_norm_g, w_kv_b):
    z, xbc, misc, q_a, kv_a = _proj_in(*h_args, _pack_w_in0(w_in), L0_SEGS, seq)
    y_ssd = _ssd(z, xbc, misc, conv_w, conv_b, dt_bias, a_log, d_skip, ssm_norm_g, bsz, seq)
    q, k, v = _mla_prep(q_a, kv_a, misc, positions, q_a_norm_g, kv_a_norm_g, w_q_b, w_kv_b, bsz, seq)
    pair = lambda t: t.reshape(bsz, MLA_HEADS // MLA_HEADS_PER_STEP, MLA_HEADS_PER_STEP, seq, t.shape[-1])
    o = _flash(pair(q), pair(k), pair(v), None, tq=512, tk=512, out_dtype=BF16)
    return y_ssd, o.reshape(bsz * seq, MLA_HEADS * MLA_V)


def _mixer1_parts(h_args, bsz, seq, w_in, w_gk2, b_gk, gla_norm_g, cmp_pos, cmp_k_w1, cmp_k_w2, cmp_v_w1, cmp_v_w2):
    gq, gk, gv, gg, nq, kc, vc, ks, vs, kw, vw, misc = _proj_in(*h_args, _pack_w_in1(w_in), L1_SEGS, seq)
    o_gla = _gla(gq, gk, gv, gg, misc, w_gk2, b_gk, gla_norm_g, bsz, seq)
    k_cmp = _compress(kc, cmp_pos, cmp_k_w1, cmp_k_w2, bsz, seq, HI)
    v_cmp = _compress(vc, cmp_pos, cmp_v_w1, cmp_v_w2, bsz, seq, None)
    o_c, selb, qs = _nsa_select(nq, k_cmp, v_cmp, bsz, seq)
    if selb.shape[-1] < LANES:
        selb = jnp.pad(selb, ((0, 0), (0, 0), (0, 0), (0, LANES - selb.shape[-1])))
    slopes = _alibi_table(NSA_HEADS, NSA_GROUPS)
    width = NSA_HEADS * NSA_HEAD_DIM
    o_s = _flash(qs, ks, vs, selb, tq=512, tk=512, slopes=slopes).reshape(bsz * seq, width)
    o_w = _flash(qs, kw, vw, None, tq=NSA_WINDOW, tk=NSA_WINDOW, slopes=slopes, window=NSA_WINDOW)
    return o_gla, _nsa_combine(o_c, o_s, o_w.reshape(bsz * seq, width), misc)


def kernel(x, c, positions, l0_ada_w, l0_ada_b, l0_mix_pre_g, l0_mix_post_g, l0_w_in, l0_conv_w, l0_conv_b, l0_dt_bias, l0_a_log, l0_d_skip, l0_ssm_norm_g, l0_q_a_norm_g, l0_w_q_b, l0_kv_a_norm_g, l0_w_kv_b, l0_w_out, l0_ffn_pre_g, l0_ffn_post_g, l0_w_gate, l0_w_up, l0_w_down, l1_ada_w, l1_ada_b, l1_mix_pre_g, l1_mix_post_g, l1_w_in, l1_w_gk2, l1_b_gk, l1_gla_norm_g, l1_cmp_pos, l1_cmp_k_w1, l1_cmp_k_w2, l1_cmp_v_w1, l1_cmp_v_w2, l1_w_out, l1_ffn_pre_g, l1_ffn_post_g, l1_w_gate, l1_w_up, l1_w_down):
    bsz, seq, d = x.shape
    x2 = x.reshape(bsz * seq, d)

    def sublayers(x2, ada_w, ada_b, pre_m, post_m, mixer, w_out, pre_f, post_f, w_gate, w_up, w_down):
        shift_m, scale_m, gate_m, shift_f, scale_f, gate_f = _ada(c, ada_w, ada_b)
        a, b = mixer((x2, pre_m, scale_m, shift_m))
        ka = a.shape[1]
        x2 = _out_res(x2, a, b, w_out[:ka].astype(BF16), w_out[ka:].astype(BF16), post_m, gate_m, seq)
        return _ffn(x2, pre_f, scale_f, shift_f, w_gate.astype(BF16), w_up.astype(BF16), w_down.astype(BF16),
                    post_f, gate_f, seq)

    x2 = sublayers(
        x2, l0_ada_w, l0_ada_b, l0_mix_pre_g, l0_mix_post_g,
        lambda h: _mixer0_parts(h, positions, bsz, seq, l0_w_in, l0_conv_w, l0_conv_b, l0_dt_bias, l0_a_log,
                                l0_d_skip, l0_ssm_norm_g, l0_q_a_norm_g, l0_w_q_b, l0_kv_a_norm_g, l0_w_kv_b),
        l0_w_out, l0_ffn_pre_g, l0_ffn_post_g, l0_w_gate, l0_w_up, l0_w_down)
    x2 = sublayers(
        x2, l1_ada_w, l1_ada_b, l1_mix_pre_g, l1_mix_post_g,
        lambda h: _mixer1_parts(h, bsz, seq, l1_w_in, l1_w_gk2, l1_b_gk, l1_gla_norm_g, l1_cmp_pos,
                                l1_cmp_k_w1, l1_cmp_k_w2, l1_cmp_v_w1, l1_cmp_v_w2),
        l1_w_out, l1_ffn_pre_g, l1_ffn_post_g, l1_w_gate, l1_w_up, l1_w_down)
    return x2.reshape(bsz, seq, d)
```

```python
import functools

import numpy as np
import jax
import jax.numpy as jnp
from jax import lax
from jax.experimental import pallas as pl
from jax.experimental.pallas import tpu as pltpu

F32, BF16 = jnp.float32, jnp.bfloat16
HI = lax.Precision.HIGHEST
LANES = 128
VMEM_LIMIT = 48 * 1024 * 1024

NORM_EPS = 1e-6
NEG_INF = -1e30
LOG2E = 1.4426950408889634
MASK_DIST = 1e33
N_MOD = 6

SSD_HEADS, SSD_HEAD_DIM, SSD_STATE, SSD_GROUPS, SSD_CONV = 16, 64, 128, 2, 4
SSD_INNER = SSD_HEADS * SSD_HEAD_DIM
SSD_CONV_CH = SSD_INNER + 2 * SSD_GROUPS * SSD_STATE
SSD_CHUNK = 128

MLA_HEADS, MLA_Q_RANK, MLA_KV_RANK, MLA_NOPE, MLA_ROPE, MLA_V = 8, 384, 256, 64, 32, 64
ROPE_THETA = 10000.0

GLA_HEADS, GLA_DK, GLA_DV, GLA_GATE_RANK, GLA_GATE_NORM = 4, 128, 256, 16, 16.0
GLA_CHUNK = 128
GLA_SUB = 16

NSA_HEADS, NSA_GROUPS, NSA_HEAD_DIM = 8, 2, 64
NSA_HG = NSA_HEADS // NSA_GROUPS
NSA_CMP_BLOCK, NSA_CMP_STRIDE, NSA_CMP_HIDDEN = 32, 16, 256
NSA_SEL_BLOCK, NSA_N_SEL, NSA_WINDOW, NSA_FORCE = 64, 16, 512, 1e4
_SEL_SHIFT = NSA_SEL_BLOCK.bit_length() - 1
assert 1 << _SEL_SHIFT == NSA_SEL_BLOCK


def _cparams(sem):
    return pltpu.CompilerParams(dimension_semantics=sem, vmem_limit_bytes=VMEM_LIMIT)


def _dot(a, b, prec=None):
    return jnp.dot(a, b, preferred_element_type=F32, precision=prec)


def _dot_nt(a, b, prec=None):
    return lax.dot_general(a, b, (((1,), (1,)), ((), ())), preferred_element_type=F32, precision=prec)


def _split(x, n):
    parts = []
    for _ in range(n):
        p = x.astype(BF16)
        parts.append(p)
        x = x - p.astype(F32)
    return parts


def _dot_01_left(sel, x, n):
    return sum(_dot(sel, p) for p in _split(x, n))


def _dot_01_right(x, sel, n):
    return sum(_dot(p, sel) for p in _split(x, n))


def _dot3(a, b, nt=False):
    f = _dot_nt if nt else _dot
    (ah, al), (bh, bl) = _split(a, 2), _split(b, 2)
    return f(ah, bh) + (f(ah, bl) + f(al, bh))


def _silu(x):
    return x * (1.0 / (1.0 + jnp.exp(-x)))


def _softplus(x):
    return jnp.maximum(x, 0.0) + jnp.log1p(jnp.exp(-jnp.abs(x)))


def _rms(x, g):
    return x * lax.rsqrt(jnp.mean(x * x, axis=-1, keepdims=True) + NORM_EPS) * g


def _tril(n):
    return lax.broadcasted_iota(jnp.int32, (n, n), 0) >= lax.broadcasted_iota(jnp.int32, (n, n), 1)


def _alibi_table(n, groups):
    s = 2.0 ** (-8.0 * np.arange(1, n + 1) / n)
    return [[float(v) for v in row] for row in s.reshape(groups, n // groups)]


def _by_group(g, table, h):
    val = table[0][h]
    for gi in range(1, len(table)):
        val = jnp.where(g == gi, table[gi][h], val)
    return val


def _ada_kernel(c_ref, w_ref, b_ref, o_ref):
    o_ref[...] = _dot(_silu(c_ref[...]), w_ref[...], HI) + b_ref[...]


def _ada(c, w, b):
    bsz, d = c.shape
    n = w.shape[1]
    rows = 8
    cp = jnp.zeros((rows, d), F32).at[:bsz].set(c)
    tn = 1024
    out = pl.pallas_call(
        _ada_kernel,
        grid=(n // tn,),
        in_specs=[pl.BlockSpec((rows, d), lambda j: (0, 0)),
                  pl.BlockSpec((d, tn), lambda j: (0, j)),
                  pl.BlockSpec((1, tn), lambda j: (0, j))],
        out_specs=pl.BlockSpec((rows, tn), lambda j: (0, j)),
        out_shape=jax.ShapeDtypeStruct((rows, n), F32),
        compiler_params=_cparams(("arbitrary",)),
        name="ada",
    )(cp, w, b.reshape(1, n))
    return [m.reshape(bsz, 1, d) for m in jnp.split(out[:bsz], N_MOD, axis=-1)]


def _proj_in_kernel(x_ref, g_ref, sc_ref, sh_ref, w_ref, *refs, segs):
    feat_ref, o_refs = (refs[0], refs[1:]) if any(s[4] for s in segs) else (None, refs)
    h = (_rms(x_ref[...], g_ref[...]) * (1.0 + sc_ref[0]) + sh_ref[0]).astype(BF16)
    for (a, b, heads, _, aug), o_ref in zip(segs, o_refs):
        res = _dot(h, w_ref[:, a:b])
        if heads:
            for g in range(heads):
                dh = (b - a) // heads
                piece = res[:, g * dh:(g + 1) * dh]
                if aug:
                    piece = jnp.concatenate([piece, feat_ref[...].astype(F32)], axis=-1)
                o_ref[0, g] = piece.astype(o_ref.dtype)
        else:
            o_ref[...] = res.astype(o_ref.dtype)


def _key_features(tm):
    assert tm % AUG_TK == 0
    off = np.arange(tm) % AUG_TK
    f = np.zeros((tm, LANES - NSA_HEAD_DIM), np.float32)
    f[:, 0] = f[:, 2] = off >> 4
    f[:, 1] = f[:, 3] = off & 15
    f[np.arange(tm), AUG_SEL - AUG_POS + off // NSA_SEL_BLOCK] = 1.0
    return jnp.asarray(f, BF16)


def _proj_in(x2, g, scale, shift, w, segs, seq, tm=512):
    m, d = x2.shape
    n = w.shape[1]
    tpb = seq // tm
    has_feat = any(s[4] for s in segs)
    width = lambda a, b, hd, aug: LANES if aug else (b - a) // hd
    return pl.pallas_call(
        functools.partial(_proj_in_kernel, segs=segs),
        grid=(m // tm,),
        in_specs=[pl.BlockSpec((tm, d), lambda i: (i, 0)),
                  pl.BlockSpec((1, d), lambda i: (0, 0)),
                  pl.BlockSpec((1, 1, d), lambda i: (i // tpb, 0, 0)),
                  pl.BlockSpec((1, 1, d), lambda i: (i // tpb, 0, 0)),
                  pl.BlockSpec((d, n), lambda i: (0, 0))]
                 + ([pl.BlockSpec((tm, LANES - NSA_HEAD_DIM), lambda i: (0, 0))] if has_feat else []),
        out_specs=[pl.BlockSpec((1, hd, tm, width(a, b, hd, aug)), lambda i: (i // tpb, 0, i % tpb, 0)) if hd else
                   pl.BlockSpec((tm, b - a), lambda i: (i, 0)) for a, b, hd, _, aug in segs],
        out_shape=[jax.ShapeDtypeStruct((m // seq, hd, seq, width(a, b, hd, aug)), dt) if hd else
                   jax.ShapeDtypeStruct((m, b - a), dt) for a, b, hd, dt, aug in segs],
        compiler_params=_cparams(("arbitrary",)),
        name="proj_in",
    )(x2, g.reshape(1, d), scale, shift, w, *([_key_features(tm)] if has_feat else []))


def _out_res_kernel(x_ref, a_ref, b_ref, wa_ref, wb_ref, g_ref, gate_ref, o_ref):
    y = _dot(a_ref[...], wa_ref[...]) + _dot(b_ref[...], wb_ref[...])
    o_ref[...] = x_ref[...] + gate_ref[0] * _rms(y, g_ref[...])


def _out_res(x2, a, b, wa, wb, g, gate, seq, tm=512):
    m, d = x2.shape
    ka, kb = a.shape[1], b.shape[1]
    tpb = seq // tm
    return pl.pallas_call(
        _out_res_kernel,
        grid=(m // tm,),
        in_specs=[pl.BlockSpec((tm, d), lambda i: (i, 0)),
                  pl.BlockSpec((tm, ka), lambda i: (i, 0)),
                  pl.BlockSpec((tm, kb), lambda i: (i, 0)),
                  pl.BlockSpec((ka, d), lambda i: (0, 0)),
                  pl.BlockSpec((kb, d), lambda i: (0, 0)),
                  pl.BlockSpec((1, d), lambda i: (0, 0)),
                  pl.BlockSpec((1, 1, d), lambda i: (i // tpb, 0, 0))],
        out_specs=pl.BlockSpec((tm, d), lambda i: (i, 0)),
        out_shape=jax.ShapeDtypeStruct((m, d), F32),
        compiler_params=_cparams(("arbitrary",)),
        name="out_res",
    )(x2, a, b, wa, wb, g.reshape(1, d), gate)


def _ffn_kernel(x_ref, gpre_ref, sc_ref, sh_ref, wg_ref, wu_ref, wd_ref, gpost_ref, gate_ref, o_ref,
                h_scr, acc_scr):
    j = pl.program_id(1)

    @pl.when(j == 0)
    def _():
        h_scr[...] = (_rms(x_ref[...], gpre_ref[...]) * (1.0 + sc_ref[0]) + sh_ref[0]).astype(BF16)
        acc_scr[...] = jnp.zeros_like(acc_scr)

    h = h_scr[...]
    act = (_silu(_dot(h, wg_ref[...])) * _dot(h, wu_ref[...])).astype(BF16)
    acc_scr[...] += _dot(act, wd_ref[...])

    @pl.when(j == pl.num_programs(1) - 1)
    def _():
        o_ref[...] = x_ref[...] + gate_ref[0] * _rms(acc_scr[...], gpost_ref[...])


def _ffn(x2, gpre, scale, shift, wg, wu, wd, gpost, gate, seq, tm=512, th=1408):
    m, d = x2.shape
    hid = wg.shape[1]
    tpb = seq // tm
    return pl.pallas_call(
        _ffn_kernel,
        grid=(m // tm, hid // th),
        in_specs=[pl.BlockSpec((tm, d), lambda i, j: (i, 0)),
                  pl.BlockSpec((1, d), lambda i, j: (0, 0)),
                  pl.BlockSpec((1, 1, d), lambda i, j: (i // tpb, 0, 0)),
                  pl.BlockSpec((1, 1, d), lambda i, j: (i // tpb, 0, 0)),
                  pl.BlockSpec((d, th), lambda i, j: (0, j)),
                  pl.BlockSpec((d, th), lambda i, j: (0, j)),
                  pl.BlockSpec((th, d), lambda i, j: (j, 0)),
                  pl.BlockSpec((1, d), lambda i, j: (0, 0)),
                  pl.BlockSpec((1, 1, d), lambda i, j: (i // tpb, 0, 0))],
        out_specs=pl.BlockSpec((tm, d), lambda i, j: (i, 0)),
        out_shape=jax.ShapeDtypeStruct((m, d), F32),
        scratch_shapes=[pltpu.VMEM((tm, d), BF16), pltpu.VMEM((tm, d), F32)],
        compiler_params=_cparams(("arbitrary", "arbitrary")),
        name="ffn",
    )(x2, gpre.reshape(1, d), scale, shift, wg, wu, wd, gpost.reshape(1, d), gate)


def _ssd_kernel(z_ref, xbc_ref, misc_ref, cw_ref, cb_ref, dtb_ref, alog_ref, dsk_ref, ng_ref, e_ref, o_ref,
                ext_scr, st_scr):
    L = SSD_CHUNK
    gsz = SSD_INNER // SSD_GROUPS
    hpg = SSD_HEADS // SSD_GROUPS
    pad = 8

    @pl.when(pl.program_id(1) == 0)
    def _():
        ext_scr[0:pad, :] = jnp.zeros((pad, SSD_CONV_CH), F32)
        st_scr[...] = jnp.zeros_like(st_scr)

    xt = xbc_ref[...]
    ext_scr[pad:pad + L, :] = xt
    acc = cb_ref[...] + cw_ref[0:1, :] * ext_scr[pad - 3:pad - 3 + L, :]
    for k in range(1, SSD_CONV):
        acc = acc + cw_ref[k:k + 1, :] * ext_scr[pad - 3 + k:pad - 3 + k + L, :]
    ext_scr[0:pad, :] = xt[L - pad:L, :]
    xbc = _silu(acc)
    xs = xbc[:, :SSD_INNER]

    e = e_ref[...]
    dt = _softplus(misc_ref[...] + dtb_ref[...])
    adt = dt * (-jnp.exp(alog_ref[...]))
    tril = _tril(L)
    a_cs = _dot_01_left(jnp.where(tril, 1.0, 0.0).astype(BF16), adt, 3)
    a_cs_t = a_cs.T
    ea = jnp.exp(a_cs)
    ea_e = _dot_01_right(ea, e, 2)
    dec_e = _dot_01_right(jnp.exp(a_cs[L - 1:L, :] - a_cs), e, 2)
    xd = xs * _dot_01_right(dt, e, 2)

    ys = []
    for g in range(SSD_GROUPS):
        bg = xbc[:, SSD_INNER + g * SSD_STATE:SSD_INNER + (g + 1) * SSD_STATE]
        cg = xbc[:, SSD_INNER + (SSD_GROUPS + g) * SSD_STATE:SSD_INNER + (SSD_GROUPS + g + 1) * SSD_STATE]
        bg16, cg16 = bg.astype(BF16), cg.astype(BF16)
        gmat = _dot_nt(cg16, bg16)
        cols = slice(g * gsz, (g + 1) * gsz)
        xdg = xd[:, cols]
        st = st_scr[g]
        y_off = _dot(cg16, st.astype(BF16)) * ea_e[:, cols]
        st_scr[g] = st * ea_e[L - 1:L, cols] + _dot(bg.T.astype(BF16), (xdg * dec_e[:, cols]).astype(BF16))
        yd = []
        for h in range(hpg):
            hh = g * hpg + h
            seg = a_cs[:, hh:hh + 1] - a_cs_t[hh:hh + 1, :]
            lmat = jnp.where(tril, jnp.exp(seg), 0.0)
            yd.append(_dot((gmat * lmat).astype(BF16),
                           xdg[:, h * SSD_HEAD_DIM:(h + 1) * SSD_HEAD_DIM].astype(BF16)))
        y = jnp.concatenate(yd, axis=-1) + y_off + dsk_ref[:, cols] * xs[:, cols]
        y = y * _silu(z_ref[:, cols].astype(F32))
        ys.append(_rms(y, ng_ref[:, cols]))
    o_ref[...] = jnp.concatenate(ys, axis=-1).astype(o_ref.dtype)


def _ssd(z, xbc, misc, conv_w, conv_b, dt_bias, a_log, d_skip, norm_g, bsz, seq):
    L = SSD_CHUNK
    nc = seq // L
    m = bsz * seq
    e = np.zeros((LANES, SSD_INNER), np.float32)
    for h in range(SSD_HEADS):
        e[h, h * SSD_HEAD_DIM:(h + 1) * SSD_HEAD_DIM] = 1.0
    pad128 = lambda v: jnp.zeros((1, LANES), F32).at[0, :v.shape[0]].set(v)
    row = lambda i, c: (i * nc + c, 0)
    const = lambda i, c: (0, 0)
    return pl.pallas_call(
        _ssd_kernel,
        grid=(bsz, nc),
        in_specs=[pl.BlockSpec((L, SSD_INNER), row),
                  pl.BlockSpec((L, SSD_CONV_CH), row),
                  pl.BlockSpec((L, LANES), row),
                  pl.BlockSpec((SSD_CONV, SSD_CONV_CH), const),
                  pl.BlockSpec((1, SSD_CONV_CH), const),
                  pl.BlockSpec((1, LANES), const),
                  pl.BlockSpec((1, LANES), const),
                  pl.BlockSpec((1, SSD_INNER), const),
                  pl.BlockSpec((1, SSD_INNER), const),
                  pl.BlockSpec((LANES, SSD_INNER), const)],
        out_specs=pl.BlockSpec((L, SSD_INNER), row),
        out_shape=jax.ShapeDtypeStruct((m, SSD_INNER), BF16),
        scratch_shapes=[pltpu.VMEM((L + 8, SSD_CONV_CH), F32),
                        pltpu.VMEM((SSD_GROUPS, SSD_STATE, SSD_INNER // SSD_GROUPS), F32)],
        compiler_params=_cparams(("arbitrary", "arbitrary")),
        name="ssd",
    )(z, xbc, misc, conv_w, conv_b.reshape(1, -1), pad128(dt_bias), pad128(a_log),
      jnp.repeat(d_skip, SSD_HEAD_DIM).reshape(1, -1), norm_g.reshape(1, -1), jnp.asarray(e, BF16))


MLA_D = LANES
MLA_HEADS_PER_STEP = 8
_R1 = MLA_NOPE
_R2 = MLA_NOPE + MLA_ROPE // 2


def _mla_prep_kernel(qa_ref, kva_ref, misc_ref, pos_ref, gq_ref, gkv_ref, wq_ref, wqs_ref, wk_ref, wv_ref,
                     pk_ref, pks_ref, invf_ref, sgn_ref, q_ref, k_ref, v_ref):
    ang = pos_ref[...].astype(F32) * invf_ref[...]
    cos = jnp.cos(ang)
    sin = jnp.sin(ang) * sgn_ref[...]
    nq = _rms(qa_ref[...].astype(F32), gq_ref[...]).astype(BF16)
    nkv = _rms(kva_ref[...].astype(F32), gkv_ref[...]).astype(BF16)
    misc = misc_ref[...]
    k_rot = _dot_01_right(misc, pk_ref[...], 3) * cos + _dot_01_right(misc, pks_ref[...], 3) * sin
    for h in range(MLA_HEADS):
        qh = _dot(nq, wq_ref[h]) * cos + _dot(nq, wqs_ref[h]) * sin
        q_ref[0, h] = (qh * ((MLA_NOPE + MLA_ROPE) ** -0.5 * LOG2E)).astype(q_ref.dtype)
        k_ref[0, h] = (_dot(nkv, wk_ref[h]) + k_rot).astype(k_ref.dtype)
        v_ref[0, h] = _dot(nkv, wv_ref[h]).astype(v_ref.dtype)


def _mla_prep(q_a, kv_a, misc, positions, gq, gkv, w_q_b, w_kv_b, bsz, seq, tm=512):
    m = bsz * seq
    half = MLA_ROPE // 2
    dq = MLA_NOPE + MLA_ROPE
    wq3 = w_q_b.reshape(MLA_Q_RANK, MLA_HEADS, dq).transpose(1, 0, 2)
    zq = jnp.zeros((MLA_HEADS, MLA_Q_RANK, MLA_D - dq), F32)
    wq = jnp.concatenate([wq3, zq], axis=-1).astype(BF16)
    wqs = jnp.concatenate([jnp.zeros((MLA_HEADS, MLA_Q_RANK, MLA_NOPE), F32), wq3[..., _R2:dq], wq3[..., _R1:_R2], zq],
                          axis=-1).astype(BF16)
    wkv3 = w_kv_b.reshape(MLA_KV_RANK, MLA_HEADS, MLA_NOPE + MLA_V).transpose(1, 0, 2)
    wk = jnp.concatenate([wkv3[..., :MLA_NOPE], jnp.zeros((MLA_HEADS, MLA_KV_RANK, MLA_D - MLA_NOPE), F32)],
                         axis=-1).astype(BF16)
    wv = wkv3[..., MLA_NOPE:].astype(BF16)
    pk = np.zeros((LANES, MLA_D), np.float32)
    pks = np.zeros((LANES, MLA_D), np.float32)
    invf = np.zeros((1, MLA_D), np.float32)
    sgn = np.zeros((1, MLA_D), np.float32)
    inv = (ROPE_THETA ** (-np.arange(0, MLA_ROPE, 2) / MLA_ROPE)).astype(np.float32)
    for i in range(half):
        pk[SSD_HEADS + i, _R1 + i] = 1.0
        pk[SSD_HEADS + half + i, _R2 + i] = 1.0
        pks[SSD_HEADS + half + i, _R1 + i] = 1.0
        pks[SSD_HEADS + i, _R2 + i] = 1.0
        invf[0, _R1 + i] = invf[0, _R2 + i] = inv[i]
        sgn[0, _R1 + i], sgn[0, _R2 + i] = -1.0, 1.0
    tpb = seq // tm
    c2 = lambda i: (0, 0)
    c3 = lambda i: (0, 0, 0)
    hm = lambda i: (i // tpb, 0, i % tpb, 0)
    return pl.pallas_call(
        _mla_prep_kernel,
        grid=(m // tm,),
        in_specs=[pl.BlockSpec((tm, MLA_Q_RANK), lambda i: (i, 0)),
                  pl.BlockSpec((tm, MLA_KV_RANK), lambda i: (i, 0)),
                  pl.BlockSpec((tm, LANES), lambda i: (i, 0)),
                  pl.BlockSpec((tm, 1), lambda i: (i, 0)),
                  pl.BlockSpec((1, MLA_Q_RANK), c2),
                  pl.BlockSpec((1, MLA_KV_RANK), c2),
                  pl.BlockSpec((MLA_HEADS, MLA_Q_RANK, MLA_D), c3),
                  pl.BlockSpec((MLA_HEADS, MLA_Q_RANK, MLA_D), c3),
                  pl.BlockSpec((MLA_HEADS, MLA_KV_RANK, MLA_D), c3),
                  pl.BlockSpec((MLA_HEADS, MLA_KV_RANK, MLA_V), c3),
                  pl.BlockSpec((LANES, MLA_D), c2),
                  pl.BlockSpec((LANES, MLA_D), c2),
                  pl.BlockSpec((1, MLA_D), c2),
                  pl.BlockSpec((1, MLA_D), c2)],
        out_specs=[pl.BlockSpec((1, MLA_HEADS, tm, MLA_D), hm),
                   pl.BlockSpec((1, MLA_HEADS, tm, MLA_D), hm),
                   pl.BlockSpec((1, MLA_HEADS, tm, MLA_V), hm)],
        out_shape=[jax.ShapeDtypeStruct((bsz, MLA_HEADS, seq, MLA_D), BF16),
                   jax.ShapeDtypeStruct((bsz, MLA_HEADS, seq, MLA_D), BF16),
                   jax.ShapeDtypeStruct((bsz, MLA_HEADS, seq, MLA_V), BF16)],
        compiler_params=_cparams(("arbitrary",)),
        name="mla_prep",
    )(q_a, kv_a, misc, positions.reshape(m, 1), gq.reshape(1, -1), gkv.reshape(1, -1), wq, wqs, wk, wv,
      jnp.asarray(pk, BF16), jnp.asarray(pks, BF16), jnp.asarray(invf), jnp.asarray(sgn))


_FIRST, _LAST, _MASKED = 1, 2, 4
AUG_TK = 512
AUG_POS = NSA_HEAD_DIM
AUG_SEL = AUG_POS + 4
AUG_NSEL = AUG_TK // NSA_SEL_BLOCK
SOFTMAX_ROWS = 64


def _flash_schedule(seq, tq, tk, window):
    qi_l, kb_l, fl_l = [], [], []
    for qi in range(seq // tq):
        q0, q1 = qi * tq, qi * tq + tq - 1
        first = q0 // tk
        lo = 0 if window is None else max(0, q0 - window + 1) // tk
        blocks = [first] + [b for b in range(lo, q1 // tk + 1) if b != first]
        for n, kb in enumerate(blocks):
            k0, k1 = kb * tk, kb * tk + tk - 1
            masked = k1 > q0 or (window is not None and q1 - k0 >= window)
            qi_l.append(qi)
            kb_l.append(kb)
            fl_l.append((_FIRST if n == 0 else 0) | (_LAST if n == len(blocks) - 1 else 0) | (_MASKED if masked else 0))
    return [jnp.asarray(np.asarray(t, np.int32)) for t in (qi_l, kb_l, fl_l)]


def _flash_kernel(qi_ref, kb_ref, fl_ref, *refs, hg, tq, tk, dv, slopes, window, has_sel, kv_per_head):
    if has_sel:
        q_ref, k_ref, v_ref, selb_ref, o_ref, m_scr, l_scr, alpha_scr, acc_scr, s_scr, p_scr, bias_scr = refs
    else:
        q_ref, k_ref, v_ref, o_ref, m_scr, l_scr, alpha_scr, acc_scr, s_scr, p_scr, bias_scr = refs
    g = pl.program_id(1)
    st = pl.program_id(2)
    qi, kb, fl = qi_ref[st], kb_ref[st], fl_ref[st]
    masked = (fl & _MASKED) != 0
    rows = hg * tq
    rc = SOFTMAX_ROWS

    @pl.when((fl & _FIRST) != 0)
    def _():
        m_scr[...] = jnp.full_like(m_scr, NEG_INF)
        l_scr[...] = jnp.zeros_like(l_scr)
        acc_scr[...] = jnp.zeros_like(acc_scr)

    def mask_bias():
        dist = (qi * tq + lax.broadcasted_iota(jnp.int32, (tq, tk), 0)) - \
               (kb * tk + lax.broadcasted_iota(jnp.int32, (tq, tk), 1))
        ok = dist >= 0
        if window is not None:
            ok = ok & (dist < window)
        return jnp.where(ok, 0.0, NEG_INF)

    def step(with_mask):
        with_bias = with_mask
        if with_mask:
            bias_scr[...] = mask_bias()
        q_sel = None
        if has_sel:
            src = lax.broadcasted_iota(jnp.int32, (LANES, LANES), 0)
            dst = lax.broadcasted_iota(jnp.int32, (LANES, LANES), 1)
            place = jnp.where(src - kb * AUG_NSEL == dst - AUG_SEL,
                              jnp.where(dst >= AUG_SEL, jnp.where(dst < AUG_SEL + AUG_NSEL, 1.0, 0.0), 0.0), 0.0)
            q_sel = _dot(selb_ref[0, 0], place.astype(BF16)).astype(BF16)
        kv = (lambda ref, h: ref[0, 0, h]) if kv_per_head else (lambda ref, h: ref[0, 0])
        for h in range(hg):
            q = q_ref[0, 0, h] if q_sel is None else q_ref[0, 0, h] + q_sel
            s_scr[h * tq:(h + 1) * tq, :] = _dot_nt(q, kv(k_ref, h))
        shift = [0.0] * hg
        if slopes is not None:
            shift = [(_by_group(g, slopes, h) * LOG2E) * (kb * tk - qi * tq).astype(F32) for h in range(hg)]
        for h in range(hg):
            for c in range(tq // rc):
                rs = slice(h * tq + c * rc, h * tq + (c + 1) * rc)
                s = s_scr[rs, :]
                if with_bias:
                    s = s + bias_scr[c * rc:(c + 1) * rc, :]
                    s_scr[rs, :] = s
                m_prev = m_scr[rs, :]
                m_next = jnp.maximum(m_prev, jnp.max(s, axis=1, keepdims=True) + shift[h])
                alpha_scr[rs, :] = jnp.exp2(m_prev - m_next)
                m_scr[rs, :] = m_next
        for h in range(hg):
            for c in range(tq // rc):
                rs = slice(h * tq + c * rc, h * tq + (c + 1) * rc)
                p = jnp.exp2(s_scr[rs, :] - jnp.tile(m_scr[rs, :] - shift[h], (1, tk // LANES)))
                alpha = alpha_scr[rs, :]
                l_scr[rs, :] = alpha * l_scr[rs, :] + jnp.sum(p, axis=1, keepdims=True)
                acc_scr[rs, :] = acc_scr[rs, :] * alpha[:, :dv]
                p_scr[rs, :] = p.astype(BF16)
        for h in range(hg):
            hs = slice(h * tq, (h + 1) * tq)
            acc_scr[hs, :] += _dot(p_scr[hs, :], kv(v_ref, h))

    pl.when(masked)(lambda: step(True))
    pl.when(jnp.logical_not(masked))(lambda: step(False))

    @pl.when((fl & _LAST) != 0)
    def _():
        out = acc_scr[...] / jnp.maximum(l_scr[:, :dv], 1e-30)
        o_ref[0] = jnp.concatenate([out[h * tq:(h + 1) * tq] for h in range(hg)], axis=-1).astype(o_ref.dtype)


def _flash(q, k, v, selb, *, tq, tk, slopes=None, window=None, out_dtype=F32):
    bsz, ng, hg, seq, d = q.shape
    dv = v.shape[-1]
    kv_per_head = k.ndim == 5
    sched = _flash_schedule(seq, tq, tk, window)
    rows = hg * tq
    assert slopes is None or (tk == AUG_TK and d == LANES and k.shape[-1] == LANES)
    if kv_per_head:
        kv_spec = lambda w: pl.BlockSpec((1, 1, hg, tk, w), lambda b, g, s, qi, kb, fl: (b, g, 0, kb[s], 0))
    else:
        kv_spec = lambda w: pl.BlockSpec((1, 1, tk, w), lambda b, g, s, qi, kb, fl: (b, g, kb[s], 0))
    in_specs = [pl.BlockSpec((1, 1, hg, tq, d), lambda b, g, s, qi, kb, fl: (b, g, 0, qi[s], 0)),
                kv_spec(d), kv_spec(dv)]
    args = [q, k, v]
    if selb is not None:
        in_specs.append(pl.BlockSpec((1, 1, tq, LANES), lambda b, g, s, qi, kb, fl: (b, g, qi[s], 0)))
        args.append(selb)
    return pl.pallas_call(
        functools.partial(_flash_kernel, hg=hg, tq=tq, tk=tk, dv=dv, slopes=slopes, window=window,
                          has_sel=selb is not None, kv_per_head=kv_per_head),
        grid_spec=pltpu.PrefetchScalarGridSpec(
            num_scalar_prefetch=3,
            grid=(bsz, ng, int(sched[0].shape[0])),
            in_specs=in_specs,
            out_specs=pl.BlockSpec((1, tq, hg * dv), lambda b, g, s, qi, kb, fl: (b, qi[s], g)),
            scratch_shapes=[pltpu.VMEM((rows, LANES), F32), pltpu.VMEM((rows, LANES), F32),
                            pltpu.VMEM((rows, LANES), F32),
                            pltpu.VMEM((rows, dv), F32), pltpu.VMEM((rows, tk), F32),
                            pltpu.VMEM((rows, tk), BF16), pltpu.VMEM((tq, tk), F32)]),
        out_shape=jax.ShapeDtypeStruct((bsz, seq, ng * hg * dv), out_dtype),
        compiler_params=_cparams(("arbitrary", "arbitrary", "arbitrary")),
        name="flash",
    )(*sched, *args)


def _gla_kernel(q_ref, k_ref, v_ref, gg_ref, misc_ref, w2_ref, bgk_ref, ng_ref, o_ref, st_scr):
    L = GLA_CHUNK

    @pl.when(pl.program_id(1) == 0)
    def _():
        st_scr[...] = jnp.zeros_like(st_scr)

    zg = _dot3(misc_ref[...], w2_ref[...]) + bgk_ref[...]
    log_a = -_softplus(-zg) * (1.0 / GLA_GATE_NORM)
    tril = _tril(L)
    bc_all = _dot_01_left(jnp.where(tril, 1.0, 0.0).astype(BF16), log_a, 3)
    row = lax.broadcasted_iota(jnp.int32, (L, GLA_DK), 0)
    for h in range(GLA_HEADS):
        kc, vc = slice(h * GLA_DK, (h + 1) * GLA_DK), slice(h * GLA_DV, (h + 1) * GLA_DV)
        bc = bc_all[:, kc]
        q = q_ref[:, kc] * (GLA_DK ** -0.5)
        k = k_ref[:, kc]
        v = v_ref[:, vc].astype(F32)
        st = st_scr[h]
        o = _dot_nt((q * jnp.exp(bc)).astype(BF16), st.astype(BF16))
        att = []
        for i in range(L // GLA_SUB):
            r0 = i * GLA_SUB
            ref = bc[r0:r0 + 1, :]
            qi = q[r0:r0 + GLA_SUB] * jnp.exp(bc[r0:r0 + GLA_SUB] - ref)
            ki = k * jnp.exp(jnp.where(row < r0 + GLA_SUB, ref - bc, 0.0))
            att.append(_dot_nt(qi.astype(BF16), ki.astype(BF16)))
        att = jnp.where(tril, jnp.concatenate(att, axis=0), 0.0)
        o = o + _dot(att.astype(BF16), v.astype(BF16))
        b_last = bc[L - 1:L, :]
        kd = (k * jnp.exp(b_last - bc)).astype(BF16)
        st_scr[h] = st * jnp.exp(b_last) + _dot(v.T.astype(BF16), kd)
        o_ref[:, vc] = (_rms(o, ng_ref[...]) * _silu(gg_ref[:, vc].astype(F32))).astype(o_ref.dtype)


def _gla(gq, gk, gv, gg, misc, w_gk2, b_gk, norm_g, bsz, seq):
    L = GLA_CHUNK
    nc = seq // L
    m = bsz * seq
    nk, nv = GLA_HEADS * GLA_DK, GLA_HEADS * GLA_DV
    w2 = jnp.zeros((LANES, nk), F32).at[:GLA_GATE_RANK].set(w_gk2)
    row = lambda b, c: (b * nc + c, 0)
    const = lambda b, c: (0, 0)
    return pl.pallas_call(
        _gla_kernel,
        grid=(bsz, nc),
        in_specs=[pl.BlockSpec((L, nk), row),
                  pl.BlockSpec((L, nk), row),
                  pl.BlockSpec((L, nv), row),
                  pl.BlockSpec((L, nv), row),
                  pl.BlockSpec((L, LANES), row),
                  pl.BlockSpec((LANES, nk), const),
                  pl.BlockSpec((1, nk), const),
                  pl.BlockSpec((1, GLA_DV), const)],
        out_specs=pl.BlockSpec((L, nv), row),
        out_shape=jax.ShapeDtypeStruct((m, nv), BF16),
        scratch_shapes=[pltpu.VMEM((GLA_HEADS, GLA_DV, GLA_DK), F32)],
        compiler_params=_cparams(("arbitrary", "arbitrary")),
        name="gla",
    )(gq, gk, gv, gg, misc, w2, b_gk.reshape(1, -1), norm_g.reshape(1, -1))


def _cmp_kernel(z_ref, pa_ref, pb_ref, wa_ref, wb_ref, w2_ref, o_ref, *, prec):
    z = z_ref[0]
    if prec is None:
        cast = lambda t: t.astype(BF16)
    else:
        cast = lambda t: t
    first = _dot(cast(z + pa_ref[...]), cast(wa_ref[...]), prec)
    second = _dot(cast(z + pb_ref[...]), cast(wb_ref[...]), prec)
    n = first.shape[0]
    hid = _silu(first + pltpu.roll(second, n - 1, 0))
    for g in range(NSA_GROUPS):
        o_ref[0, g] = _dot(cast(hid[:, g * NSA_CMP_HIDDEN:(g + 1) * NSA_CMP_HIDDEN]), cast(w2_ref[...]), prec)


def _compress(z, cmp_pos, w1, w2, bsz, seq, prec):
    nb = seq // NSA_CMP_STRIDE
    width = NSA_CMP_STRIDE * NSA_GROUPS * NSA_HEAD_DIM
    zr = z.reshape(bsz, nb, width)
    per = NSA_CMP_BLOCK // NSA_CMP_STRIDE
    eye = jnp.eye(NSA_GROUPS, dtype=F32)
    wbig = jnp.einsum("ldj,gh->lgdhj", w1.reshape(NSA_CMP_BLOCK, NSA_HEAD_DIM, NSA_CMP_HIDDEN), eye)
    wbig = wbig.reshape(per, width, NSA_GROUPS * NSA_CMP_HIDDEN)
    posb = jnp.broadcast_to(cmp_pos[:, None, :], (NSA_CMP_BLOCK, NSA_GROUPS, NSA_HEAD_DIM)).reshape(per, 1, width)
    assert per == 2
    c2 = lambda b: (0, 0)
    return pl.pallas_call(
        functools.partial(_cmp_kernel, prec=prec),
        grid=(bsz,),
        in_specs=[pl.BlockSpec((1, nb, width), lambda b: (b, 0, 0)),
                  pl.BlockSpec((1, width), c2), pl.BlockSpec((1, width), c2),
                  pl.BlockSpec((width, NSA_GROUPS * NSA_CMP_HIDDEN), c2),
                  pl.BlockSpec((width, NSA_GROUPS * NSA_CMP_HIDDEN), c2),
                  pl.BlockSpec((NSA_CMP_HIDDEN, NSA_HEAD_DIM), c2)],
        out_specs=pl.BlockSpec((1, NSA_GROUPS, nb, NSA_HEAD_DIM), lambda b: (b, 0, 0, 0)),
        out_shape=jax.ShapeDtypeStruct((bsz, NSA_GROUPS, nb, NSA_HEAD_DIM), F32),
        compiler_params=_cparams(("arbitrary",)),
        name="nsa_compress",
    )(zr, posb[0], posb[1], wbig[0], wbig[1], w2)


def _nsa_select_kernel(q_ref, kc_ref, vc_ref, ovt_ref, oc_ref, selb_ref, qs_ref, *, tq, n_sel, slopes, scale):
    qi = pl.program_id(1)
    ncmp = kc_ref.shape[2]
    nslc = ovt_ref.shape[0]
    qpos_c = qi * tq + lax.broadcasted_iota(jnp.int32, (ncmp, tq), 1)
    cmp_end = lax.broadcasted_iota(jnp.int32, (ncmp, tq), 0) * NSA_CMP_STRIDE + (NSA_CMP_BLOCK - 1)
    dist = (qpos_c - cmp_end).astype(F32)
    dist = jnp.where(dist >= 0.0, dist, MASK_DIST)
    dh = NSA_HEAD_DIM
    blk = lax.broadcasted_iota(jnp.int32, (nslc, tq), 0)
    qpos = qi * tq + lax.broadcasted_iota(jnp.int32, (nslc, tq), 1)
    forced = (blk == 0) | (blk == (qpos >> _SEL_SHIFT))
    avail = blk * NSA_SEL_BLOCK <= qpos
    blkf = blk.astype(F32)
    for g in range(NSA_GROUPS):
        kc_hi, kc_lo = _split(kc_ref[0, g], 2)
        vc = vc_ref[0, g].astype(BF16)
        o_c = []
        psum = jnp.zeros((ncmp, tq), F32)
        for h in range(NSA_HG):
            col = (g * NSA_HG + h) * dh
            q = q_ref[:, col:col + dh] * (scale * LOG2E)
            c = jnp.full((1, LANES - dh), slopes[g][h] * LOG2E, F32)
            c_hi = c.astype(BF16).astype(F32)
            c_lo = c - c_hi
            lane = lax.broadcasted_iota(jnp.int32, (1, LANES - dh), 1)
            feat = jnp.where(lane == 0, 16.0 * c_hi, jnp.where(lane == 1, c_hi, jnp.where(
                lane == 2, 16.0 * c_lo, jnp.where(lane == 3, c_lo, 0.0))))
            qs_ref[0, g, h] = jnp.concatenate([q, jnp.broadcast_to(feat, (tq, LANES - dh))],
                                              axis=-1).astype(qs_ref.dtype)
            q_hi, q_lo = _split(q, 2)
            qk = _dot_nt(kc_hi, q_hi) + (_dot_nt(kc_hi, q_lo) + _dot_nt(kc_lo, q_hi))
            s = qk - (slopes[g][h] * LOG2E) * dist
            mx = jnp.max(s, axis=0, keepdims=True)
            ex = jnp.exp2(s - mx)
            den = jnp.maximum(jnp.sum(ex, axis=0, keepdims=True), 1e-30)
            p = ex * jnp.where(mx > NEG_INF, 1.0 / den, 0.0)
            o_c.append(lax.dot_general(p.astype(BF16), vc, (((0,), (0,)), ((), ())), preferred_element_type=F32))
            psum = psum + p
        oc_ref[:, g * NSA_HG * dh:(g + 1) * NSA_HG * dh] = jnp.concatenate(o_c, axis=-1)
        imp = _dot_01_left(ovt_ref[...], psum, 2)
        imp = jnp.where(forced, NSA_FORCE, jnp.where(avail, imp, -1.0))
        selb = jnp.full((nslc, tq), NEG_INF, F32)
        for _ in range(n_sel):
            mx = jnp.max(imp, axis=0, keepdims=True)
            first = jnp.min(jnp.where(imp == mx, blkf, float(nslc)), axis=0, keepdims=True)
            one = blkf == first
            selb = jnp.where(one, 0.0, selb)
            imp = jnp.where(one, -2.0, imp)
        selb_ref[0, g] = selb.T.astype(selb_ref.dtype)


def _nsa_select(nq, k_cmp, v_cmp, bsz, seq, tq=512):
    ng, hg, dh = NSA_GROUPS, NSA_HG, NSA_HEAD_DIM
    ncmp = k_cmp.shape[2]
    nslc = seq // NSA_SEL_BLOCK
    n_sel = min(NSA_N_SEL, nslc)
    nt = seq // tq
    c_start = np.arange(ncmp) * NSA_CMP_STRIDE
    s_start = np.arange(nslc) * NSA_SEL_BLOCK
    ovt = ((c_start[None, :] < s_start[:, None] + NSA_SEL_BLOCK)
           & (c_start[None, :] + NSA_CMP_BLOCK > s_start[:, None])).astype(np.float32)
    ovt[:, (seq - NSA_CMP_BLOCK) // NSA_CMP_STRIDE + 1:] = 0.0
    return pl.pallas_call(
        functools.partial(_nsa_select_kernel, tq=tq, n_sel=n_sel, slopes=_alibi_table(NSA_HEADS, NSA_GROUPS),
                          scale=NSA_HEAD_DIM ** -0.5),
        grid=(bsz, nt),
        in_specs=[pl.BlockSpec((tq, ng * hg * dh), lambda b, i: (b * nt + i, 0)),
                  pl.BlockSpec((1, ng, ncmp, dh), lambda b, i: (b, 0, 0, 0)),
                  pl.BlockSpec((1, ng, ncmp, dh), lambda b, i: (b, 0, 0, 0)),
                  pl.BlockSpec((nslc, ncmp), lambda b, i: (0, 0))],
        out_specs=[pl.BlockSpec((tq, ng * hg * dh), lambda b, i: (b * nt + i, 0)),
                   pl.BlockSpec((1, ng, tq, nslc), lambda b, i: (b, 0, i, 0)),
                   pl.BlockSpec((1, ng, hg, tq, LANES), lambda b, i: (b, 0, 0, i, 0))],
        out_shape=[jax.ShapeDtypeStruct((bsz * seq, ng * hg * dh), F32),
                   jax.ShapeDtypeStruct((bsz, ng, seq, nslc), BF16),
                   jax.ShapeDtypeStruct((bsz, ng, hg, seq, LANES), BF16)],
        compiler_params=_cparams(("arbitrary", "arbitrary")),
        name="nsa_select",
    )(nq, k_cmp, v_cmp, jnp.asarray(ovt, BF16))


def _nsa_combine_kernel(oc_ref, os_ref, ow_ref, misc_ref, e_ref, o_ref):
    gates = 1.0 / (1.0 + jnp.exp(-misc_ref[...]))
    o = (_dot_01_right(gates, e_ref[0], 2) * oc_ref[...] + _dot_01_right(gates, e_ref[1], 2) * os_ref[...]
         + _dot_01_right(gates, e_ref[2], 2) * ow_ref[...])
    o_ref[...] = o.astype(o_ref.dtype)


def _nsa_combine(o_c, o_s, o_w, misc, tq=512):
    m, width = o_c.shape
    e = np.zeros((3, LANES, width), np.float32)
    for h in range(NSA_HEADS):
        for r in range(3):
            e[r, GLA_GATE_RANK + 3 * h + r, h * NSA_HEAD_DIM:(h + 1) * NSA_HEAD_DIM] = 1.0
    spec = pl.BlockSpec((tq, width), lambda i: (i, 0))
    return pl.pallas_call(
        _nsa_combine_kernel,
        grid=(m // tq,),
        in_specs=[spec, spec, spec, pl.BlockSpec((tq, LANES), lambda i: (i, 0)),
                  pl.BlockSpec((3, LANES, width), lambda i: (0, 0, 0))],
        out_specs=spec,
        out_shape=jax.ShapeDtypeStruct((m, width), BF16),
        compiler_params=_cparams(("arbitrary",)),
        name="nsa_combine",
    )(o_c, o_s, o_w, misc, jnp.asarray(e, BF16))


L0_SEGS = ((0, 1024, 0, BF16, False), (1024, 2560, 0, F32, False), (2560, 2688, 0, F32, False),
           (2688, 3072, 0, BF16, False), (3072, 3328, 0, BF16, False))
_G = NSA_GROUPS
L1_SEGS = ((0, 512, 0, F32, False), (512, 1024, 0, F32, False), (1024, 2048, 0, BF16, False),
           (2048, 3072, 0, BF16, False), (3072, 3584, 0, F32, False), (3584, 3712, 0, F32, False),
           (3712, 3840, 0, F32, False), (3840, 3968, _G, BF16, True), (3968, 4096, _G, BF16, False),
           (4096, 4224, _G, BF16, True), (4224, 4352, _G, BF16, False), (4352, 4480, 0, F32, False))


def _pack_w_in0(w):
    d = w.shape[0]
    a = SSD_INNER + SSD_CONV_CH
    dt = w[:, a:a + SSD_HEADS]
    qa = w[:, a + SSD_HEADS:a + SSD_HEADS + MLA_Q_RANK]
    kva = w[:, a + SSD_HEADS + MLA_Q_RANK:a + SSD_HEADS + MLA_Q_RANK + MLA_KV_RANK]
    kpe = w[:, a + SSD_HEADS + MLA_Q_RANK + MLA_KV_RANK:]
    pad = jnp.zeros((d, LANES - SSD_HEADS - MLA_ROPE), F32)
    return jnp.concatenate([w[:, :a], dt, kpe, pad, qa, kva], axis=1).astype(BF16)


def _pack_w_in1(w):
    d = w.shape[0]
    qk = 2 * GLA_HEADS * GLA_DK
    vv = GLA_HEADS * GLA_DV
    o = qk + vv
    glr = w[:, o:o + GLA_GATE_RANK]
    gg = w[:, o + GLA_GATE_RANK:o + GLA_GATE_RANK + vv]
    o2 = o + GLA_GATE_RANK + vv
    nsa = w[:, o2:o2 + NSA_HEADS * NSA_HEAD_DIM + 6 * NSA_GROUPS * NSA_HEAD_DIM]
    ngate = w[:, o2 + NSA_HEADS * NSA_HEAD_DIM + 6 * NSA_GROUPS * NSA_HEAD_DIM:]
    pad = jnp.zeros((d, LANES - GLA_GATE_RANK - 3 * NSA_HEADS), F32)
    return jnp.concatenate([w[:, :o], gg, nsa, glr, ngate, pad], axis=1).astype(BF16)


def _mixer0_parts(h_args, positions, bsz, seq, w_in, conv_w, conv_b, dt_bias, a_log, d_skip, ssm_norm_g,
                  q_a_norm_g, w_q_b, kv_a_norm_g, w_kv_b):
    z, xbc, misc, q_a, kv_a = _proj_in(*h_args, _pack_w_in0(w_in), L0_SEGS, seq)
    y_ssd = _ssd(z, xbc, misc, conv_w, conv_b, dt_bias, a_log, d_skip, ssm_norm_g, bsz, seq)
    q, k, v = _mla_prep(q_a, kv_a, misc, positions, q_a_norm_g, kv_a_norm_g, w_q_b, w_kv_b, bsz, seq)
    pair = lambda t: t.reshape(bsz, MLA_HEADS // MLA_HEADS_PER_STEP, MLA_HEADS_PER_STEP, seq, t.shape[-1])
    o = _flash(pair(q), pair(k), pair(v), None, tq=512, tk=512, out_dtype=BF16)
    return y_ssd, o.reshape(bsz * seq, MLA_HEADS * MLA_V)


def _mixer1_parts(h_args, bsz, seq, w_in, w_gk2, b_gk, gla_norm_g, cmp_pos, cmp_k_w1, cmp_k_w2, cmp_v_w1, cmp_v_w2):
    gq, gk, gv, gg, nq, kc, vc, ks, vs, kw, vw, misc = _proj_in(*h_args, _pack_w_in1(w_in), L1_SEGS, seq)
    o_gla = _gla(gq, gk, gv, gg, misc, w_gk2, b_gk, gla_norm_g, bsz, seq)
    k_cmp = _compress(kc, cmp_pos, cmp_k_w1, cmp_k_w2, bsz, seq, HI)
    v_cmp = _compress(vc, cmp_pos, cmp_v_w1, cmp_v_w2, bsz, seq, None)
    o_c, selb, qs = _nsa_select(nq, k_cmp, v_cmp, bsz, seq)
    if selb.shape[-1] < LANES:
        selb = jnp.pad(selb, ((0, 0), (0, 0), (0, 0), (0, LANES - selb.shape[-1])))
    slopes = _alibi_table(NSA_HEADS, NSA_GROUPS)
    width = NSA_HEADS * NSA_HEAD_DIM
    o_s = _flash(qs, ks, vs, selb, tq=512, tk=512, slopes=slopes).reshape(bsz * seq, width)
    o_w = _flash(qs, kw, vw, None, tq=NSA_WINDOW, tk=NSA_WINDOW, slopes=slopes, window=NSA_WINDOW)
    return o_gla, _nsa_combine(o_c, o_s, o_w.reshape(bsz * seq, width), misc)


def kernel(x, c, positions, l0_ada_w, l0_ada_b, l0_mix_pre_g, l0_mix_post_g, l0_w_in, l0_conv_w, l0_conv_b, l0_dt_bias, l0_a_log, l0_d_skip, l0_ssm_norm_g, l0_q_a_norm_g, l0_w_q_b, l0_kv_a_norm_g, l0_w_kv_b, l0_w_out, l0_ffn_pre_g, l0_ffn_post_g, l0_w_gate, l0_w_up, l0_w_down, l1_ada_w, l1_ada_b, l1_mix_pre_g, l1_mix_post_g, l1_w_in, l1_w_gk2, l1_b_gk, l1_gla_norm_g, l1_cmp_pos, l1_cmp_k_w1, l1_cmp_k_w2, l1_cmp_v_w1, l1_cmp_v_w2, l1_w_out, l1_ffn_pre_g, l1_ffn_post_g, l1_w_gate, l1_w_up, l1_w_down):
    bsz, seq, d = x.shape
    x2 = x.reshape(bsz * seq, d)

    def sublayers(x2, ada_w, ada_b, pre_m, post_m, mixer, w_out, pre_f, post_f, w_gate, w_up, w_down):
        shift_m, scale_m, gate_m, shift_f, scale_f, gate_f = _ada(c, ada_w, ada_b)
        a, b = mixer((x2, pre_m, scale_m, shift_m))
        ka = a.shape[1]
        x2 = _out_res(x2, a, b, w_out[:ka].astype(BF16), w_out[ka:].astype(BF16), post_m, gate_m, seq)
        return _ffn(x2, pre_f, scale_f, shift_f, w_gate.astype(BF16), w_up.astype(BF16), w_down.astype(BF16),
                    post_f, gate_f, seq)

    x2 = sublayers(
        x2, l0_ada_w, l0_ada_b, l0_mix_pre_g, l0_mix_post_g,
        lambda h: _mixer0_parts(h, positions, bsz, seq, l0_w_in, l0_conv_w, l0_conv_b, l0_dt_bias, l0_a_log,
                                l0_d_skip, l0_ssm_norm_g, l0_q_a_norm_g, l0_w_q_b, l0_kv_a_norm_g, l0_w_kv_b),
        l0_w_out, l0_ffn_pre_g, l0_ffn_post_g, l0_w_gate, l0_w_up, l0_w_down)
    x2 = sublayers(
        x2, l1_ada_w, l1_ada_b, l1_mix_pre_g, l1_mix_post_g,
        lambda h: _mixer1_parts(h, bsz, seq, l1_w_in, l1_w_gk2, l1_b_gk, l1_gla_norm_g, l1_cmp_pos,
                                l1_cmp_k_w1, l1_cmp_k_w2, l1_cmp_v_w1, l1_cmp_v_w2),
        l1_w_out, l1_ffn_pre_g, l1_ffn_post_g, l1_w_gate, l1_w_up, l1_w_down)
    return x2.reshape(bsz, seq, d)
```
